```python
import math
import jax, jax.numpy as jnp
from jax import lax
import numpy as np

D_MODEL = 2048
BATCH = 2
SEQ = 4096
DEPTH = 1

GRID_W = 64
CTX_LEN = 256
D_SSM = D_MODEL
SSM_HEADDIM = 64
SSM_HEADS = D_SSM // SSM_HEADDIM
SSM_GROUPS = 8
SSM_STATE = 128
SSM_CHUNK = 128
CONV_WIDTH = 3
D_XBC = D_SSM + 2 * SSM_GROUPS * SSM_STATE
POOL_WINDOWS = (2, 4, 8, 16)
D_POOL = D_MODEL // 2
POOL_GROUP = D_POOL // len(POOL_WINDOWS)
D_MIX = D_SSM + D_POOL
D_IN_PROJ = D_SSM + D_XBC + 2 * SSM_HEADS + D_POOL
N_EXPERTS = 256
TOP_K = 8
N_EXPERT_GROUPS = 8
TOPK_GROUPS = 4
D_EXPERT = D_MODEL // 4
D_SHARED = D_EXPERT
ROUTE_SCALE = 2.5
MOE_BLOCK = 128
N_MOD = 6
EPS = 1e-6

kernel_name = "hymba_ssd_pool_moe_dit_block"

F32 = jnp.float32


def rmsnorm(x, g):
    xf = x.astype(F32)
    y = xf * lax.rsqrt(jnp.mean(xf * xf, axis=-1, keepdims=True) + EPS)
    return (y * g.astype(F32)).astype(x.dtype)


def modulate_norm(x, g, shift, scale):
    return rmsnorm(x, g) * (1 + scale) + shift


def flip(t):
    return jnp.flip(t, axis=1)


def dwconv_centred(u, w, b):
    L = u.shape[1]
    p = CONV_WIDTH // 2
    up = jnp.pad(u, ((0, 0), (p, CONV_WIDTH - 1 - p), (0, 0)))
    return sum(up[:, k:k + L] * w[k] for k in range(CONV_WIDTH)) + b


def segsum(a):
    T = a.shape[-1]
    rep = jnp.broadcast_to(a[..., :, None], a.shape + (T,))
    cs = jnp.cumsum(jnp.where(jnp.tril(jnp.ones((T, T), bool), -1), rep, 0.0), axis=-2)
    return jnp.where(jnp.tril(jnp.ones((T, T), bool)), cs, -jnp.inf)


def ssd_scan(x, dt, a, bm, cm, h0, need_y):
    bsz, L, H, P = x.shape
    G, N = bm.shape[2], bm.shape[3]
    R = H // G
    nc, q = L // SSM_CHUNK, SSM_CHUNK
    xdt = (x.astype(F32) * dt[..., None]).reshape(bsz, nc, q, G, R, P)
    bq = bm.astype(F32).reshape(bsz, nc, q, G, N)
    da = jnp.moveaxis((dt * a).reshape(bsz, nc, q, G, R), 2, -1)
    a_cs = jnp.cumsum(da, axis=-1)
    decay_states = jnp.moveaxis(jnp.exp(a_cs[..., -1:] - a_cs), -1, 2)
    states = jnp.einsum('bclgn,bclgrp->bcgrpn', bq, xdt * decay_states[..., None])
    a_chunk = jnp.pad(a_cs[..., -1], ((0, 0), (1, 0), (0, 0), (0, 0)))
    decay_chunk = jnp.exp(segsum(jnp.moveaxis(a_chunk, 1, -1)))
    states = jnp.concatenate([h0.reshape(bsz, 1, G, R, P, N), states], axis=1)
    states = jnp.einsum('bgrzc,bcgrpn->bzgrpn', decay_chunk, states)
    h_final = states[:, -1].reshape(bsz, H, P, N)
    if not need_y:
        return None, h_final
    cq = cm.astype(F32).reshape(bsz, nc, q, G, N)
    cb = jnp.einsum('bclgn,bcsgn->bcgls', cq, bq)
    att = cb[:, :, :, None] * jnp.exp(segsum(da))
    y_diag = jnp.einsum('bcgrls,bcsgrp->bclgrp', att, xdt)
    y_off = jnp.einsum('bclgn,bcgrpn->bclgrp', cq, states[:, :-1]) * jnp.moveaxis(jnp.exp(a_cs), -1, 2)[..., None]
    y = (y_diag + y_off).reshape(bsz, L, H, P)
    return y.astype(x.dtype), h_final


def ssd_inputs(xbc, dt_raw, conv_w, conv_b, dt_bias):
    bsz, L = xbc.shape[:2]
    xbc = jax.nn.silu(dwconv_centred(xbc, conv_w, conv_b))
    xs, bm, cm = jnp.split(xbc, [D_SSM, D_SSM + SSM_GROUPS * SSM_STATE], axis=-1)
    dt = jax.nn.softplus(dt_raw.astype(F32).reshape(bsz, L, 2, SSM_HEADS) + dt_bias.astype(F32))
    return (xs.reshape(bsz, L, SSM_HEADS, SSM_HEADDIM),
            bm.reshape(bsz, L, SSM_GROUPS, SSM_STATE),
            cm.reshape(bsz, L, SSM_GROUPS, SSM_STATE),
            dt[:, :, 0], dt[:, :, 1])


def ssd_output(y_f, y_b_rev, xs, z, d_skip, g_ssd):
    bsz, L = xs.shape[:2]
    y = y_f + flip(y_b_rev) + xs * d_skip[:, None]
    return rmsnorm(y.reshape(bsz, L, D_SSM) * jax.nn.silu(z), g_ssd)


def box_mean(u, w, axis):
    L = u.shape[axis]
    cs = jnp.cumsum(u, axis=axis)
    cs = jnp.concatenate([jnp.zeros_like(lax.slice_in_dim(cs, 0, 1, axis=axis)), cs], axis=axis)
    t = jnp.arange(L)
    lo = jnp.clip(t - w // 2, 0, L)
    hi = jnp.clip(t - w // 2 + w, 0, L)
    total = jnp.take(cs, hi, axis=axis) - jnp.take(cs, lo, axis=axis)
    cnt = (hi - lo).astype(u.dtype).reshape((L,) + (1,) * (u.ndim - axis - 1))
    return total / cnt


def pool_mixer(u, pool_w, pool_scale, rows):
    bsz, L, _ = u.shape
    uf = u.astype(F32)
    diffs = []
    for g, w in enumerate(POOL_WINDOWS):
        ug = uf[..., g * POOL_GROUP:(g + 1) * POOL_GROUP]
        if rows is None:
            m = box_mean(ug, w, 1)
        else:
            v = ug.reshape(bsz, rows, GRID_W, POOL_GROUP)
            m = box_mean(box_mean(v, w, 1), w, 2).reshape(bsz, L, POOL_GROUP)
        diffs.append(m - ug)
    d = jnp.stack(diffs, axis=2).astype(u.dtype)
    y = jnp.einsum('blgc,gcd->blgd', d, pool_w).reshape(bsz, L, D_POOL)
    return y * pool_scale


def moe_ffn(h, w_router, router_bias, w_exp_gate, w_exp_up, w_exp_down, w_sh_gate, w_sh_up, w_sh_down):
    n_tok, d = h.shape
    scores = jax.nn.sigmoid((h @ w_router).astype(F32))
    biased = (scores + router_bias.astype(F32)).reshape(n_tok, N_EXPERT_GROUPS, -1)
    grp_score = jnp.sum(lax.top_k(biased, 2)[0], axis=-1)
    _, top_grp = lax.top_k(grp_score, TOPK_GROUPS)
    grp_keep = jnp.any(top_grp[..., None] == jnp.arange(N_EXPERT_GROUPS), axis=1)
    masked = jnp.where(grp_keep[..., None], biased, -jnp.inf).reshape(n_tok, N_EXPERTS)
    _, top_e = lax.top_k(masked, TOP_K)
    w_sel = jnp.take_along_axis(scores, top_e, axis=-1)
    w_sel = w_sel / jnp.sum(w_sel, axis=-1, keepdims=True) * ROUTE_SCALE
    n_asg = n_tok * TOP_K
    flat_e = top_e.reshape(-1)
    order = jnp.argsort(flat_e)
    sorted_e = flat_e[order]
    sorted_tok = (order // TOP_K).astype(jnp.int32)
    sorted_w = w_sel.reshape(-1)[order]
    counts = jnp.bincount(flat_e, length=N_EXPERTS)
    padded = (counts + MOE_BLOCK - 1) // MOE_BLOCK * MOE_BLOCK
    start = jnp.cumsum(counts) - counts
    pend = jnp.cumsum(padded)
    pstart = pend - padded
    dest = pstart[sorted_e] + jnp.arange(n_asg) - start[sorted_e]
    n_blocks = -(-(n_asg + N_EXPERTS * (MOE_BLOCK - 1)) // MOE_BLOCK)
    slots = n_blocks * MOE_BLOCK
    slot_tok = jnp.full((slots,), n_tok, jnp.int32).at[dest].set(sorted_tok)
    slot_w = jnp.zeros((slots,), F32).at[dest].set(sorted_w)
    block_e = jnp.minimum(jnp.searchsorted(pend, jnp.arange(n_blocks) * MOE_BLOCK, side='right'), N_EXPERTS - 1)
    h_pad = jnp.concatenate([h, jnp.zeros((1, d), h.dtype)], axis=0)

    def block_step(acc, blk):
        tok, wgt, e = blk
        xb = h_pad[tok]
        yb = (jax.nn.silu(xb @ w_exp_gate[e]) * (xb @ w_exp_up[e])) @ w_exp_down[e]
        return acc.at[tok].add(yb.astype(F32) * wgt[:, None]), None

    routed, _ = lax.scan(block_step, jnp.zeros((n_tok + 1, d), F32),
                         (slot_tok.reshape(n_blocks, MOE_BLOCK), slot_w.reshape(n_blocks, MOE_BLOCK), block_e))
    shared = (jax.nn.silu(h @ w_sh_gate) * (h @ w_sh_up)) @ w_sh_down
    return routed[:n_tok].astype(h.dtype) + shared


def layer_forward(x, xc, mod, mod_c, rows, is_last, g_mix, w_in, conv_w, conv_b, dt_bias, a_log, d_skip, g_ssd,
                  pool_w, pool_scale, w_out, g_ffn, w_router, router_bias, w_exp_gate, w_exp_up, w_exp_down,
                  w_sh_gate, w_sh_up, w_sh_down):
    bsz, seq, d = x.shape
    a = -jnp.exp(a_log.astype(F32))
    h0 = jnp.zeros((bsz, SSM_HEADS, SSM_HEADDIM, SSM_STATE), F32)
    cut = [D_SSM, D_SSM + D_XBC, D_SSM + D_XBC + 2 * SSM_HEADS]
    hc = modulate_norm(xc, g_mix, mod_c[0], mod_c[1])
    if is_last:
        proj_c = hc @ w_in[:, D_SSM:D_SSM + D_XBC + 2 * SSM_HEADS]
        xbc_c, dt_raw_c = jnp.split(proj_c, [D_XBC], axis=-1)
    else:
        z_c, xbc_c, dt_raw_c, u_pool_c = jnp.split(hc @ w_in, cut, axis=-1)
    xs_c, bm_c, cm_c, dtc_f, dtc_b = ssd_inputs(xbc_c, dt_raw_c, conv_w, conv_b, dt_bias)
    yc_f, hc_f = ssd_scan(xs_c, dtc_f, a[0], bm_c, cm_c, h0, not is_last)
    yc_b, hc_b = ssd_scan(flip(xs_c), flip(dtc_b), a[1], flip(bm_c), flip(cm_c), h0, not is_last)
    h = modulate_norm(x, g_mix, mod[:, 0], mod[:, 1])
    z, xbc, dt_raw, u_pool = jnp.split(h @ w_in, cut, axis=-1)
    xs, bm, cm, dt_f, dt_b = ssd_inputs(xbc, dt_raw, conv_w, conv_b, dt_bias)
    y_f, _ = ssd_scan(xs, dt_f, a[0], bm, cm, hc_f, True)
    y_b, _ = ssd_scan(flip(xs), flip(dt_b), a[1], flip(bm), flip(cm), hc_b, True)
    y_ssd = ssd_output(y_f, y_b, xs, z, d_skip, g_ssd)
    y_pool = pool_mixer(u_pool, pool_w, pool_scale, rows)
    x = x + mod[:, 2] * (jnp.concatenate([y_ssd, y_pool], axis=-1) @ w_out)
    h2 = modulate_norm(x, g_ffn, mod[:, 3], mod[:, 4]).reshape(-1, d)
    moe_args = (w_router, router_bias, w_exp_gate, w_exp_up, w_exp_down, w_sh_gate, w_sh_up, w_sh_down)
    if is_last:
        ffn = moe_ffn(h2, *moe_args)
        return x + mod[:, 5] * ffn.reshape(bsz, seq, d), None
    yc_ssd = ssd_output(yc_f, yc_b, xs_c, z_c, d_skip, g_ssd)
    yc_pool = pool_mixer(u_pool_c, pool_w, pool_scale, None)
    xc = xc + mod_c[2] * (jnp.concatenate([yc_ssd, yc_pool], axis=-1) @ w_out)
    h2c = modulate_norm(xc, g_ffn, mod_c[3], mod_c[4]).reshape(-1, d)
    ffn = moe_ffn(jnp.concatenate([h2, h2c], axis=0), *moe_args)
    n_lat = bsz * seq
    x = x + mod[:, 5] * ffn[:n_lat].reshape(bsz, seq, d)
    xc = xc + mod_c[5] * ffn[n_lat:].reshape(xc.shape)
    return x, xc


def setup_inputs(seed: int = 0) -> dict:
    key = jax.random.key(seed)
    ks = jax.random.split(key, 28)

    def nrm(k, shape, scale):
        return jax.random.normal(k, shape, F32) * scale

    col_scale = jnp.ones((D_IN_PROJ,), F32).at[D_SSM + D_XBC:D_SSM + D_XBC + 2 * SSM_HEADS].set(0.1)
    dt0 = jnp.exp(jax.random.uniform(ks[10], (DEPTH, 2, SSM_HEADS), F32, math.log(1e-3), math.log(1e-1)))
    return {
        "x": nrm(ks[0], (BATCH, SEQ, D_MODEL), 1.0),
        "c": nrm(ks[1], (BATCH, D_MODEL), 1.0),
        "ctx": nrm(ks[2], (BATCH, CTX_LEN, D_MODEL), 1.0),
        "c_ctx": nrm(ks[3], (D_MODEL,), 1.0),
        "w_ada": nrm(ks[4], (DEPTH, D_MODEL, N_MOD * D_MODEL), 0.5 * D_MODEL ** -0.5),
        "b_ada": nrm(ks[5], (DEPTH, N_MOD * D_MODEL), 0.02),
        "g_mix": 1.0 + nrm(ks[6], (DEPTH, D_MODEL), 0.02),
        "w_in": nrm(ks[7], (DEPTH, D_MODEL, D_IN_PROJ), D_MODEL ** -0.5) * col_scale,
        "conv_w": nrm(ks[8], (DEPTH, CONV_WIDTH, D_XBC), CONV_WIDTH ** -0.5),
        "conv_b": nrm(ks[9], (DEPTH, D_XBC), 0.02),
        "dt_bias": dt0 + jnp.log(-jnp.expm1(-dt0)),
        "a_log": jnp.log(jax.random.uniform(ks[11], (DEPTH, 2, SSM_HEADS), F32, 1.0, 16.0)),
        "d_skip": 1.0 + nrm(ks[12], (DEPTH, SSM_HEADS), 0.1),
        "g_ssd": 1.0 + nrm(ks[13], (DEPTH, D_SSM), 0.02),
        "pool_w": nrm(ks[14], (DEPTH, len(POOL_WINDOWS), POOL_GROUP, POOL_GROUP), POOL_GROUP ** -0.5),
        "pool_scale": 1.0 + nrm(ks[15], (DEPTH, D_POOL), 0.02),
        "w_out": nrm(ks[16], (DEPTH, D_MIX, D_MODEL), D_MIX ** -0.5),
        "g_ffn": 1.0 + nrm(ks[17], (DEPTH, D_MODEL), 0.02),
        "w_router": nrm(ks[18], (DEPTH, D_MODEL, N_EXPERTS), D_MODEL ** -0.5),
        "router_bias": nrm(ks[19], (DEPTH, N_EXPERTS), 0.01),
        "w_exp_gate": nrm(ks[20], (DEPTH, N_EXPERTS, D_MODEL, D_EXPERT), D_MODEL ** -0.5),
        "w_exp_up": nrm(ks[21], (DEPTH, N_EXPERTS, D_MODEL, D_EXPERT), D_MODEL ** -0.5),
        "w_exp_down": nrm(ks[22], (DEPTH, N_EXPERTS, D_EXPERT, D_MODEL), D_EXPERT ** -0.5),
        "w_sh_gate": nrm(ks[23], (DEPTH, D_MODEL, D_SHARED), D_MODEL ** -0.5),
        "w_sh_up": nrm(ks[24], (DEPTH, D_MODEL, D_SHARED), D_MODEL ** -0.5),
        "w_sh_down": nrm(ks[25], (DEPTH, D_SHARED, D_MODEL), D_SHARED ** -0.5),
        "g_final": 1.0 + nrm(ks[26], (D_MODEL,), 0.02),
    }


def reference(x, c, ctx, c_ctx, w_ada, b_ada, g_mix, w_in, conv_w, conv_b, dt_bias, a_log, d_skip, g_ssd,
              pool_w, pool_scale, w_out, g_ffn, w_router, router_bias, w_exp_gate, w_exp_up, w_exp_down,
              w_sh_gate, w_sh_up, w_sh_down, g_final):
    bsz, seq, d = x.shape
    rows = seq // GRID_W
    xc = ctx
    sc = jax.nn.silu(c)
    scc = jax.nn.silu(c_ctx)
    for i in range(DEPTH):
        is_last = i == DEPTH - 1
        mod = (sc @ w_ada[i] + b_ada[i]).reshape(bsz, N_MOD, 1, d)
        mod_c = (scc @ w_ada[i] + b_ada[i]).reshape(N_MOD, 1, 1, d)
        x, xc = layer_forward(x, xc, mod, mod_c, rows, is_last, g_mix[i], w_in[i], conv_w[i], conv_b[i],
                              dt_bias[i], a_log[i], d_skip[i], g_ssd[i], pool_w[i], pool_scale[i], w_out[i],
                              g_ffn[i], w_router[i], router_bias[i], w_exp_gate[i], w_exp_up[i], w_exp_down[i],
                              w_sh_gate[i], w_sh_up[i], w_sh_down[i])
    return rmsnorm(x, g_final)
```

```python
import functools

import jax
import jax.numpy as jnp
from jax import lax
from jax.experimental import pallas as pl
from jax.experimental.pallas import tpu as pltpu

F32 = jnp.float32
BF16 = jnp.bfloat16
I32 = jnp.int32
U32 = jnp.uint32

EPS = 1e-6
GRID_W = 64
SSM_HEADDIM = 64
SSM_GROUPS = 8
SSM_STATE = 128
SSD_CHUNK = 128
POOL_WINDOWS = (2, 4, 8, 16)
TOP_K = 8
N_EXPERT_GROUPS = 8
TOPK_GROUPS = 4
ROUTE_SCALE = 2.5
N_MOD = 6
MOE_BLOCK = 512
GATHER_GROUP = 64
V7X_VMEM_LIMIT = 56 * 1024 * 1024


def _cparams(sem, vmem=V7X_VMEM_LIMIT):
    return pltpu.CompilerParams(dimension_semantics=sem, vmem_limit_bytes=vmem)


def _sigmoid(x):
    return 1.0 / (1.0 + jnp.exp(-x))


def _silu(x):
    return x * _sigmoid(x)


def _split2(x):
    hi = x.astype(BF16)
    lo = (x - hi.astype(F32)).astype(BF16)
    return hi, lo


def _dot(a, b):
    return jnp.dot(a, b, preferred_element_type=F32)


def _dot_nt(a, b):
    return lax.dot_general(a, b, (((1,), (1,)), ((), ())), preferred_element_type=F32)


def _dot_tn(a, b):
    return lax.dot_general(a, b, (((0,), (0,)), ((), ())), preferred_element_type=F32)


def _dot3(a, b):
    a_hi, a_lo = _split2(a)
    b_hi, b_lo = _split2(b)
    return _dot(a_hi, b_hi) + _dot(a_lo, b_hi) + _dot(a_hi, b_lo)


def _ada_kernel(c_ref, w_ref, b_ref, o_ref):
    c = c_ref[...]
    o_ref[...] = _dot3(_silu(c), w_ref[...]) + b_ref[...]


def _ada(cvec, w_ada, b_ada, tn=1024):
    d, n = w_ada.shape
    return pl.pallas_call(
        _ada_kernel,
        grid=(n // tn,),
        in_specs=[pl.BlockSpec((8, d), lambda j: (0, 0)),
                  pl.BlockSpec((d, tn), lambda j: (0, j)),
                  pl.BlockSpec((1, tn), lambda j: (0, j))],
        out_specs=pl.BlockSpec((8, tn), lambda j: (0, j)),
        out_shape=jax.ShapeDtypeStruct((8, n), F32),
        compiler_params=_cparams(("parallel",)),
        name="ada",
    )(cvec, w_ada, b_ada.reshape(1, n))


def _inproj_kernel(x_ref, g_ref, sh_ref, sc_ref, w_ref, wdt_ref, o_ref, dt_ref, h_scr):
    @pl.when(pl.program_id(1) == 0)
    def _():
        x = x_ref[...]
        ms = jnp.mean(x * x, axis=-1, keepdims=True)
        y = x * lax.rsqrt(ms + EPS) * g_ref[...]
        h = y * (1.0 + sc_ref[0]) + sh_ref[0]
        h_hi, h_lo = _split2(h)
        h_scr[...] = h_hi
        w_hi, w_lo = _split2(wdt_ref[...])
        dt_ref[...] = _dot(h_hi, w_hi) + _dot(h_lo, w_hi) + _dot(h_hi, w_lo)

    o_ref[...] = _dot(h_scr[...], w_ref[...]).astype(o_ref.dtype)


def _inproj(x2d, g, shift, scale, w_main, w_dt, rows_per_batch, tm, tn):
    n, d = x2d.shape
    nc = w_main.shape[1]
    tpb = rows_per_batch // tm
    return pl.pallas_call(
        _inproj_kernel,
        grid=(n // tm, nc // tn),
        in_specs=[pl.BlockSpec((tm, d), lambda i, j: (i, 0)),
                  pl.BlockSpec((1, d), lambda i, j: (0, 0)),
                  pl.BlockSpec((1, 1, d), lambda i, j: (i // tpb, 0, 0)),
                  pl.BlockSpec((1, 1, d), lambda i, j: (i // tpb, 0, 0)),
                  pl.BlockSpec((d, tn), lambda i, j: (0, j)),
                  pl.BlockSpec((d, 128), lambda i, j: (0, 0))],
        out_specs=[pl.BlockSpec((tm, tn), lambda i, j: (i, j)),
                   pl.BlockSpec((tm, 128), lambda i, j: (i, 0))],
        out_shape=[jax.ShapeDtypeStruct((n, nc), BF16),
                   jax.ShapeDtypeStruct((n, 128), F32)],
        scratch_shapes=[pltpu.VMEM((tm, d), BF16)],
        compiler_params=_cparams(("parallel", "arbitrary")),
        name="inproj",
    )(x2d, g.reshape(1, d), shift, scale, w_main, w_dt)


def _expand_heads(v, base, width):
    t = v.shape[0]
    lane = lax.broadcasted_iota(I32, (t, 4 * width), 1)
    out = jnp.broadcast_to(v[:, base + 3:base + 4], (t, 4 * width))
    for j in (2, 1, 0):
        out = jnp.where(lane < (j + 1) * width, v[:, base + j:base + j + 1], out)
    return out


def _conv_silu(src_ref, w_ref, b_ref, dst_ref, length, tile):
    c = src_ref.shape[-1]
    w = w_ref[...]
    b = b_ref[...]
    rid = lax.broadcasted_iota(I32, (tile, c), 0)
    for r0 in range(0, length, tile):
        cur = src_ref[0, r0:r0 + tile, :].astype(F32)
        if r0 == 0:
            prev_row = jnp.zeros((1, c), F32)
        else:
            prev_row = src_ref[0, r0 - 16:r0, :].astype(F32)[15:16, :]
        if r0 + tile == length:
            next_row = jnp.zeros((1, c), F32)
        else:
            next_row = src_ref[0, r0 + tile:r0 + tile + 16, :].astype(F32)[0:1, :]
        up = jnp.where(rid == 0, prev_row, pltpu.roll(cur, 1, 0))
        dn = jnp.where(rid == tile - 1, next_row, pltpu.roll(cur, tile - 1, 0))
        o = up * w[0:1, :] + cur * w[1:2, :] + dn * w[2:3, :] + b
        dst_ref[r0:r0 + tile, :] = _silu(o).astype(dst_ref.dtype)


def _softplus(x):
    return jnp.maximum(x, 0.0) + jnp.log(1.0 + jnp.exp(-jnp.abs(x)))


def _ssd_kernel(need_y, length, *refs):
    t = SSD_CHUNK
    nch = length // t
    hw = 4 * SSM_HEADDIM
    if need_y:
        (xr, br, cr, zr, cwx, cbx, cwb, cbb, cwc, cbc, dtc, dtr, bias_c, a_c, bias_r, a_r, dsk, h0,
         y_ref, xs_s, b_s, c_s, dt_s, csr_s, csc_s, st_s, dec_s) = refs
    else:
        (xr, br, cwx, cbx, cwb, cbb, dtc, dtr, bias_c, a_c, bias_r, a_r, h0,
         hfin, xs_s, b_s, dt_s, csr_s, csc_s, st_s, dec_s) = refs

    ctile = min(256, length)
    _conv_silu(xr, cwx, cbx, xs_s, length, ctile)
    _conv_silu(br, cwb, cbb, b_s, length, ctile)
    if need_y:
        _conv_silu(cr, cwc, cbc, c_s, length, ctile)

    dt_s[...] = _softplus(dtc[0, 0] + bias_c[0])
    da_r = _softplus(dtr[0, 0] + bias_r[0]) * a_r[0]
    da2 = da_r.reshape(nch * 8, t)
    kk = lax.broadcasted_iota(I32, (t, 2 * t), 0)
    ll = lax.broadcasted_iota(I32, (t, 2 * t), 1)
    tri = jnp.where(ll < t, jnp.where(kk <= ll, 1.0, 0.0), jnp.where(kk >= ll - t, 1.0, 0.0)).astype(BF16)
    p0 = da2.astype(BF16)
    r1 = da2 - p0.astype(F32)
    p1 = r1.astype(BF16)
    p2 = (r1 - p1.astype(F32)).astype(BF16)
    cum = _dot(p0, tri) + _dot(p1, tri) + _dot(p2, tri)
    rowj = lax.broadcasted_iota(I32, (nch * 8, t), 0) & 7
    csr = jnp.where(rowj < 4, cum[:, :t], cum[:, t:])
    csr_s[...] = csr.reshape(nch, 8, t)

    def chunk_terms(c):
        dt = dt_s[c]
        csc = csc_s[c]
        return (_expand_heads(dt, 0, SSM_HEADDIM), _expand_heads(dt, 4, SSM_HEADDIM),
                _expand_heads(csc, 0, SSM_HEADDIM), _expand_heads(csc, 4, SSM_HEADDIM))

    def phase_a(c, carry):
        r0 = pl.multiple_of(c * t, t)
        csr_c = csr_s[c]
        csr_pad = jnp.concatenate([csr_c, jnp.zeros((t - 8, t), F32)], axis=0)
        csc_s[c] = csr_pad.T[:, 0:8]
        dte_f, dte_b, cse_f, cse_b = chunk_terms(c)
        xs = xs_s[pl.ds(r0, t), :]
        bc = b_s[pl.ds(r0, t), :]
        last_f = cse_f[t - 1:t, :]
        first_b = cse_b[0:1, :]
        xw_f = (xs * (dte_f * jnp.exp(last_f - cse_f))).astype(BF16)
        xw_b = (xs * (dte_b * jnp.exp(first_b - cse_b))).astype(BF16)
        st_s[c, 0] = _dot_tn(bc, xw_f)
        st_s[c, 1] = _dot_tn(bc, xw_b)
        dec_s[c, 0:1, :] = jnp.exp(last_f)
        dec_s[c, 1:2, :] = jnp.exp(first_b)
        return carry

    lax.fori_loop(0, nch, phase_a, 0)

    def rec_f(c, s):
        loc = st_s[c, 0]
        st_s[c, 0] = s
        return dec_s[c, 0:1, :] * s + loc

    def rec_b(k, s):
        c = nch - 1 - k
        loc = st_s[c, 1]
        st_s[c, 1] = s
        return dec_s[c, 1:2, :] * s + loc

    s_f = lax.fori_loop(0, nch, rec_f, h0[0, 0, 0])
    s_b = lax.fori_loop(0, nch, rec_b, h0[0, 0, 1])
    if not need_y:
        hfin[0, 0, 0] = s_f
        hfin[0, 0, 1] = s_b
        return

    li = lax.broadcasted_iota(I32, (t, t), 0)
    si = lax.broadcasted_iota(I32, (t, t), 1)
    dskip = dsk[...]

    def phase_c(c, carry):
        r0 = pl.multiple_of(c * t, t)
        dte_f, dte_b, cse_f, cse_b = chunk_terms(c)
        csr_c = csr_s[c]
        csc_c = csc_s[c]
        xs = xs_s[pl.ds(r0, t), :]
        bc = b_s[pl.ds(r0, t), :]
        cc = c_s[pl.ds(r0, t), :]
        cb = _dot_nt(cc, bc)
        xdt_f = xs * dte_f
        xdt_b = xs * dte_b
        ys = []
        for j in range(4):
            d_f = csc_c[:, j:j + 1] - csr_c[j:j + 1, :]
            att_f = jnp.where(si <= li, cb * jnp.exp(jnp.minimum(d_f, 0.0)), 0.0)
            d_b = csc_c[:, 4 + j:5 + j] - csr_c[4 + j:5 + j, :]
            att_b = jnp.where(si >= li, cb * jnp.exp(jnp.minimum(d_b, 0.0)), 0.0)
            lhs = jnp.concatenate([att_f, att_b], axis=1).astype(BF16)
            sl = slice(j * SSM_HEADDIM, (j + 1) * SSM_HEADDIM)
            rhs = jnp.concatenate([xdt_f[:, sl], xdt_b[:, sl]], axis=0).astype(BF16)
            ys.append(_dot(lhs, rhs))
        y = jnp.concatenate(ys, axis=1)
        y = y + _dot(cc, st_s[c, 0].astype(BF16)) * jnp.exp(cse_f)
        y = y + _dot(cc, st_s[c, 1].astype(BF16)) * jnp.exp(cse_b)
        y = y + xs * dskip
        z = zr[0, pl.ds(r0, t), :].astype(F32)
        y_ref[0, pl.ds(r0, t), :] = (y * _silu(z)).astype(y_ref.dtype)
        return carry

    lax.fori_loop(0, nch, phase_c, 0)


def _ssd(main3, dt_raw, conv_w, conv_b, dt_bias, a_log, d_skip, h0, need_y, d_ssm, xbc_col0):
    nb, length, _ = main3.shape
    g_n, t = SSM_GROUPS, SSD_CHUNK
    nch = length // t
    heads = d_ssm // SSM_HEADDIM
    hpg = heads // g_n
    assert hpg == 4 and SSM_STATE == 128
    hw = hpg * SSM_HEADDIM
    dt5 = dt_raw[..., :2 * heads].reshape(nb, nch, t, 2, g_n, hpg)
    dtc = jnp.transpose(dt5, (0, 4, 1, 2, 3, 5)).reshape(nb, g_n, nch, t, 8)
    dtr = jnp.transpose(dt5, (0, 4, 1, 3, 5, 2)).reshape(nb, g_n, nch, 8, t)
    par = lambda p: jnp.transpose(p.reshape(2, g_n, hpg), (1, 0, 2)).reshape(g_n, 8)
    bias = par(dt_bias.astype(F32))
    a = par(-jnp.exp(a_log.astype(F32)))
    bias_c, a_c = bias.reshape(g_n, 1, 8), a.reshape(g_n, 1, 8)
    bias_r, a_r = bias.reshape(g_n, 8, 1), a.reshape(g_n, 8, 1)
    nx = d_ssm
    nbc = g_n * SSM_STATE
    cwx, cwb, cwc = conv_w[:, :nx], conv_w[:, nx:nx + nbc], conv_w[:, nx + nbc:]
    cb2 = conv_b.reshape(1, -1)
    cbx, cbb, cbc = cb2[:, :nx], cb2[:, nx:nx + nbc], cb2[:, nx + nbc:]
    xcol = xbc_col0 // hw
    bcol = (xbc_col0 + nx) // SSM_STATE
    ccol = (xbc_col0 + nx + nbc) // SSM_STATE

    seq = lambda shape, imap: pl.BlockSpec(shape, imap)
    x_spec = seq((1, length, hw), lambda b, g: (b, 0, xcol + g))
    b_spec = seq((1, length, SSM_STATE), lambda b, g: (b, 0, bcol + g))
    c_spec = seq((1, length, SSM_STATE), lambda b, g: (b, 0, ccol + g))
    z_spec = seq((1, length, hw), lambda b, g: (b, 0, g))
    wx_spec = [seq((3, hw), lambda b, g: (0, g)), seq((1, hw), lambda b, g: (0, g))]
    wn_spec = [seq((3, SSM_STATE), lambda b, g: (0, g)), seq((1, SSM_STATE), lambda b, g: (0, g))]
    dt_specs = [seq((1, 1, nch, t, 8), lambda b, g: (b, g, 0, 0, 0)),
                seq((1, 1, nch, 8, t), lambda b, g: (b, g, 0, 0, 0)),
                seq((1, 1, 8), lambda b, g: (g, 0, 0)), seq((1, 1, 8), lambda b, g: (g, 0, 0)),
                seq((1, 8, 1), lambda b, g: (g, 0, 0)), seq((1, 8, 1), lambda b, g: (g, 0, 0))]
    h_spec = seq((1, 1, 2, SSM_STATE, hw), lambda b, g: (b, g, 0, 0, 0))
    common_scr = [pltpu.VMEM((nch, t, 8), F32),
                  pltpu.VMEM((nch, 8, t), F32),
                  pltpu.VMEM((nch, t, 8), F32),
                  pltpu.VMEM((nch, 2, SSM_STATE, hw), F32),
                  pltpu.VMEM((nch, 8, hw), F32)]
    if need_y:
        in_specs = ([x_spec, b_spec, c_spec, z_spec] + wx_spec + wn_spec + wn_spec + dt_specs
                    + [seq((1, hw), lambda b, g: (0, g)), h_spec])
        args = (main3, main3, main3, main3, cwx, cbx, cwb, cbb, cwc, cbc, dtc, dtr,
                bias_c, a_c, bias_r, a_r, d_skip, h0)
        out_specs = seq((1, length, hw), lambda b, g: (b, 0, g))
        out_shape = jax.ShapeDtypeStruct((nb, length, d_ssm), BF16)
        scratch = [pltpu.VMEM((length, hw), F32), pltpu.VMEM((length, SSM_STATE), BF16),
                   pltpu.VMEM((length, SSM_STATE), BF16)] + common_scr
    else:
        in_specs = [x_spec, b_spec] + wx_spec + wn_spec + dt_specs + [h_spec]
        args = (main3, main3, cwx, cbx, cwb, cbb, dtc, dtr, bias_c, a_c, bias_r, a_r, h0)
        out_specs = h_spec
        out_shape = jax.ShapeDtypeStruct((nb, g_n, 2, SSM_STATE, hw), F32)
        scratch = [pltpu.VMEM((length, hw), F32), pltpu.VMEM((length, SSM_STATE), BF16)] + common_scr
    return pl.pallas_call(
        functools.partial(_ssd_kernel, need_y, length),
        grid=(nb, g_n),
        in_specs=in_specs,
        out_specs=out_specs,
        out_shape=out_shape,
        scratch_shapes=scratch,
        compiler_params=_cparams(("parallel", "parallel")),
        name="ssd_y" if need_y else "ssd_state",
    )(*args)


def _pool_kernel(length, u_ref, pw_ref, ps_ref, o_ref, pad_s):
    gw = GRID_W
    rows = length // gw
    cg = pw_ref.shape[-1]
    halo = (max(POOL_WINDOWS) // 2) * gw
    tile = min(256, length)
    zeros = jnp.zeros((halo, cg), F32)
    pad_s[0:halo, :] = zeros
    pad_s[halo + length:halo + length + halo, :] = zeros
    for g, w in enumerate(POOL_WINDOWS):
        lo = -(w // 2)
        csl = slice(g * cg, (g + 1) * cg)
        for r0 in range(0, length, tile):
            pad_s[halo + r0:halo + r0 + tile, :] = u_ref[0, r0:r0 + tile, csl].astype(F32)
        pw = pw_ref[g]
        scale = ps_ref[:, csl]

        def body(i, carry, lo=lo, w=w, csl=csl, pw=pw, scale=scale):
            r0 = pl.multiple_of(i * tile, tile)
            acc = pad_s[pl.ds(halo + r0 + lo * gw, tile), :]
            for k in range(lo + 1, lo + w):
                acc = acc + pad_s[pl.ds(halo + r0 + k * gw, tile), :]
            l_idx = r0 + lax.broadcasted_iota(I32, (tile, cg), 0)
            ii = l_idx >> (gw.bit_length() - 1)
            jj = l_idx & (gw - 1)
            tot = acc
            for k in range(lo, lo + w):
                if k == 0:
                    continue
                sh = pltpu.roll(acc, (-k) % tile, 0)
                ok = (jj >= -k) if k < 0 else (jj < gw - k)
                tot = tot + jnp.where(ok, sh, 0.0)
            cnt_i = jnp.minimum(ii + lo + w, rows) - jnp.maximum(ii + lo, 0)
            cnt_j = jnp.minimum(jj + lo + w, gw) - jnp.maximum(jj + lo, 0)
            u = pad_s[pl.ds(halo + r0, tile), :]
            d = tot / (cnt_i * cnt_j).astype(F32) - u
            y = _dot(d.astype(BF16), pw) * scale
            o_ref[0, pl.ds(r0, tile), csl] = y.astype(o_ref.dtype)
            return carry

        lax.fori_loop(0, length // tile, body, 0)


def _pool(main3, pool_w, pool_scale, col0):
    nb, length, _ = main3.shape
    ng, cg, _ = pool_w.shape
    dp = ng * cg
    halo = (max(POOL_WINDOWS) // 2) * GRID_W
    return pl.pallas_call(
        functools.partial(_pool_kernel, length),
        grid=(nb,),
        in_specs=[pl.BlockSpec((1, length, dp), lambda b: (b, 0, col0 // dp)),
                  pl.BlockSpec((ng, cg, cg), lambda b: (0, 0, 0)),
                  pl.BlockSpec((1, dp), lambda b: (0, 0))],
        out_specs=pl.BlockSpec((1, length, dp), lambda b: (b, 0, 0)),
        out_shape=jax.ShapeDtypeStruct((nb, length, dp), BF16),
        scratch_shapes=[pltpu.VMEM((length + 2 * halo, cg), F32)],
        compiler_params=_cparams(("parallel",)),
        name="pool",
    )(main3, pool_w.astype(BF16), pool_scale.reshape(1, dp))


def _pack_bf16_pair(lo, hi):
    lo_b = lax.bitcast_convert_type(lo, U32) >> 16
    hi_b = lax.bitcast_convert_type(hi, U32) & jnp.uint32(0xFFFF0000)
    return lo_b | hi_b


def _unpack_bf16_pair(p):
    lo = lax.bitcast_convert_type(p << 16, F32)
    hi = lax.bitcast_convert_type(p & jnp.uint32(0xFFFF0000), F32)
    return lo, hi


def _outproj_kernel(yg_ref, yp_ref, x_ref, gs_ref, m2_ref, m3_ref, m4_ref, m5_ref, gf_ref,
                    ws_ref, wp_ref, wrh_ref, wrl_ref, wg_ref, wu_ref, wd_ref,
                    xs1_ref, h2p_ref, lg_ref):
    yg = yg_ref[...].astype(F32)
    ms = jnp.mean(yg * yg, axis=-1, keepdims=True)
    a = (yg * lax.rsqrt(ms + EPS) * gs_ref[...]).astype(BF16)
    o = _dot(a, ws_ref[...]) + _dot(yp_ref[...], wp_ref[...])
    x1 = x_ref[...] + m2_ref[0] * o
    ms1 = jnp.mean(x1 * x1, axis=-1, keepdims=True)
    h2 = x1 * lax.rsqrt(ms1 + EPS) * gf_ref[...]
    h2 = h2 * (1.0 + m4_ref[0]) + m3_ref[0]
    h_hi, h_lo = _split2(h2)
    wrh = wrh_ref[...]
    lg_ref[...] = _dot_nt(wrh, h_hi) + _dot_nt(wrl_ref[...], h_hi) + _dot_nt(wrh, h_lo)
    act = (_silu(_dot(h_hi, wg_ref[...])) * _dot(h_hi, wu_ref[...])).astype(BF16)
    xs1_ref[...] = x1 + m5_ref[0] * _dot(act, wd_ref[...])
    hf = h_hi.astype(F32)
    half = hf.shape[1] // 2
    h2p_ref[...] = _pack_bf16_pair(hf[:, :half], hf[:, half:])


def _outproj(yg, yp, x2d, g_ssd, mods, g_ffn, w_ssd, w_pool, wr_hi, wr_lo, wsg, wsu, wsd,
             rows_per_batch, tm):
    n, d = x2d.shape
    dp = yp.shape[1]
    ne = wr_hi.shape[0]
    dsh = wsg.shape[1]
    tpb = rows_per_batch // tm
    row = lambda c: pl.BlockSpec((tm, c), lambda i: (i, 0))
    vec = pl.BlockSpec((1, d), lambda i: (0, 0))
    mod = pl.BlockSpec((1, 1, d), lambda i: (i // tpb, 0, 0))
    res = lambda r, c: pl.BlockSpec((r, c), lambda i: (0, 0), pipeline_mode=pl.Buffered(1))
    m2, m3, m4, m5 = mods
    return pl.pallas_call(
        _outproj_kernel,
        grid=(n // tm,),
        in_specs=[row(d), row(dp), row(d), vec, mod, mod, mod, mod, vec,
                  res(d, d), res(dp, d), res(ne, d), res(ne, d), res(d, dsh), res(d, dsh), res(dsh, d)],
        out_specs=[row(d), row(d // 2), pl.BlockSpec((ne, tm), lambda i: (0, i))],
        out_shape=[jax.ShapeDtypeStruct((n, d), F32),
                   jax.ShapeDtypeStruct((n, d // 2), U32),
                   jax.ShapeDtypeStruct((ne, n), F32)],
        compiler_params=_cparams(("parallel",)),
        name="outproj",
    )(yg, yp, x2d, g_ssd.reshape(1, d), m2, m3, m4, m5, g_ffn.reshape(1, d),
      w_ssd, w_pool, wr_hi, wr_lo, wsg, wsu, wsd)


def _topk_kernel(lg_ref, rb_ref, te_ref, tw_ref, rk_ref, cnt_ref, carry):
    ne, tm = lg_ref.shape
    gsz = ne // N_EXPERT_GROUPS

    @pl.when(pl.program_id(0) == 0)
    def _():
        carry[...] = jnp.zeros_like(carry)

    s = _sigmoid(lg_ref[...])
    biased = s + rb_ref[...]
    neg = jnp.float32(-jnp.inf)
    big = jnp.int32(1 << 20)
    gi = lax.broadcasted_iota(I32, (gsz, tm), 0)
    gscore = []
    for g in range(N_EXPERT_GROUPS):
        v = biased[g * gsz:(g + 1) * gsz, :]
        m1 = jnp.max(v, axis=0, keepdims=True)
        i1 = jnp.min(jnp.where(v == m1, gi, big), axis=0, keepdims=True)
        m2 = jnp.max(jnp.where(gi == i1, neg, v), axis=0, keepdims=True)
        gscore.append(m1 + m2)
    parts = []
    for g in range(N_EXPERT_GROUPS):
        rank = jnp.zeros((1, tm), I32)
        for h in range(N_EXPERT_GROUPS):
            if h == g:
                continue
            ahead = (gscore[h] > gscore[g]) | ((gscore[h] == gscore[g]) & (h < g))
            rank = rank + ahead.astype(I32)
        keep = rank < TOPK_GROUPS
        parts.append(jnp.where(keep, biased[g * gsz:(g + 1) * gsz, :], neg))
    masked = jnp.concatenate(parts, axis=0)
    ei = lax.broadcasted_iota(I32, (ne, tm), 0)
    idxs, wts = [], []
    msel = jnp.zeros((ne, tm), F32)
    for _ in range(TOP_K):
        m = jnp.max(masked, axis=0, keepdims=True)
        idx = jnp.min(jnp.where(masked == m, ei, big), axis=0, keepdims=True)
        sel = ei == idx
        wts.append(jnp.sum(jnp.where(sel, s, 0.0), axis=0, keepdims=True))
        idxs.append(idx)
        masked = jnp.where(sel, neg, masked)
        msel = jnp.where(sel, 1.0, msel)
    wsum = wts[0]
    for w in wts[1:]:
        wsum = wsum + w
    ti = lax.broadcasted_iota(I32, (tm, tm), 0)
    tj = lax.broadcasted_iota(I32, (tm, tm), 1)
    before = jnp.where(ti < tj, 1.0, 0.0).astype(BF16)
    base = carry[...]
    rank_in = _dot(msel.astype(BF16), before) + jnp.concatenate([base] * (tm // 128), axis=1)
    for k in range(TOP_K):
        sel = ei == idxs[k]
        te_ref[k:k + 1, :] = idxs[k]
        tw_ref[k:k + 1, :] = wts[k] / wsum * ROUTE_SCALE
        rk_ref[k:k + 1, :] = jnp.sum(jnp.where(sel, rank_in, 0.0), axis=0, keepdims=True).astype(I32)
    total = base + _dot(msel.astype(BF16), jnp.ones((tm, 128), BF16))
    carry[...] = total
    cnt_ref[...] = total


def _topk(lg_t, router_bias, tm):
    ne, n = lg_t.shape
    row8 = lambda dt: jax.ShapeDtypeStruct((TOP_K, n), dt)
    return pl.pallas_call(
        _topk_kernel,
        grid=(n // tm,),
        in_specs=[pl.BlockSpec((ne, tm), lambda i: (0, i)),
                  pl.BlockSpec((ne, 1), lambda i: (0, 0))],
        out_specs=[pl.BlockSpec((TOP_K, tm), lambda i: (0, i))] * 3
        + [pl.BlockSpec((ne, 128), lambda i: (0, 0))],
        out_shape=[row8(I32), row8(F32), row8(I32), jax.ShapeDtypeStruct((ne, 128), F32)],
        scratch_shapes=[pltpu.VMEM((ne, 128), F32)],
        compiler_params=_cparams(("arbitrary",)),
        name="topk",
    )(lg_t, router_bias.reshape(ne, 1).astype(F32))


def _moe_kernel(be_ref, ng_ref, tok_ref, tokn_ref, h_hbm, wg_ref, wu_ref, wd_ref, o_ref,
                rows, sems):
    del be_ref
    i = pl.program_id(0)
    nsteps = pl.num_programs(0)
    slot = i % 2

    def row_copy(tref, r, dst_slot):
        tok = tref[0, 0, r]
        return pltpu.make_async_copy(h_hbm.at[pl.ds(tok, 1), :], rows.at[dst_slot, pl.ds(r, 1), :],
                                     sems.at[dst_slot])

    def issue(tref, ngroups, dst_slot):
        def body(r, carry):
            row_copy(tref, r, dst_slot).start()
            return carry
        lax.fori_loop(0, ngroups * GATHER_GROUP, body, 0)

    @pl.when(i == 0)
    def _():
        rows[...] = jnp.zeros_like(rows)
        issue(tok_ref, ng_ref[0], 0)

    @pl.when(i + 1 < nsteps)
    def _():
        issue(tokn_ref, ng_ref[i + 1], 1 - slot)

    ngr = ng_ref[i]

    @pl.when(ngr > 0)
    def _():
        def wait_body(g, carry):
            pltpu.make_async_copy(h_hbm.at[pl.ds(0, GATHER_GROUP), :],
                                  rows.at[slot, pl.ds(0, GATHER_GROUP), :], sems.at[slot]).wait()
            return carry
        lax.fori_loop(0, ngr, wait_body, 0)
        x_lo, x_hi = _unpack_bf16_pair(rows[slot])
        x_lo = x_lo.astype(BF16)
        x_hi = x_hi.astype(BF16)
        half = x_lo.shape[1]
        wg = wg_ref[0].astype(BF16)
        wu = wu_ref[0].astype(BF16)
        g = _dot(x_lo, wg[:half, :]) + _dot(x_hi, wg[half:, :])
        u = _dot(x_lo, wu[:half, :]) + _dot(x_hi, wu[half:, :])
        act = (_silu(g) * u).astype(BF16)
        y = _dot(act, wd_ref[0].astype(BF16))
        yb = y.astype(BF16).astype(F32)
        o_ref[...] = _pack_bf16_pair(yb[:, :half], yb[:, half:])

    @pl.when(ngr == 0)
    def _():
        o_ref[...] = jnp.zeros_like(o_ref)


def _moe(block_e, ngroups, tok_blk, h2p, w_gate, w_up, w_down):
    nblk = block_e.shape[0]
    mb = MOE_BLOCK
    ne, d, f = w_gate.shape
    half = h2p.shape[1]
    tok3 = tok_blk.reshape(nblk, 1, mb)
    grid_spec = pltpu.PrefetchScalarGridSpec(
        num_scalar_prefetch=2,
        grid=(nblk,),
        in_specs=[pl.BlockSpec((1, 1, mb), lambda i, be, ng: (i, 0, 0), memory_space=pltpu.SMEM),
                  pl.BlockSpec((1, 1, mb), lambda i, be, ng: (jnp.minimum(i + 1, nblk - 1), 0, 0),
                               memory_space=pltpu.SMEM),
                  pl.BlockSpec(memory_space=pl.ANY),
                  pl.BlockSpec((1, d, f), lambda i, be, ng: (be[i], 0, 0)),
                  pl.BlockSpec((1, d, f), lambda i, be, ng: (be[i], 0, 0)),
                  pl.BlockSpec((1, f, d), lambda i, be, ng: (be[i], 0, 0))],
        out_specs=pl.BlockSpec((mb, half), lambda i, be, ng: (i, 0)),
        scratch_shapes=[pltpu.VMEM((2, mb, half), U32), pltpu.SemaphoreType.DMA((2,))],
    )
    return pl.pallas_call(
        _moe_kernel,
        grid_spec=grid_spec,
        out_shape=jax.ShapeDtypeStruct((nblk * mb, half), U32),
        compiler_params=_cparams(("arbitrary",)),
        name="moe",
    )(block_e, ngroups, tok3, tok3, h2p, w_gate, w_up, w_down)


def _combine_kernel(pos_ref, posn_ref, y_hbm, w_ref, xs1_ref, m5_ref, gf_ref, o_ref, rows, sems):
    i = pl.program_id(0)
    nsteps = pl.num_programs(0)
    slot = i % 2
    tm = o_ref.shape[0]
    nrow = TOP_K * tm

    def issue(pref, dst_slot):
        def body(r, carry):
            p = pref[0, 0, r]
            pltpu.make_async_copy(y_hbm.at[pl.ds(p, 1), :], rows.at[dst_slot, pl.ds(r, 1), :],
                                  sems.at[dst_slot]).start()
            return carry
        lax.fori_loop(0, nrow, body, 0)

    @pl.when(i == 0)
    def _():
        issue(pos_ref, 0)

    @pl.when(i + 1 < nsteps)
    def _():
        issue(posn_ref, 1 - slot)

    pltpu.make_async_copy(y_hbm.at[pl.ds(0, nrow), :], rows.at[slot], sems.at[slot]).wait()
    w = w_ref[...]
    acc_lo = acc_hi = None
    for k in range(TOP_K):
        lo, hi = _unpack_bf16_pair(rows[slot, k * tm:(k + 1) * tm, :])
        wk = w[:, k:k + 1]
        acc_lo = wk * lo if acc_lo is None else acc_lo + wk * lo
        acc_hi = wk * hi if acc_hi is None else acc_hi + wk * hi
    routed = jnp.concatenate([acc_lo, acc_hi], axis=1)
    x = xs1_ref[...] + m5_ref[0] * routed
    ms = jnp.mean(x * x, axis=-1, keepdims=True)
    o_ref[...] = x * lax.rsqrt(ms + EPS) * gf_ref[...]


def _combine(pos_tiles, y_packed, w_tok, xs1, m5, g_final, rows_per_batch, tm):
    n, d = xs1.shape
    half = y_packed.shape[1]
    nt = n // tm
    tpb = rows_per_batch // tm
    pos3 = pos_tiles.reshape(nt, 1, TOP_K * tm)
    return pl.pallas_call(
        _combine_kernel,
        grid=(nt,),
        in_specs=[pl.BlockSpec((1, 1, TOP_K * tm), lambda i: (i, 0, 0), memory_space=pltpu.SMEM),
                  pl.BlockSpec((1, 1, TOP_K * tm), lambda i: (jnp.minimum(i + 1, nt - 1), 0, 0),
                               memory_space=pltpu.SMEM),
                  pl.BlockSpec(memory_space=pl.ANY),
                  pl.BlockSpec((tm, TOP_K), lambda i: (i, 0)),
                  pl.BlockSpec((tm, d), lambda i: (i, 0)),
                  pl.BlockSpec((1, 1, d), lambda i: (i // tpb, 0, 0)),
                  pl.BlockSpec((1, d), lambda i: (0, 0))],
        out_specs=pl.BlockSpec((tm, d), lambda i: (i, 0)),
        out_shape=jax.ShapeDtypeStruct((n, d), F32),
        scratch_shapes=[pltpu.VMEM((2, TOP_K * tm, half), U32), pltpu.SemaphoreType.DMA((2,))],
        compiler_params=_cparams(("arbitrary",)),
        name="combine",
    )(pos3, pos3, y_packed, w_tok, xs1, m5, g_final.reshape(1, d))


def _dispatch_plan(top_e, rank, counts, n_tok):
    mb = MOE_BLOCK
    ne = counts.shape[0]
    n_asg = TOP_K * n_tok
    nblk = -(-(n_asg + ne * (mb - 1)) // mb)
    tok_ids = jnp.broadcast_to(jnp.arange(n_tok, dtype=I32)[None, :], top_e.shape)
    keys = jnp.sort((top_e * n_tok + tok_ids).reshape(-1))
    sorted_tok = jnp.concatenate([keys % n_tok, jnp.zeros((mb,), I32)])
    start = jnp.cumsum(counts) - counts
    eblk = (counts + mb - 1) // mb
    pend = jnp.cumsum(eblk)
    pstart = pend - eblk
    n_real = pend[-1]
    bid = jnp.arange(nblk, dtype=I32)
    last_real = jnp.maximum(n_real - 1, 0)
    bsrc = jnp.minimum(bid, last_real)
    block_e = jnp.minimum(jnp.searchsorted(pend, bsrc, side="right"), ne - 1).astype(I32)
    off = (bsrc - pstart[block_e]) * mb
    nvalid = jnp.where(bid < n_real, jnp.clip(counts[block_e] - off, 0, mb), 0)
    ngroups = ((nvalid + GATHER_GROUP - 1) // GATHER_GROUP).astype(I32)
    a0 = start[block_e] + off
    tok_blk = jax.vmap(lambda s: lax.dynamic_slice(sorted_tok, (s,), (mb,)))(a0)
    pos = pstart[top_e] * mb + rank
    return block_e, ngroups, tok_blk, pos


def kernel(x, c, ctx, c_ctx, w_ada, b_ada, g_mix, w_in, conv_w, conv_b, dt_bias, a_log, d_skip, g_ssd,
           pool_w, pool_scale, w_out, g_ffn, w_router, router_bias, w_exp_gate, w_exp_up, w_exp_down,
           w_sh_gate, w_sh_up, w_sh_down, g_final):
    bsz, seq, d = x.shape
    ctx_len = ctx.shape[1]
    assert w_ada.shape[0] == 1, "single-layer block"
    d_ssm = g_ssd.shape[1]
    heads = d_skip.shape[1]
    d_pool = pool_scale.shape[1]
    d_xbc = conv_w.shape[2]
    n = bsz * seq

    cvec = jnp.zeros((8, d), F32).at[:bsz].set(c).at[bsz].set(c_ctx)
    mod_all = _ada(cvec, w_ada[0], b_ada[0]).reshape(8, N_MOD, d)
    mod = mod_all[:bsz]
    mod_c = mod_all[bsz:bsz + 1]
    mk = lambda m, k: m[:, k:k + 1, :]

    w = w_in[0]
    c_dt = d_ssm + d_xbc
    w_main = jnp.concatenate([w[:, :c_dt], w[:, c_dt + 2 * heads:]], axis=1).astype(BF16)
    w_dt = jnp.pad(w[:, c_dt:c_dt + 2 * heads], ((0, 0), (0, 128 - 2 * heads)))

    main_c, dt_c = _inproj(ctx.reshape(bsz * ctx_len, d), g_mix[0], mk(mod_c, 0), mk(mod_c, 1),
                           w_main, w_dt, bsz * ctx_len, tm=256, tn=1024)
    h_zero = jnp.zeros((bsz, SSM_GROUPS, 2, SSM_STATE, 4 * SSM_HEADDIM), F32)
    h_ctx = _ssd(main_c.reshape(bsz, ctx_len, -1), dt_c.reshape(bsz, ctx_len, 128), conv_w[0], conv_b[0],
                 dt_bias[0], a_log[0], None, h_zero, False, d_ssm, d_ssm)

    x2d = x.reshape(n, d)
    main, dt_raw = _inproj(x2d, g_mix[0], mk(mod, 0), mk(mod, 1), w_main, w_dt, seq, tm=min(1024, seq), tn=1024)
    main3 = main.reshape(bsz, seq, -1)
    dsk = jnp.repeat(d_skip[0].astype(F32), SSM_HEADDIM).reshape(1, d_ssm)
    yg = _ssd(main3, dt_raw.reshape(bsz, seq, 128), conv_w[0], conv_b[0], dt_bias[0], a_log[0], dsk,
              h_ctx, True, d_ssm, d_ssm)
    yp = _pool(main3, pool_w[0], pool_scale[0], d_ssm + d_xbc)

    wo = w_out[0].astype(BF16)
    wr = w_router[0].T
    wr_hi = wr.astype(BF16)
    wr_lo = (wr - wr_hi.astype(F32)).astype(BF16)
    xs1, h2p, lg_t = _outproj(
        yg.reshape(n, d_ssm), yp.reshape(n, d_pool), x2d, g_ssd[0],
        (mk(mod, 2), mk(mod, 3), mk(mod, 4), mk(mod, 5)), g_ffn[0],
        wo[:d_ssm], wo[d_ssm:], wr_hi, wr_lo,
        w_sh_gate[0].astype(BF16), w_sh_up[0].astype(BF16), w_sh_down[0].astype(BF16), seq, tm=256)

    top_e, top_w, rank, cnt = _topk(lg_t, router_bias[0], tm=512)
    counts = cnt[:, 0].astype(I32)
    block_e, ngroups, tok_blk, pos = _dispatch_plan(top_e, rank, counts, n)
    y_packed = _moe(block_e, ngroups, tok_blk, h2p, w_exp_gate[0], w_exp_up[0], w_exp_down[0])

    tmc = 128
    pos_tiles = jnp.transpose(pos.reshape(TOP_K, n // tmc, tmc), (1, 0, 2))
    out = _combine(pos_tiles, y_packed, top_w.T, xs1, mk(mod, 5), g_final, seq, tmc)
    return out.reshape(bsz, seq, d)
```

```python
import functools

import jax
import jax.numpy as jnp
from jax import lax
from jax.experimental import pallas as pl
from jax.experimental.pallas import tpu as pltpu

F32 = jnp.float32
BF16 = jnp.bfloat16
I32 = jnp.int32
U32 = jnp.uint32

EPS = 1e-6
GRID_W = 64
SSM_HEADDIM = 64
SSM_GROUPS = 8
SSM_STATE = 128
SSD_CHUNK = 128
POOL_WINDOWS = (2, 4, 8, 16)
TOP_K = 8
N_EXPERT_GROUPS = 8
TOPK_GROUPS = 4
ROUTE_SCALE = 2.5
N_MOD = 6
LANES = 128
MOE_BLOCK = 512
GATHER_GROUP = 64
TOK_WINDOW_ROWS = MOE_BLOCK // LANES + 1
V7X_VMEM_LIMIT = 56 * 1024 * 1024


def _cparams(sem, vmem=V7X_VMEM_LIMIT):
    return pltpu.CompilerParams(dimension_semantics=sem, vmem_limit_bytes=vmem)


def _sigmoid(x):
    return 1.0 / (1.0 + jnp.exp(-x))


def _silu(x):
    return x * _sigmoid(x)


def _split2(x):
    hi = x.astype(BF16)
    lo = (x - hi.astype(F32)).astype(BF16)
    return hi, lo


def _dot(a, b):
    return jnp.dot(a, b, preferred_element_type=F32)


def _dot_nt(a, b):
    return lax.dot_general(a, b, (((1,), (1,)), ((), ())), preferred_element_type=F32)


def _dot_tn(a, b):
    return lax.dot_general(a, b, (((0,), (0,)), ((), ())), preferred_element_type=F32)


def _dot3(a, b):
    a_hi, a_lo = _split2(a)
    b_hi, b_lo = _split2(b)
    return _dot(a_hi, b_hi) + _dot(a_lo, b_hi) + _dot(a_hi, b_lo)


def _ada_kernel(c_ref, w_ref, b_ref, o_ref):
    c = c_ref[...]
    o_ref[...] = _dot3(_silu(c), w_ref[...]) + b_ref[...]


def _ada(cvec, w_ada, b_ada, tn=1024):
    d, n = w_ada.shape
    return pl.pallas_call(
        _ada_kernel,
        grid=(n // tn,),
        in_specs=[pl.BlockSpec((8, d), lambda j: (0, 0)),
                  pl.BlockSpec((d, tn), lambda j: (0, j)),
                  pl.BlockSpec((1, tn), lambda j: (0, j))],
        out_specs=pl.BlockSpec((8, tn), lambda j: (0, j)),
        out_shape=jax.ShapeDtypeStruct((8, n), F32),
        compiler_params=_cparams(("parallel",)),
        name="ada",
    )(cvec, w_ada, b_ada.reshape(1, n))


def _inproj_kernel(x_ref, g_ref, sh_ref, sc_ref, w_ref, wdt_ref, o_ref, dt_ref, h_scr):
    @pl.when(pl.program_id(1) == 0)
    def _():
        x = x_ref[...]
        ms = jnp.mean(x * x, axis=-1, keepdims=True)
        y = x * lax.rsqrt(ms + EPS) * g_ref[...]
        h = y * (1.0 + sc_ref[0]) + sh_ref[0]
        h_hi, h_lo = _split2(h)
        h_scr[...] = h_hi
        w_hi, w_lo = _split2(wdt_ref[...])
        dt_ref[...] = _dot(h_hi, w_hi) + _dot(h_lo, w_hi) + _dot(h_hi, w_lo)

    o_ref[...] = _dot(h_scr[...], w_ref[...]).astype(o_ref.dtype)


def _inproj(x2d, g, shift, scale, w_main, w_dt, rows_per_batch, tm, tn):
    n, d = x2d.shape
    nc = w_main.shape[1]
    tpb = rows_per_batch // tm
    return pl.pallas_call(
        _inproj_kernel,
        grid=(n // tm, nc // tn),
        in_specs=[pl.BlockSpec((tm, d), lambda i, j: (i, 0)),
                  pl.BlockSpec((1, d), lambda i, j: (0, 0)),
                  pl.BlockSpec((1, 1, d), lambda i, j: (i // tpb, 0, 0)),
                  pl.BlockSpec((1, 1, d), lambda i, j: (i // tpb, 0, 0)),
                  pl.BlockSpec((d, tn), lambda i, j: (0, j)),
                  pl.BlockSpec((d, 128), lambda i, j: (0, 0))],
        out_specs=[pl.BlockSpec((tm, tn), lambda i, j: (i, j)),
                   pl.BlockSpec((tm, 128), lambda i, j: (i, 0))],
        out_shape=[jax.ShapeDtypeStruct((n, nc), BF16),
                   jax.ShapeDtypeStruct((n, 128), F32)],
        scratch_shapes=[pltpu.VMEM((tm, d), BF16)],
        compiler_params=_cparams(("parallel", "arbitrary")),
        name="inproj",
    )(x2d, g.reshape(1, d), shift, scale, w_main, w_dt)


def _expand_heads(v, base, width):
    t = v.shape[0]
    lane = lax.broadcasted_iota(I32, (t, 4 * width), 1)
    out = jnp.broadcast_to(v[:, base + 3:base + 4], (t, 4 * width))
    for j in (2, 1, 0):
        out = jnp.where(lane < (j + 1) * width, v[:, base + j:base + j + 1], out)
    return out


def _conv_silu(src_ref, w_ref, b_ref, dst_ref, length, tile):
    c = src_ref.shape[-1]
    w = w_ref[...]
    b = b_ref[...]
    rid = lax.broadcasted_iota(I32, (tile, c), 0)
    for r0 in range(0, length, tile):
        cur = src_ref[0, r0:r0 + tile, :].astype(F32)
        if r0 == 0:
            prev_row = jnp.zeros((1, c), F32)
        else:
            prev_row = src_ref[0, r0 - 16:r0, :].astype(F32)[15:16, :]
        if r0 + tile == length:
            next_row = jnp.zeros((1, c), F32)
        else:
            next_row = src_ref[0, r0 + tile:r0 + tile + 16, :].astype(F32)[0:1, :]
        up = jnp.where(rid == 0, prev_row, pltpu.roll(cur, 1, 0))
        dn = jnp.where(rid == tile - 1, next_row, pltpu.roll(cur, tile - 1, 0))
        o = up * w[0:1, :] + cur * w[1:2, :] + dn * w[2:3, :] + b
        dst_ref[r0:r0 + tile, :] = _silu(o).astype(dst_ref.dtype)


def _softplus(x):
    return jnp.maximum(x, 0.0) + jnp.log(1.0 + jnp.exp(-jnp.abs(x)))


def _ssd_kernel(need_y, length, *refs):
    t = SSD_CHUNK
    nch = length // t
    hw = 4 * SSM_HEADDIM
    if need_y:
        (xr, br, cr, zr, cwx, cbx, cwb, cbb, cwc, cbc, dtc, dtr, bias_c, a_c, bias_r, a_r, dsk, h0,
         y_ref, xs_s, b_s, c_s, dt_s, csr_s, csc_s, st_s, dec_s) = refs
    else:
        (xr, br, cwx, cbx, cwb, cbb, dtc, dtr, bias_c, a_c, bias_r, a_r, h0,
         hfin, xs_s, b_s, dt_s, csr_s, csc_s, st_s, dec_s) = refs

    ctile = min(256, length)
    _conv_silu(xr, cwx, cbx, xs_s, length, ctile)
    _conv_silu(br, cwb, cbb, b_s, length, ctile)
    if need_y:
        _conv_silu(cr, cwc, cbc, c_s, length, ctile)

    dt_s[...] = _softplus(dtc[0, 0] + bias_c[0])
    da_r = _softplus(dtr[0, 0] + bias_r[0]) * a_r[0]
    da2 = da_r.reshape(nch * 8, t)
    kk = lax.broadcasted_iota(I32, (t, 2 * t), 0)
    ll = lax.broadcasted_iota(I32, (t, 2 * t), 1)
    tri = jnp.where(ll < t, jnp.where(kk <= ll, 1.0, 0.0), jnp.where(kk >= ll - t, 1.0, 0.0)).astype(BF16)
    p0 = da2.astype(BF16)
    r1 = da2 - p0.astype(F32)
    p1 = r1.astype(BF16)
    p2 = (r1 - p1.astype(F32)).astype(BF16)
    cum = _dot(p0, tri) + _dot(p1, tri) + _dot(p2, tri)
    rowj = lax.broadcasted_iota(I32, (nch * 8, t), 0) & 7
    csr = jnp.where(rowj < 4, cum[:, :t], cum[:, t:])
    csr_s[...] = csr.reshape(nch, 8, t)

    def chunk_terms(c):
        dt = dt_s[c]
        csc = csc_s[c]
        return (_expand_heads(dt, 0, SSM_HEADDIM), _expand_heads(dt, 4, SSM_HEADDIM),
                _expand_heads(csc, 0, SSM_HEADDIM), _expand_heads(csc, 4, SSM_HEADDIM))

    def phase_a(c, carry):
        r0 = pl.multiple_of(c * t, t)
        csr_c = csr_s[c]
        csr_pad = jnp.concatenate([csr_c, jnp.zeros((t - 8, t), F32)], axis=0)
        csc_s[c] = csr_pad.T[:, 0:8]
        dte_f, dte_b, cse_f, cse_b = chunk_terms(c)
        xs = xs_s[pl.ds(r0, t), :]
        bc = b_s[pl.ds(r0, t), :]
        last_f = cse_f[t - 1:t, :]
        first_b = cse_b[0:1, :]
        xw_f = (xs * (dte_f * jnp.exp(last_f - cse_f))).astype(BF16)
        xw_b = (xs * (dte_b * jnp.exp(first_b - cse_b))).astype(BF16)
        st_s[c, 0] = _dot_tn(bc, xw_f)
        st_s[c, 1] = _dot_tn(bc, xw_b)
        dec_s[c, 0:1, :] = jnp.exp(last_f)
        dec_s[c, 1:2, :] = jnp.exp(first_b)
        return carry

    lax.fori_loop(0, nch, phase_a, 0)

    def rec_f(c, s):
        loc = st_s[c, 0]
        st_s[c, 0] = s
        return dec_s[c, 0:1, :] * s + loc

    def rec_b(k, s):
        c = nch - 1 - k
        loc = st_s[c, 1]
        st_s[c, 1] = s
        return dec_s[c, 1:2, :] * s + loc

    s_f = lax.fori_loop(0, nch, rec_f, h0[0, 0, 0])
    s_b = lax.fori_loop(0, nch, rec_b, h0[0, 0, 1])
    if not need_y:
        hfin[0, 0, 0] = s_f
        hfin[0, 0, 1] = s_b
        return

    li = lax.broadcasted_iota(I32, (t, t), 0)
    si = lax.broadcasted_iota(I32, (t, t), 1)
    dskip = dsk[...]

    def phase_c(c, carry):
        r0 = pl.multiple_of(c * t, t)
        dte_f, dte_b, cse_f, cse_b = chunk_terms(c)
        csr_c = csr_s[c]
        csc_c = csc_s[c]
        xs = xs_s[pl.ds(r0, t), :]
        bc = b_s[pl.ds(r0, t), :]
        cc = c_s[pl.ds(r0, t), :]
        cb = _dot_nt(cc, bc)
        xdt_f = xs * dte_f
        xdt_b = xs * dte_b
        ys = []
        for j in range(4):
            d_f = csc_c[:, j:j + 1] - csr_c[j:j + 1, :]
            att_f = jnp.where(si <= li, cb * jnp.exp(jnp.minimum(d_f, 0.0)), 0.0)
            d_b = csc_c[:, 4 + j:5 + j] - csr_c[4 + j:5 + j, :]
            att_b = jnp.where(si >= li, cb * jnp.exp(jnp.minimum(d_b, 0.0)), 0.0)
            lhs = jnp.concatenate([att_f, att_b], axis=1).astype(BF16)
            sl = slice(j * SSM_HEADDIM, (j + 1) * SSM_HEADDIM)
            rhs = jnp.concatenate([xdt_f[:, sl], xdt_b[:, sl]], axis=0).astype(BF16)
            ys.append(_dot(lhs, rhs))
        y = jnp.concatenate(ys, axis=1)
        y = y + _dot(cc, st_s[c, 0].astype(BF16)) * jnp.exp(cse_f)
        y = y + _dot(cc, st_s[c, 1].astype(BF16)) * jnp.exp(cse_b)
        y = y + xs * dskip
        z = zr[0, pl.ds(r0, t), :].astype(F32)
        y_ref[0, pl.ds(r0, t), :] = (y * _silu(z)).astype(y_ref.dtype)
        return carry

    lax.fori_loop(0, nch, phase_c, 0)


def _ssd(main3, dt_raw, conv_w, conv_b, dt_bias, a_log, d_skip, h0, need_y, d_ssm, xbc_col0):
    nb, length, _ = main3.shape
    g_n, t = SSM_GROUPS, SSD_CHUNK
    nch = length // t
    heads = d_ssm // SSM_HEADDIM
    hpg = heads // g_n
    assert hpg == 4 and SSM_STATE == 128
    hw = hpg * SSM_HEADDIM
    dt5 = dt_raw[..., :2 * heads].reshape(nb, nch, t, 2, g_n, hpg)
    dtc = jnp.transpose(dt5, (0, 4, 1, 2, 3, 5)).reshape(nb, g_n, nch, t, 8)
    dtr = jnp.transpose(dt5, (0, 4, 1, 3, 5, 2)).reshape(nb, g_n, nch, 8, t)
    par = lambda p: jnp.transpose(p.reshape(2, g_n, hpg), (1, 0, 2)).reshape(g_n, 8)
    bias = par(dt_bias.astype(F32))
    a = par(-jnp.exp(a_log.astype(F32)))
    bias_c, a_c = bias.reshape(g_n, 1, 8), a.reshape(g_n, 1, 8)
    bias_r, a_r = bias.reshape(g_n, 8, 1), a.reshape(g_n, 8, 1)
    nx = d_ssm
    nbc = g_n * SSM_STATE
    cwx, cwb, cwc = conv_w[:, :nx], conv_w[:, nx:nx + nbc], conv_w[:, nx + nbc:]
    cb2 = conv_b.reshape(1, -1)
    cbx, cbb, cbc = cb2[:, :nx], cb2[:, nx:nx + nbc], cb2[:, nx + nbc:]
    xcol = xbc_col0 // hw
    bcol = (xbc_col0 + nx) // SSM_STATE
    ccol = (xbc_col0 + nx + nbc) // SSM_STATE

    seq = lambda shape, imap: pl.BlockSpec(shape, imap)
    x_spec = seq((1, length, hw), lambda b, g: (b, 0, xcol + g))
    b_spec = seq((1, length, SSM_STATE), lambda b, g: (b, 0, bcol + g))
    c_spec = seq((1, length, SSM_STATE), lambda b, g: (b, 0, ccol + g))
    z_spec = seq((1, length, hw), lambda b, g: (b, 0, g))
    wx_spec = [seq((3, hw), lambda b, g: (0, g)), seq((1, hw), lambda b, g: (0, g))]
    wn_spec = [seq((3, SSM_STATE), lambda b, g: (0, g)), seq((1, SSM_STATE), lambda b, g: (0, g))]
    dt_specs = [seq((1, 1, nch, t, 8), lambda b, g: (b, g, 0, 0, 0)),
                seq((1, 1, nch, 8, t), lambda b, g: (b, g, 0, 0, 0)),
                seq((1, 1, 8), lambda b, g: (g, 0, 0)), seq((1, 1, 8), lambda b, g: (g, 0, 0)),
                seq((1, 8, 1), lambda b, g: (g, 0, 0)), seq((1, 8, 1), lambda b, g: (g, 0, 0))]
    h_spec = seq((1, 1, 2, SSM_STATE, hw), lambda b, g: (b, g, 0, 0, 0))
    common_scr = [pltpu.VMEM((nch, t, 8), F32),
                  pltpu.VMEM((nch, 8, t), F32),
                  pltpu.VMEM((nch, t, 8), F32),
                  pltpu.VMEM((nch, 2, SSM_STATE, hw), F32),
                  pltpu.VMEM((nch, 8, hw), F32)]
    if need_y:
        in_specs = ([x_spec, b_spec, c_spec, z_spec] + wx_spec + wn_spec + wn_spec + dt_specs
                    + [seq((1, hw), lambda b, g: (0, g)), h_spec])
        args = (main3, main3, main3, main3, cwx, cbx, cwb, cbb, cwc, cbc, dtc, dtr,
                bias_c, a_c, bias_r, a_r, d_skip, h0)
        out_specs = seq((1, length, hw), lambda b, g: (b, 0, g))
        out_shape = jax.ShapeDtypeStruct((nb, length, d_ssm), BF16)
        scratch = [pltpu.VMEM((length, hw), F32), pltpu.VMEM((length, SSM_STATE), BF16),
                   pltpu.VMEM((length, SSM_STATE), BF16)] + common_scr
    else:
        in_specs = [x_spec, b_spec] + wx_spec + wn_spec + dt_specs + [h_spec]
        args = (main3, main3, cwx, cbx, cwb, cbb, dtc, dtr, bias_c, a_c, bias_r, a_r, h0)
        out_specs = h_spec
        out_shape = jax.ShapeDtypeStruct((nb, g_n, 2, SSM_STATE, hw), F32)
        scratch = [pltpu.VMEM((length, hw), F32), pltpu.VMEM((length, SSM_STATE), BF16)] + common_scr
    return pl.pallas_call(
        functools.partial(_ssd_kernel, need_y, length),
        grid=(nb, g_n),
        in_specs=in_specs,
        out_specs=out_specs,
        out_shape=out_shape,
        scratch_shapes=scratch,
        compiler_params=_cparams(("parallel", "parallel")),
        name="ssd_y" if need_y else "ssd_state",
    )(*args)


def _pool_kernel(length, u_ref, pw_ref, ps_ref, o_ref, pad_s):
    gw = GRID_W
    rows = length // gw
    cg = pw_ref.shape[-1]
    halo = (max(POOL_WINDOWS) // 2) * gw
    tile = min(256, length)
    zeros = jnp.zeros((halo, cg), F32)
    pad_s[0:halo, :] = zeros
    pad_s[halo + length:halo + length + halo, :] = zeros
    for g, w in enumerate(POOL_WINDOWS):
        lo = -(w // 2)
        csl = slice(g * cg, (g + 1) * cg)
        for r0 in range(0, length, tile):
            pad_s[halo + r0:halo + r0 + tile, :] = u_ref[0, r0:r0 + tile, csl].astype(F32)
        pw = pw_ref[g]
        scale = ps_ref[:, csl]

        def body(i, carry, lo=lo, w=w, csl=csl, pw=pw, scale=scale):
            r0 = pl.multiple_of(i * tile, tile)
            acc = pad_s[pl.ds(halo + r0 + lo * gw, tile), :]
            for k in range(lo + 1, lo + w):
                acc = acc + pad_s[pl.ds(halo + r0 + k * gw, tile), :]
            l_idx = r0 + lax.broadcasted_iota(I32, (tile, cg), 0)
            ii = l_idx >> (gw.bit_length() - 1)
            jj = l_idx & (gw - 1)
            tot = acc
            for k in range(lo, lo + w):
                if k == 0:
                    continue
                sh = pltpu.roll(acc, (-k) % tile, 0)
                ok = (jj >= -k) if k < 0 else (jj < gw - k)
                tot = tot + jnp.where(ok, sh, 0.0)
            cnt_i = jnp.minimum(ii + lo + w, rows) - jnp.maximum(ii + lo, 0)
            cnt_j = jnp.minimum(jj + lo + w, gw) - jnp.maximum(jj + lo, 0)
            u = pad_s[pl.ds(halo + r0, tile), :]
            d = tot / (cnt_i * cnt_j).astype(F32) - u
            y = _dot(d.astype(BF16), pw) * scale
            o_ref[0, pl.ds(r0, tile), csl] = y.astype(o_ref.dtype)
            return carry

        lax.fori_loop(0, length // tile, body, 0)


def _pool(main3, pool_w, pool_scale, col0):
    nb, length, _ = main3.shape
    ng, cg, _ = pool_w.shape
    dp = ng * cg
    halo = (max(POOL_WINDOWS) // 2) * GRID_W
    return pl.pallas_call(
        functools.partial(_pool_kernel, length),
        grid=(nb,),
        in_specs=[pl.BlockSpec((1, length, dp), lambda b: (b, 0, col0 // dp)),
                  pl.BlockSpec((ng, cg, cg), lambda b: (0, 0, 0)),
                  pl.BlockSpec((1, dp), lambda b: (0, 0))],
        out_specs=pl.BlockSpec((1, length, dp), lambda b: (b, 0, 0)),
        out_shape=jax.ShapeDtypeStruct((nb, length, dp), BF16),
        scratch_shapes=[pltpu.VMEM((length + 2 * halo, cg), F32)],
        compiler_params=_cparams(("parallel",)),
        name="pool",
    )(main3, pool_w.astype(BF16), pool_scale.reshape(1, dp))


def _pack_bf16_pair(lo, hi):
    lo_b = lax.bitcast_convert_type(lo, U32) >> 16
    hi_b = lax.bitcast_convert_type(hi, U32) & jnp.uint32(0xFFFF0000)
    return lo_b | hi_b


def _unpack_bf16_pair(p):
    lo = lax.bitcast_convert_type(p << 16, F32)
    hi = lax.bitcast_convert_type(p & jnp.uint32(0xFFFF0000), F32)
    return lo, hi


def _outproj_kernel(yg_ref, yp_ref, x_ref, gs_ref, m2_ref, m3_ref, m4_ref, m5_ref, gf_ref,
                    ws_ref, wp_ref, wrh_ref, wrl_ref, wg_ref, wu_ref, wd_ref,
                    xs1_ref, h2p_ref, lg_ref):
    yg = yg_ref[...].astype(F32)
    ms = jnp.mean(yg * yg, axis=-1, keepdims=True)
    a = (yg * lax.rsqrt(ms + EPS) * gs_ref[...]).astype(BF16)
    o = _dot(a, ws_ref[...]) + _dot(yp_ref[...], wp_ref[...])
    x1 = x_ref[...] + m2_ref[0] * o
    ms1 = jnp.mean(x1 * x1, axis=-1, keepdims=True)
    h2 = x1 * lax.rsqrt(ms1 + EPS) * gf_ref[...]
    h2 = h2 * (1.0 + m4_ref[0]) + m3_ref[0]
    h_hi, h_lo = _split2(h2)
    wrh = wrh_ref[...]
    lg_ref[...] = _dot_nt(wrh, h_hi) + _dot_nt(wrl_ref[...], h_hi) + _dot_nt(wrh, h_lo)
    act = (_silu(_dot(h_hi, wg_ref[...])) * _dot(h_hi, wu_ref[...])).astype(BF16)
    xs1_ref[...] = x1 + m5_ref[0] * _dot(act, wd_ref[...])
    hf = h_hi.astype(F32)
    half = hf.shape[1] // 2
    h2p_ref[...] = _pack_bf16_pair(hf[:, :half], hf[:, half:])


def _outproj(yg, yp, x2d, g_ssd, mods, g_ffn, w_ssd, w_pool, wr_hi, wr_lo, wsg, wsu, wsd,
             rows_per_batch, tm):
    n, d = x2d.shape
    dp = yp.shape[1]
    ne = wr_hi.shape[0]
    dsh = wsg.shape[1]
    tpb = rows_per_batch // tm
    row = lambda c: pl.BlockSpec((tm, c), lambda i: (i, 0))
    vec = pl.BlockSpec((1, d), lambda i: (0, 0))
    mod = pl.BlockSpec((1, 1, d), lambda i: (i // tpb, 0, 0))
    res = lambda r, c: pl.BlockSpec((r, c), lambda i: (0, 0), pipeline_mode=pl.Buffered(1))
    m2, m3, m4, m5 = mods
    return pl.pallas_call(
        _outproj_kernel,
        grid=(n // tm,),
        in_specs=[row(d), row(dp), row(d), vec, mod, mod, mod, mod, vec,
                  res(d, d), res(dp, d), res(ne, d), res(ne, d), res(d, dsh), res(d, dsh), res(dsh, d)],
        out_specs=[row(d), row(d // 2), pl.BlockSpec((ne, tm), lambda i: (0, i))],
        out_shape=[jax.ShapeDtypeStruct((n, d), F32),
                   jax.ShapeDtypeStruct((n, d // 2), U32),
                   jax.ShapeDtypeStruct((ne, n), F32)],
        compiler_params=_cparams(("parallel",)),
        name="outproj",
    )(yg, yp, x2d, g_ssd.reshape(1, d), m2, m3, m4, m5, g_ffn.reshape(1, d),
      w_ssd, w_pool, wr_hi, wr_lo, wsg, wsu, wsd)


def _topk_kernel(lg_ref, rb_ref, te_ref, tw_ref, rk_ref, cnt_ref, carry):
    ne, tm = lg_ref.shape
    gsz = ne // N_EXPERT_GROUPS

    @pl.when(pl.program_id(0) == 0)
    def _():
        carry[...] = jnp.zeros_like(carry)

    s = _sigmoid(lg_ref[...])
    biased = s + rb_ref[...]
    neg = jnp.float32(-jnp.inf)
    big = jnp.int32(1 << 20)
    gi = lax.broadcasted_iota(I32, (gsz, tm), 0)
    gscore = []
    for g in range(N_EXPERT_GROUPS):
        v = biased[g * gsz:(g + 1) * gsz, :]
        m1 = jnp.max(v, axis=0, keepdims=True)
        i1 = jnp.min(jnp.where(v == m1, gi, big), axis=0, keepdims=True)
        m2 = jnp.max(jnp.where(gi == i1, neg, v), axis=0, keepdims=True)
        gscore.append(m1 + m2)
    parts = []
    for g in range(N_EXPERT_GROUPS):
        rank = jnp.zeros((1, tm), I32)
        for h in range(N_EXPERT_GROUPS):
            if h == g:
                continue
            ahead = (gscore[h] > gscore[g]) | ((gscore[h] == gscore[g]) & (h < g))
            rank = rank + ahead.astype(I32)
        keep = rank < TOPK_GROUPS
        parts.append(jnp.where(keep, biased[g * gsz:(g + 1) * gsz, :], neg))
    masked = jnp.concatenate(parts, axis=0)
    ei = lax.broadcasted_iota(I32, (ne, tm), 0)
    idxs, wts = [], []
    msel = jnp.zeros((ne, tm), F32)
    for _ in range(TOP_K):
        m = jnp.max(masked, axis=0, keepdims=True)
        idx = jnp.min(jnp.where(masked == m, ei, big), axis=0, keepdims=True)
        sel = ei == idx
        wts.append(jnp.sum(jnp.where(sel, s, 0.0), axis=0, keepdims=True))
        idxs.append(idx)
        masked = jnp.where(sel, neg, masked)
        msel = jnp.where(sel, 1.0, msel)
    wsum = wts[0]
    for w in wts[1:]:
        wsum = wsum + w
    ti = lax.broadcasted_iota(I32, (tm, tm), 0)
    tj = lax.broadcasted_iota(I32, (tm, tm), 1)
    before = jnp.where(ti < tj, 1.0, 0.0).astype(BF16)
    base = carry[...]
    rank_in = _dot(msel.astype(BF16), before) + jnp.concatenate([base] * (tm // 128), axis=1)
    for k in range(TOP_K):
        sel = ei == idxs[k]
        te_ref[k:k + 1, :] = idxs[k]
        tw_ref[k:k + 1, :] = wts[k] / wsum * ROUTE_SCALE
        rk_ref[k:k + 1, :] = jnp.sum(jnp.where(sel, rank_in, 0.0), axis=0, keepdims=True).astype(I32)
    total = base + _dot(msel.astype(BF16), jnp.ones((tm, 128), BF16))
    carry[...] = total
    cnt_ref[...] = total


def _topk(lg_t, router_bias, tm):
    ne, n = lg_t.shape
    row8 = lambda dt: jax.ShapeDtypeStruct((TOP_K, n), dt)
    return pl.pallas_call(
        _topk_kernel,
        grid=(n // tm,),
        in_specs=[pl.BlockSpec((ne, tm), lambda i: (0, i)),
                  pl.BlockSpec((ne, 1), lambda i: (0, 0))],
        out_specs=[pl.BlockSpec((TOP_K, tm), lambda i: (0, i))] * 3
        + [pl.BlockSpec((ne, 128), lambda i: (0, 0))],
        out_shape=[row8(I32), row8(F32), row8(I32), jax.ShapeDtypeStruct((ne, 128), F32)],
        scratch_shapes=[pltpu.VMEM((ne, 128), F32)],
        compiler_params=_cparams(("arbitrary",)),
        name="topk",
    )(lg_t, router_bias.reshape(ne, 1).astype(F32))


def _moe_kernel(be_ref, ng_ref, a0_ref, *refs):
    del be_ref
    nw = TOK_WINDOW_ROWS
    tok_refs, tokn_refs = refs[:nw], refs[nw:2 * nw]
    h_hbm, wg_ref, wu_ref, wd_ref, o_ref, rows, sems = refs[2 * nw:]
    i = pl.program_id(0)
    nsteps = pl.num_programs(0)
    slot = i % 2

    def issue(trefs, a0, ngroups, dst_slot):
        off = a0 & (LANES - 1)
        nrows = ngroups * GATHER_GROUP
        for q, tref in enumerate(trefs):
            def body(r, carry, q=q, tref=tref):
                tok = tref[0, 0, r + off - q * LANES]
                pltpu.make_async_copy(h_hbm.at[pl.ds(tok, 1), :], rows.at[dst_slot, pl.ds(r, 1), :],
                                      sems.at[dst_slot]).start()
                return carry
            lax.fori_loop(jnp.maximum(q * LANES - off, 0), jnp.minimum((q + 1) * LANES - off, nrows), body, 0)

    @pl.when(i == 0)
    def _():
        rows[...] = jnp.zeros_like(rows)
        issue(tok_refs, a0_ref[0], ng_ref[0], 0)

    @pl.when(i + 1 < nsteps)
    def _():
        issue(tokn_refs, a0_ref[i + 1], ng_ref[i + 1], 1 - slot)

    ngr = ng_ref[i]

    @pl.when(ngr > 0)
    def _():
        def wait_body(g, carry):
            pltpu.make_async_copy(h_hbm.at[pl.ds(0, GATHER_GROUP), :],
                                  rows.at[slot, pl.ds(0, GATHER_GROUP), :], sems.at[slot]).wait()
            return carry
        lax.fori_loop(0, ngr, wait_body, 0)
        x_lo, x_hi = _unpack_bf16_pair(rows[slot])
        x_lo = x_lo.astype(BF16)
        x_hi = x_hi.astype(BF16)
        half = x_lo.shape[1]
        wg = wg_ref[0].astype(BF16)
        wu = wu_ref[0].astype(BF16)
        g = _dot(x_lo, wg[:half, :]) + _dot(x_hi, wg[half:, :])
        u = _dot(x_lo, wu[:half, :]) + _dot(x_hi, wu[half:, :])
        act = (_silu(g) * u).astype(BF16)
        y = _dot(act, wd_ref[0].astype(BF16))
        yb = y.astype(BF16).astype(F32)
        o_ref[...] = _pack_bf16_pair(yb[:, :half], yb[:, half:])

    @pl.when(ngr == 0)
    def _():
        o_ref[...] = jnp.zeros_like(o_ref)


def _moe(block_e, ngroups, a0, sorted_tok, h2p, w_gate, w_up, w_down):
    nblk = block_e.shape[0]
    mb = MOE_BLOCK
    ne, d, f = w_gate.shape
    half = h2p.shape[1]
    shift = LANES.bit_length() - 1

    def win(q, nxt):
        def imap(i, be, ng, a0):
            j = jnp.minimum(i + 1, nblk - 1) if nxt else i
            return ((a0[j] >> shift) + q, 0, 0)
        return pl.BlockSpec((1, 1, LANES), imap, memory_space=pltpu.SMEM)

    wins = [win(q, False) for q in range(TOK_WINDOW_ROWS)] + [win(q, True) for q in range(TOK_WINDOW_ROWS)]
    grid_spec = pltpu.PrefetchScalarGridSpec(
        num_scalar_prefetch=3,
        grid=(nblk,),
        in_specs=wins + [pl.BlockSpec(memory_space=pl.ANY),
                         pl.BlockSpec((1, d, f), lambda i, be, ng, a0: (be[i], 0, 0)),
                         pl.BlockSpec((1, d, f), lambda i, be, ng, a0: (be[i], 0, 0)),
                         pl.BlockSpec((1, f, d), lambda i, be, ng, a0: (be[i], 0, 0))],
        out_specs=pl.BlockSpec((mb, half), lambda i, be, ng, a0: (i, 0)),
        scratch_shapes=[pltpu.VMEM((2, mb, half), U32), pltpu.SemaphoreType.DMA((2,))],
    )
    return pl.pallas_call(
        _moe_kernel,
        grid_spec=grid_spec,
        out_shape=jax.ShapeDtypeStruct((nblk * mb, half), U32),
        compiler_params=_cparams(("arbitrary",)),
        name="moe",
    )(block_e, ngroups, a0, *([sorted_tok] * (2 * TOK_WINDOW_ROWS)), h2p, w_gate, w_up, w_down)


def _combine_kernel(es_ref, te_ref, rk_ref, ten_ref, rkn_ref, y_hbm, w_ref, xs1_ref, m5_ref, gf_ref, o_ref,
                    rows, sems):
    i = pl.program_id(0)
    nsteps = pl.num_programs(0)
    slot = i % 2
    tm = o_ref.shape[0]
    nrow = TOP_K * tm

    def issue(e_ref, r_ref, dst_slot):
        def body(r, carry):
            p = es_ref[e_ref[0, 0, r]] + r_ref[0, 0, r]
            pltpu.make_async_copy(y_hbm.at[pl.ds(p, 1), :], rows.at[dst_slot, pl.ds(r, 1), :],
                                  sems.at[dst_slot]).start()
            return carry
        lax.fori_loop(0, nrow, body, 0)

    @pl.when(i == 0)
    def _():
        issue(te_ref, rk_ref, 0)

    @pl.when(i + 1 < nsteps)
    def _():
        issue(ten_ref, rkn_ref, 1 - slot)

    pltpu.make_async_copy(y_hbm.at[pl.ds(0, nrow), :], rows.at[slot], sems.at[slot]).wait()
    w = w_ref[...]
    acc_lo = acc_hi = None
    for k in range(TOP_K):
        lo, hi = _unpack_bf16_pair(rows[slot, k * tm:(k + 1) * tm, :])
        wk = w[:, k:k + 1]
        acc_lo = wk * lo if acc_lo is None else acc_lo + wk * lo
        acc_hi = wk * hi if acc_hi is None else acc_hi + wk * hi
    routed = jnp.concatenate([acc_lo, acc_hi], axis=1)
    x = xs1_ref[...] + m5_ref[0] * routed
    ms = jnp.mean(x * x, axis=-1, keepdims=True)
    o_ref[...] = x * lax.rsqrt(ms + EPS) * gf_ref[...]


def _combine(expert_slot0, top_e, rank, y_packed, w_tok, xs1, m5, g_final, rows_per_batch, tm):
    n, d = xs1.shape
    half = y_packed.shape[1]
    nt = n // tm
    tpb = rows_per_batch // tm
    tiles = lambda a: jnp.transpose(a.reshape(TOP_K, nt, tm), (1, 0, 2)).reshape(nt, 1, TOP_K * tm)
    te3, rk3 = tiles(top_e), tiles(rank)
    cur = pl.BlockSpec((1, 1, TOP_K * tm), lambda i, es: (i, 0, 0), memory_space=pltpu.SMEM)
    nxt = pl.BlockSpec((1, 1, TOP_K * tm), lambda i, es: (jnp.minimum(i + 1, nt - 1), 0, 0),
                       memory_space=pltpu.SMEM)
    grid_spec = pltpu.PrefetchScalarGridSpec(
        num_scalar_prefetch=1,
        grid=(nt,),
        in_specs=[cur, cur, nxt, nxt,
                  pl.BlockSpec(memory_space=pl.ANY),
                  pl.BlockSpec((tm, TOP_K), lambda i, es: (i, 0)),
                  pl.BlockSpec((tm, d), lambda i, es: (i, 0)),
                  pl.BlockSpec((1, 1, d), lambda i, es: (i // tpb, 0, 0)),
                  pl.BlockSpec((1, d), lambda i, es: (0, 0))],
        out_specs=pl.BlockSpec((tm, d), lambda i, es: (i, 0)),
        scratch_shapes=[pltpu.VMEM((2, TOP_K * tm, half), U32), pltpu.SemaphoreType.DMA((2,))],
    )
    return pl.pallas_call(
        _combine_kernel,
        grid_spec=grid_spec,
        out_shape=jax.ShapeDtypeStruct((n, d), F32),
        compiler_params=_cparams(("arbitrary",)),
        name="combine",
    )(expert_slot0, te3, rk3, te3, rk3, y_packed, w_tok, xs1, m5, g_final.reshape(1, d))


def _dispatch_plan(top_e, counts, n_tok):
    mb = MOE_BLOCK
    ne = counts.shape[0]
    n_asg = TOP_K * n_tok
    nblk = -(-(n_asg + ne * (mb - 1)) // mb)
    tok_ids = jnp.broadcast_to(jnp.arange(n_tok, dtype=I32)[None, :], top_e.shape)
    keys = jnp.sort((top_e * n_tok + tok_ids).reshape(-1))
    sorted_tok = jnp.concatenate([keys % n_tok, jnp.zeros((TOK_WINDOW_ROWS * LANES,), I32)])
    sorted_tok = sorted_tok.reshape(-1, 1, LANES)
    start = jnp.cumsum(counts) - counts
    eblk = (counts + mb - 1) // mb
    pend = jnp.cumsum(eblk)
    pstart = pend - eblk
    n_real = pend[-1]
    bid = jnp.arange(nblk, dtype=I32)
    last_real = jnp.maximum(n_real - 1, 0)
    bsrc = jnp.minimum(bid, last_real)
    block_e = jnp.minimum(jnp.searchsorted(pend, bsrc, side="right"), ne - 1).astype(I32)
    off = (bsrc - pstart[block_e]) * mb
    nvalid = jnp.where(bid < n_real, jnp.clip(counts[block_e] - off, 0, mb), 0)
    ngroups = ((nvalid + GATHER_GROUP - 1) // GATHER_GROUP).astype(I32)
    a0 = (start[block_e] + off).astype(I32)
    return block_e, ngroups, a0, sorted_tok, (pstart * mb).astype(I32)


def kernel(x, c, ctx, c_ctx, w_ada, b_ada, g_mix, w_in, conv_w, conv_b, dt_bias, a_log, d_skip, g_ssd,
           pool_w, pool_scale, w_out, g_ffn, w_router, router_bias, w_exp_gate, w_exp_up, w_exp_down,
           w_sh_gate, w_sh_up, w_sh_down, g_final):
    bsz, seq, d = x.shape
    ctx_len = ctx.shape[1]
    assert w_ada.shape[0] == 1, "single-layer block"
    d_ssm = g_ssd.shape[1]
    heads = d_skip.shape[1]
    d_pool = pool_scale.shape[1]
    d_xbc = conv_w.shape[2]
    n = bsz * seq

    cvec = jnp.zeros((8, d), F32).at[:bsz].set(c).at[bsz].set(c_ctx)
    mod_all = _ada(cvec, w_ada[0], b_ada[0]).reshape(8, N_MOD, d)
    mod = mod_all[:bsz]
    mod_c = mod_all[bsz:bsz + 1]
    mk = lambda m, k: m[:, k:k + 1, :]

    w = w_in[0]
    c_dt = d_ssm + d_xbc
    w_main = jnp.concatenate([w[:, :c_dt], w[:, c_dt + 2 * heads:]], axis=1).astype(BF16)
    w_dt = jnp.pad(w[:, c_dt:c_dt + 2 * heads], ((0, 0), (0, 128 - 2 * heads)))

    main_c, dt_c = _inproj(ctx.reshape(bsz * ctx_len, d), g_mix[0], mk(mod_c, 0), mk(mod_c, 1),
                           w_main, w_dt, bsz * ctx_len, tm=256, tn=1024)
    h_zero = jnp.zeros((bsz, SSM_GROUPS, 2, SSM_STATE, 4 * SSM_HEADDIM), F32)
    h_ctx = _ssd(main_c.reshape(bsz, ctx_len, -1), dt_c.reshape(bsz, ctx_len, 128), conv_w[0], conv_b[0],
                 dt_bias[0], a_log[0], None, h_zero, False, d_ssm, d_ssm)

    x2d = x.reshape(n, d)
    main, dt_raw = _inproj(x2d, g_mix[0], mk(mod, 0), mk(mod, 1), w_main, w_dt, seq, tm=min(1024, seq), tn=1024)
    main3 = main.reshape(bsz, seq, -1)
    dsk = jnp.repeat(d_skip[0].astype(F32), SSM_HEADDIM).reshape(1, d_ssm)
    yg = _ssd(main3, dt_raw.reshape(bsz, seq, 128), conv_w[0], conv_b[0], dt_bias[0], a_log[0], dsk,
              h_ctx, True, d_ssm, d_ssm)
    yp = _pool(main3, pool_w[0], pool_scale[0], d_ssm + d_xbc)

    wo = w_out[0].astype(BF16)
    wr = w_router[0].T
    wr_hi = wr.astype(BF16)
    wr_lo = (wr - wr_hi.astype(F32)).astype(BF16)
    xs1, h2p, lg_t = _outproj(
        yg.reshape(n, d_ssm), yp.reshape(n, d_pool), x2d, g_ssd[0],
        (mk(mod, 2), mk(mod, 3), mk(mod, 4), mk(mod, 5)), g_ffn[0],
        wo[:d_ssm], wo[d_ssm:], wr_hi, wr_lo,
        w_sh_gate[0].astype(BF16), w_sh_up[0].astype(BF16), w_sh_down[0].astype(BF16), seq, tm=256)

    top_e, top_w, rank, cnt = _topk(lg_t, router_bias[0], tm=512)
    counts = cnt[:, 0].astype(I32)
    block_e, ngroups, a0, sorted_tok, expert_slot0 = _dispatch_plan(top_e, counts, n)
    y_packed = _moe(block_e, ngroups, a0, sorted_tok, h2p, w_exp_gate[0], w_exp_up[0], w_exp_down[0])
    out = _combine(expert_slot0, top_e, rank, y_packed, top_w.T, xs1, mk(mod, 5), g_final, seq, tm=128)
    return out.reshape(bsz, seq, d)
```

```python
import functools

import jax
import jax.numpy as jnp
from jax import lax
from jax.experimental import pallas as pl
from jax.experimental.pallas import tpu as pltpu

F32 = jnp.float32
BF16 = jnp.bfloat16
I32 = jnp.int32
U32 = jnp.uint32

EPS = 1e-6
GRID_W = 64
SSM_HEADDIM = 64
SSM_GROUPS = 8
SSM_STATE = 128
SSD_CHUNK = 128
POOL_WINDOWS = (2, 4, 8, 16)
TOP_K = 8
N_EXPERT_GROUPS = 8
TOPK_GROUPS = 4
ROUTE_SCALE = 2.5
N_MOD = 6
LANES = 128
SUBLANES = 8
MOE_BLOCK = 512
GATHER_GROUP = 64
MOE_SUB = 128
ROW_PAD = SUBLANES
TOK_WINDOW_ROWS = MOE_BLOCK // LANES + 2
V7X_VMEM_LIMIT = 56 * 1024 * 1024


def _cparams(sem, vmem=V7X_VMEM_LIMIT):
    return pltpu.CompilerParams(dimension_semantics=sem, vmem_limit_bytes=vmem)


def _sigmoid(x):
    return 1.0 / (1.0 + jnp.exp(-x))


def _silu(x):
    return x * _sigmoid(x)


def _split2(x):
    hi = x.astype(BF16)
    lo = (x - hi.astype(F32)).astype(BF16)
    return hi, lo


def _dot(a, b):
    return jnp.dot(a, b, preferred_element_type=F32)


def _dot_nt(a, b):
    return lax.dot_general(a, b, (((1,), (1,)), ((), ())), preferred_element_type=F32)


def _dot_tn(a, b):
    return lax.dot_general(a, b, (((0,), (0,)), ((), ())), preferred_element_type=F32)


def _dot3(a, b):
    a_hi, a_lo = _split2(a)
    b_hi, b_lo = _split2(b)
    return _dot(a_hi, b_hi) + _dot(a_lo, b_hi) + _dot(a_hi, b_lo)


def _ada_kernel(c_ref, w_ref, b_ref, o_ref):
    c = c_ref[...]
    o_ref[...] = _dot3(_silu(c), w_ref[...]) + b_ref[...]


def _ada(cvec, w_ada, b_ada, tn=1024):
    d, n = w_ada.shape
    return pl.pallas_call(
        _ada_kernel,
        grid=(n // tn,),
        in_specs=[pl.BlockSpec((8, d), lambda j: (0, 0)),
                  pl.BlockSpec((d, tn), lambda j: (0, j)),
                  pl.BlockSpec((1, tn), lambda j: (0, j))],
        out_specs=pl.BlockSpec((8, tn), lambda j: (0, j)),
        out_shape=jax.ShapeDtypeStruct((8, n), F32),
        compiler_params=_cparams(("parallel",)),
        name="ada",
    )(cvec, w_ada, b_ada.reshape(1, n))


def _inproj_kernel(x_ref, g_ref, sh_ref, sc_ref, w_ref, wdt_ref, o_ref, dt_ref, h_scr):
    @pl.when(pl.program_id(1) == 0)
    def _():
        x = x_ref[...]
        ms = jnp.mean(x * x, axis=-1, keepdims=True)
        y = x * lax.rsqrt(ms + EPS) * g_ref[...]
        h = y * (1.0 + sc_ref[0]) + sh_ref[0]
        h_hi, h_lo = _split2(h)
        h_scr[...] = h_hi
        w_hi, w_lo = _split2(wdt_ref[...])
        dt_ref[...] = _dot(h_hi, w_hi) + _dot(h_lo, w_hi) + _dot(h_hi, w_lo)

    o_ref[...] = _dot(h_scr[...], w_ref[...]).astype(o_ref.dtype)


def _inproj(x2d, g, shift, scale, w_main, w_dt, rows_per_batch, tm, tn):
    n, d = x2d.shape
    nc = w_main.shape[1]
    tpb = rows_per_batch // tm
    return pl.pallas_call(
        _inproj_kernel,
        grid=(n // tm, nc // tn),
        in_specs=[pl.BlockSpec((tm, d), lambda i, j: (i, 0)),
                  pl.BlockSpec((1, d), lambda i, j: (0, 0)),
                  pl.BlockSpec((1, 1, d), lambda i, j: (i // tpb, 0, 0)),
                  pl.BlockSpec((1, 1, d), lambda i, j: (i // tpb, 0, 0)),
                  pl.BlockSpec((d, tn), lambda i, j: (0, j)),
                  pl.BlockSpec((d, 128), lambda i, j: (0, 0))],
        out_specs=[pl.BlockSpec((tm, tn), lambda i, j: (i, j)),
                   pl.BlockSpec((tm, 128), lambda i, j: (i, 0))],
        out_shape=[jax.ShapeDtypeStruct((n, nc), BF16),
                   jax.ShapeDtypeStruct((n, 128), F32)],
        scratch_shapes=[pltpu.VMEM((tm, d), BF16)],
        compiler_params=_cparams(("parallel", "arbitrary")),
        name="inproj",
    )(x2d, g.reshape(1, d), shift, scale, w_main, w_dt)


def _expand_heads(v, base, width):
    t = v.shape[0]
    lane = lax.broadcasted_iota(I32, (t, 4 * width), 1)
    out = jnp.broadcast_to(v[:, base + 3:base + 4], (t, 4 * width))
    for j in (2, 1, 0):
        out = jnp.where(lane < (j + 1) * width, v[:, base + j:base + j + 1], out)
    return out


def _conv_silu(src_ref, w_ref, b_ref, dst_ref, length, tile):
    c = src_ref.shape[-1]
    w = w_ref[...]
    b = b_ref[...]
    rid = lax.broadcasted_iota(I32, (tile, c), 0)
    for r0 in range(0, length, tile):
        cur = src_ref[0, r0:r0 + tile, :].astype(F32)
        if r0 == 0:
            prev_row = jnp.zeros((1, c), F32)
        else:
            prev_row = src_ref[0, r0 - 16:r0, :].astype(F32)[15:16, :]
        if r0 + tile == length:
            next_row = jnp.zeros((1, c), F32)
        else:
            next_row = src_ref[0, r0 + tile:r0 + tile + 16, :].astype(F32)[0:1, :]
        up = jnp.where(rid == 0, prev_row, pltpu.roll(cur, 1, 0))
        dn = jnp.where(rid == tile - 1, next_row, pltpu.roll(cur, tile - 1, 0))
        o = up * w[0:1, :] + cur * w[1:2, :] + dn * w[2:3, :] + b
        dst_ref[r0:r0 + tile, :] = _silu(o).astype(dst_ref.dtype)


def _softplus(x):
    return jnp.maximum(x, 0.0) + jnp.log(1.0 + jnp.exp(-jnp.abs(x)))


def _ssd_kernel(need_y, length, *refs):
    t = SSD_CHUNK
    nch = length // t
    hw = 4 * SSM_HEADDIM
    if need_y:
        (xr, br, cr, zr, cwx, cbx, cwb, cbb, cwc, cbc, dtc, dtr, bias_c, a_c, bias_r, a_r, dsk, h0,
         y_ref, xs_s, b_s, c_s, dt_s, csr_s, csc_s, st_s, dec_s) = refs
    else:
        (xr, br, cwx, cbx, cwb, cbb, dtc, dtr, bias_c, a_c, bias_r, a_r, h0,
         hfin, xs_s, b_s, dt_s, csr_s, csc_s, st_s, dec_s) = refs

    ctile = min(256, length)
    _conv_silu(xr, cwx, cbx, xs_s, length, ctile)
    _conv_silu(br, cwb, cbb, b_s, length, ctile)
    if need_y:
        _conv_silu(cr, cwc, cbc, c_s, length, ctile)

    dt_s[...] = _softplus(dtc[0, 0] + bias_c[0])
    da_r = _softplus(dtr[0, 0] + bias_r[0]) * a_r[0]
    da2 = da_r.reshape(nch * 8, t)
    kk = lax.broadcasted_iota(I32, (t, 2 * t), 0)
    ll = lax.broadcasted_iota(I32, (t, 2 * t), 1)
    tri = jnp.where(ll < t, jnp.where(kk <= ll, 1.0, 0.0), jnp.where(kk >= ll - t, 1.0, 0.0)).astype(BF16)
    p0 = da2.astype(BF16)
    r1 = da2 - p0.astype(F32)
    p1 = r1.astype(BF16)
    p2 = (r1 - p1.astype(F32)).astype(BF16)
    cum = _dot(p0, tri) + _dot(p1, tri) + _dot(p2, tri)
    rowj = lax.broadcasted_iota(I32, (nch * 8, t), 0) & 7
    csr = jnp.where(rowj < 4, cum[:, :t], cum[:, t:])
    csr_s[...] = csr.reshape(nch, 8, t)

    def chunk_terms(c):
        dt = dt_s[c]
        csc = csc_s[c]
        return (_expand_heads(dt, 0, SSM_HEADDIM), _expand_heads(dt, 4, SSM_HEADDIM),
                _expand_heads(csc, 0, SSM_HEADDIM), _expand_heads(csc, 4, SSM_HEADDIM))

    def phase_a(c, carry):
        r0 = pl.multiple_of(c * t, t)
        csr_c = csr_s[c]
        csr_pad = jnp.concatenate([csr_c, jnp.zeros((t - 8, t), F32)], axis=0)
        csc_s[c] = csr_pad.T[:, 0:8]
        dte_f, dte_b, cse_f, cse_b = chunk_terms(c)
        xs = xs_s[pl.ds(r0, t), :]
        bc = b_s[pl.ds(r0, t), :]
        last_f = cse_f[t - 1:t, :]
        first_b = cse_b[0:1, :]
        xw_f = (xs * (dte_f * jnp.exp(last_f - cse_f))).astype(BF16)
        xw_b = (xs * (dte_b * jnp.exp(first_b - cse_b))).astype(BF16)
        st_s[c, 0] = _dot_tn(bc, xw_f)
        st_s[c, 1] = _dot_tn(bc, xw_b)
        dec_s[c, 0:1, :] = jnp.exp(last_f)
        dec_s[c, 1:2, :] = jnp.exp(first_b)
        return carry

    lax.fori_loop(0, nch, phase_a, 0, unroll=2)

    def rec_f(c, s):
        loc = st_s[c, 0]
        st_s[c, 0] = s
        return dec_s[c, 0:1, :] * s + loc

    def rec_b(k, s):
        c = nch - 1 - k
        loc = st_s[c, 1]
        st_s[c, 1] = s
        return dec_s[c, 1:2, :] * s + loc

    s_f = lax.fori_loop(0, nch, rec_f, h0[0, 0, 0])
    s_b = lax.fori_loop(0, nch, rec_b, h0[0, 0, 1])
    if not need_y:
        hfin[0, 0, 0] = s_f
        hfin[0, 0, 1] = s_b
        return

    li = lax.broadcasted_iota(I32, (t, t), 0)
    si = lax.broadcasted_iota(I32, (t, t), 1)
    dskip = dsk[...]

    def phase_c(c, carry):
        r0 = pl.multiple_of(c * t, t)
        dte_f, dte_b, cse_f, cse_b = chunk_terms(c)
        csr_c = csr_s[c]
        csc_c = csc_s[c]
        xs = xs_s[pl.ds(r0, t), :]
        bc = b_s[pl.ds(r0, t), :]
        cc = c_s[pl.ds(r0, t), :]
        cb = _dot_nt(cc, bc)
        xdt_f = xs * dte_f
        xdt_b = xs * dte_b
        ys = []
        for j in range(4):
            d_f = csc_c[:, j:j + 1] - csr_c[j:j + 1, :]
            att_f = jnp.where(si <= li, cb * jnp.exp(jnp.minimum(d_f, 0.0)), 0.0)
            d_b = csc_c[:, 4 + j:5 + j] - csr_c[4 + j:5 + j, :]
            att_b = jnp.where(si >= li, cb * jnp.exp(jnp.minimum(d_b, 0.0)), 0.0)
            lhs = jnp.concatenate([att_f, att_b], axis=1).astype(BF16)
            sl = slice(j * SSM_HEADDIM, (j + 1) * SSM_HEADDIM)
            rhs = jnp.concatenate([xdt_f[:, sl], xdt_b[:, sl]], axis=0).astype(BF16)
            ys.append(_dot(lhs, rhs))
        y = jnp.concatenate(ys, axis=1)
        y = y + _dot(cc, st_s[c, 0].astype(BF16)) * jnp.exp(cse_f)
        y = y + _dot(cc, st_s[c, 1].astype(BF16)) * jnp.exp(cse_b)
        y = y + xs * dskip
        z = zr[0, pl.ds(r0, t), :].astype(F32)
        y_ref[0, pl.ds(r0, t), :] = (y * _silu(z)).astype(y_ref.dtype)
        return carry

    lax.fori_loop(0, nch, phase_c, 0, unroll=2)


def _ssd(main3, dt_raw, conv_w, conv_b, dt_bias, a_log, d_skip, h0, need_y, d_ssm, xbc_col0):
    nb, length, _ = main3.shape
    g_n, t = SSM_GROUPS, SSD_CHUNK
    nch = length // t
    heads = d_ssm // SSM_HEADDIM
    hpg = heads // g_n
    assert hpg == 4 and SSM_STATE == 128
    hw = hpg * SSM_HEADDIM
    dt5 = dt_raw[..., :2 * heads].reshape(nb, nch, t, 2, g_n, hpg)
    dtc = jnp.transpose(dt5, (0, 4, 1, 2, 3, 5)).reshape(nb, g_n, nch, t, 8)
    dtr = jnp.transpose(dt5, (0, 4, 1, 3, 5, 2)).reshape(nb, g_n, nch, 8, t)
    par = lambda p: jnp.transpose(p.reshape(2, g_n, hpg), (1, 0, 2)).reshape(g_n, 8)
    bias = par(dt_bias.astype(F32))
    a = par(-jnp.exp(a_log.astype(F32)))
    bias_c, a_c = bias.reshape(g_n, 1, 8), a.reshape(g_n, 1, 8)
    bias_r, a_r = bias.reshape(g_n, 8, 1), a.reshape(g_n, 8, 1)
    nx = d_ssm
    nbc = g_n * SSM_STATE
    cwx, cwb, cwc = conv_w[:, :nx], conv_w[:, nx:nx + nbc], conv_w[:, nx + nbc:]
    cb2 = conv_b.reshape(1, -1)
    cbx, cbb, cbc = cb2[:, :nx], cb2[:, nx:nx + nbc], cb2[:, nx + nbc:]
    xcol = xbc_col0 // hw
    bcol = (xbc_col0 + nx) // SSM_STATE
    ccol = (xbc_col0 + nx + nbc) // SSM_STATE

    seq = lambda shape, imap: pl.BlockSpec(shape, imap)
    x_spec = seq((1, length, hw), lambda b, g: (b, 0, xcol + g))
    b_spec = seq((1, length, SSM_STATE), lambda b, g: (b, 0, bcol + g))
    c_spec = seq((1, length, SSM_STATE), lambda b, g: (b, 0, ccol + g))
    z_spec = seq((1, length, hw), lambda b, g: (b, 0, g))
    wx_spec = [seq((3, hw), lambda b, g: (0, g)), seq((1, hw), lambda b, g: (0, g))]
    wn_spec = [seq((3, SSM_STATE), lambda b, g: (0, g)), seq((1, SSM_STATE), lambda b, g: (0, g))]
    dt_specs = [seq((1, 1, nch, t, 8), lambda b, g: (b, g, 0, 0, 0)),
                seq((1, 1, nch, 8, t), lambda b, g: (b, g, 0, 0, 0)),
                seq((1, 1, 8), lambda b, g: (g, 0, 0)), seq((1, 1, 8), lambda b, g: (g, 0, 0)),
                seq((1, 8, 1), lambda b, g: (g, 0, 0)), seq((1, 8, 1), lambda b, g: (g, 0, 0))]
    h_spec = seq((1, 1, 2, SSM_STATE, hw), lambda b, g: (b, g, 0, 0, 0))
    common_scr = [pltpu.VMEM((nch, t, 8), F32),
                  pltpu.VMEM((nch, 8, t), F32),
                  pltpu.VMEM((nch, t, 8), F32),
                  pltpu.VMEM((nch, 2, SSM_STATE, hw), F32),
                  pltpu.VMEM((nch, 8, hw), F32)]
    if need_y:
        in_specs = ([x_spec, b_spec, c_spec, z_spec] + wx_spec + wn_spec + wn_spec + dt_specs
                    + [seq((1, hw), lambda b, g: (0, g)), h_spec])
        args = (main3, main3, main3, main3, cwx, cbx, cwb, cbb, cwc, cbc, dtc, dtr,
                bias_c, a_c, bias_r, a_r, d_skip, h0)
        out_specs = seq((1, length, hw), lambda b, g: (b, 0, g))
        out_shape = jax.ShapeDtypeStruct((nb, length, d_ssm), BF16)
        scratch = [pltpu.VMEM((length, hw), F32), pltpu.VMEM((length, SSM_STATE), BF16),
                   pltpu.VMEM((length, SSM_STATE), BF16)] + common_scr
    else:
        in_specs = [x_spec, b_spec] + wx_spec + wn_spec + dt_specs + [h_spec]
        args = (main3, main3, cwx, cbx, cwb, cbb, dtc, dtr, bias_c, a_c, bias_r, a_r, h0)
        out_specs = h_spec
        out_shape = jax.ShapeDtypeStruct((nb, g_n, 2, SSM_STATE, hw), F32)
        scratch = [pltpu.VMEM((length, hw), F32), pltpu.VMEM((length, SSM_STATE), BF16)] + common_scr
    return pl.pallas_call(
        functools.partial(_ssd_kernel, need_y, length),
        grid=(nb, g_n),
        in_specs=in_specs,
        out_specs=out_specs,
        out_shape=out_shape,
        scratch_shapes=scratch,
        compiler_params=_cparams(("parallel", "parallel")),
        name="ssd_y" if need_y else "ssd_state",
    )(*args)


def _pool_kernel(length, u_ref, pw_ref, ps_ref, o_ref, pad_s):
    gw = GRID_W
    rows = length // gw
    cg = pw_ref.shape[-1]
    halo = (max(POOL_WINDOWS) // 2) * gw
    tile = min(256, length)
    zeros = jnp.zeros((halo, cg), F32)
    pad_s[0:halo, :] = zeros
    pad_s[halo + length:halo + length + halo, :] = zeros
    for g, w in enumerate(POOL_WINDOWS):
        lo = -(w // 2)
        csl = slice(g * cg, (g + 1) * cg)
        for r0 in range(0, length, tile):
            pad_s[halo + r0:halo + r0 + tile, :] = u_ref[0, r0:r0 + tile, csl].astype(F32)
        pw = pw_ref[g]
        scale = ps_ref[:, csl]

        def body(i, carry, lo=lo, w=w, csl=csl, pw=pw, scale=scale):
            r0 = pl.multiple_of(i * tile, tile)
            acc = pad_s[pl.ds(halo + r0 + lo * gw, tile), :]
            for k in range(lo + 1, lo + w):
                acc = acc + pad_s[pl.ds(halo + r0 + k * gw, tile), :]
            l_idx = r0 + lax.broadcasted_iota(I32, (tile, cg), 0)
            ii = l_idx >> (gw.bit_length() - 1)
            jj = l_idx & (gw - 1)
            tot = acc
            for k in range(lo, lo + w):
                if k == 0:
                    continue
                sh = pltpu.roll(acc, (-k) % tile, 0)
                ok = (jj >= -k) if k < 0 else (jj < gw - k)
                tot = tot + jnp.where(ok, sh, 0.0)
            cnt_i = jnp.minimum(ii + lo + w, rows) - jnp.maximum(ii + lo, 0)
            cnt_j = jnp.minimum(jj + lo + w, gw) - jnp.maximum(jj + lo, 0)
            u = pad_s[pl.ds(halo + r0, tile), :]
            d = tot / (cnt_i * cnt_j).astype(F32) - u
            y = _dot(d.astype(BF16), pw) * scale
            o_ref[0, pl.ds(r0, tile), csl] = y.astype(o_ref.dtype)
            return carry

        lax.fori_loop(0, length // tile, body, 0)


def _pool(main3, pool_w, pool_scale, col0):
    nb, length, _ = main3.shape
    ng, cg, _ = pool_w.shape
    dp = ng * cg
    halo = (max(POOL_WINDOWS) // 2) * GRID_W
    return pl.pallas_call(
        functools.partial(_pool_kernel, length),
        grid=(nb,),
        in_specs=[pl.BlockSpec((1, length, dp), lambda b: (b, 0, col0 // dp)),
                  pl.BlockSpec((ng, cg, cg), lambda b: (0, 0, 0)),
                  pl.BlockSpec((1, dp), lambda b: (0, 0))],
        out_specs=pl.BlockSpec((1, length, dp), lambda b: (b, 0, 0)),
        out_shape=jax.ShapeDtypeStruct((nb, length, dp), BF16),
        scratch_shapes=[pltpu.VMEM((length + 2 * halo, cg), F32)],
        compiler_params=_cparams(("parallel",)),
        name="pool",
    )(main3, pool_w.astype(BF16), pool_scale.reshape(1, dp))


def _pack_bf16_pair(lo, hi):
    lo_b = lax.bitcast_convert_type(lo, U32) >> 16
    hi_b = lax.bitcast_convert_type(hi, U32) & jnp.uint32(0xFFFF0000)
    return lo_b | hi_b


def _unpack_bf16_pair(p):
    lo = lax.bitcast_convert_type(p << 16, F32)
    hi = lax.bitcast_convert_type(p & jnp.uint32(0xFFFF0000), F32)
    return lo, hi


def _store_row_tiles(ref, row0, value):
    m = value.shape[0]
    for c in range(SUBLANES):
        ref[pl.ds(row0 * SUBLANES + c, m, stride=SUBLANES), :] = value[:, c * LANES:(c + 1) * LANES]


def _load_row_tiles(ref, row0, m):
    return jnp.concatenate(
        [ref[pl.ds(row0 * SUBLANES + c, m, stride=SUBLANES), :] for c in range(SUBLANES)], axis=1)


def _outproj_kernel(yg_ref, yp_ref, x_ref, gs_ref, m2_ref, m3_ref, m4_ref, m5_ref, gf_ref,
                    ws_ref, wp_ref, wrh_ref, wrl_ref, wg_ref, wu_ref, wd_ref,
                    xs1_ref, h2p_ref, lg_ref):
    yg = yg_ref[...].astype(F32)
    ms = jnp.mean(yg * yg, axis=-1, keepdims=True)
    a = (yg * lax.rsqrt(ms + EPS) * gs_ref[...]).astype(BF16)
    o = _dot(a, ws_ref[...]) + _dot(yp_ref[...], wp_ref[...])
    x1 = x_ref[...] + m2_ref[0] * o
    ms1 = jnp.mean(x1 * x1, axis=-1, keepdims=True)
    h2 = x1 * lax.rsqrt(ms1 + EPS) * gf_ref[...]
    h2 = h2 * (1.0 + m4_ref[0]) + m3_ref[0]
    h_hi, h_lo = _split2(h2)
    wrh = wrh_ref[...]
    lg_ref[...] = _dot_nt(wrh, h_hi) + _dot_nt(wrl_ref[...], h_hi) + _dot_nt(wrh, h_lo)
    act = (_silu(_dot(h_hi, wg_ref[...])) * _dot(h_hi, wu_ref[...])).astype(BF16)
    xs1_ref[...] = x1 + m5_ref[0] * _dot(act, wd_ref[...])
    hf = h_hi.astype(F32)
    half = hf.shape[1] // 2
    _store_row_tiles(h2p_ref, 0, _pack_bf16_pair(hf[:, :half], hf[:, half:]))


def _outproj(yg, yp, x2d, g_ssd, mods, g_ffn, w_ssd, w_pool, wr_hi, wr_lo, wsg, wsu, wsd,
             rows_per_batch, tm):
    n, d = x2d.shape
    dp = yp.shape[1]
    ne = wr_hi.shape[0]
    dsh = wsg.shape[1]
    tpb = rows_per_batch // tm
    row = lambda c: pl.BlockSpec((tm, c), lambda i: (i, 0))
    vec = pl.BlockSpec((1, d), lambda i: (0, 0))
    mod = pl.BlockSpec((1, 1, d), lambda i: (i // tpb, 0, 0))
    res = lambda r, c: pl.BlockSpec((r, c), lambda i: (0, 0), pipeline_mode=pl.Buffered(1))
    m2, m3, m4, m5 = mods
    return pl.pallas_call(
        _outproj_kernel,
        grid=(n // tm,),
        in_specs=[row(d), row(dp), row(d), vec, mod, mod, mod, mod, vec,
                  res(d, d), res(dp, d), res(ne, d), res(ne, d), res(d, dsh), res(d, dsh), res(dsh, d)],
        out_specs=[row(d), pl.BlockSpec((tm * SUBLANES, LANES), lambda i: (i, 0)),
                   pl.BlockSpec((ne, tm), lambda i: (0, i))],
        out_shape=[jax.ShapeDtypeStruct((n, d), F32),
                   jax.ShapeDtypeStruct((n * SUBLANES, LANES), U32),
                   jax.ShapeDtypeStruct((ne, n), F32)],
        compiler_params=_cparams(("parallel",)),
        name="outproj",
    )(yg, yp, x2d, g_ssd.reshape(1, d), m2, m3, m4, m5, g_ffn.reshape(1, d),
      w_ssd, w_pool, wr_hi, wr_lo, wsg, wsu, wsd)


def _topk_kernel(lg_ref, rb_ref, te_ref, tw_ref, rk_ref, cnt_ref, carry):
    ne, tm = lg_ref.shape
    gsz = ne // N_EXPERT_GROUPS

    @pl.when(pl.program_id(0) == 0)
    def _():
        carry[...] = jnp.zeros_like(carry)

    s = _sigmoid(lg_ref[...])
    biased = s + rb_ref[...]
    neg = jnp.float32(-jnp.inf)
    big = jnp.int32(1 << 20)
    gi = lax.broadcasted_iota(I32, (gsz, tm), 0)
    gscore = []
    for g in range(N_EXPERT_GROUPS):
        v = biased[g * gsz:(g + 1) * gsz, :]
        m1 = jnp.max(v, axis=0, keepdims=True)
        i1 = jnp.min(jnp.where(v == m1, gi, big), axis=0, keepdims=True)
        m2 = jnp.max(jnp.where(gi == i1, neg, v), axis=0, keepdims=True)
        gscore.append(m1 + m2)
    parts = []
    for g in range(N_EXPERT_GROUPS):
        rank = jnp.zeros((1, tm), I32)
        for h in range(N_EXPERT_GROUPS):
            if h == g:
                continue
            ahead = (gscore[h] > gscore[g]) | ((gscore[h] == gscore[g]) & (h < g))
            rank = rank + ahead.astype(I32)
        keep = rank < TOPK_GROUPS
        parts.append(jnp.where(keep, biased[g * gsz:(g + 1) * gsz, :], neg))
    masked = jnp.concatenate(parts, axis=0)
    ei = lax.broadcasted_iota(I32, (ne, tm), 0)
    idxs, wts = [], []
    msel = jnp.zeros((ne, tm), F32)
    for _ in range(TOP_K):
        m = jnp.max(masked, axis=0, keepdims=True)
        idx = jnp.min(jnp.where(masked == m, ei, big), axis=0, keepdims=True)
        sel = ei == idx
        wts.append(jnp.sum(jnp.where(sel, s, 0.0), axis=0, keepdims=True))
        idxs.append(idx)
        masked = jnp.where(sel, neg, masked)
        msel = jnp.where(sel, 1.0, msel)
    wsum = wts[0]
    for w in wts[1:]:
        wsum = wsum + w
    ti = lax.broadcasted_iota(I32, (tm, tm), 0)
    tj = lax.broadcasted_iota(I32, (tm, tm), 1)
    before = jnp.where(ti < tj, 1.0, 0.0).astype(BF16)
    base = carry[...]
    rank_in = _dot(msel.astype(BF16), before) + jnp.concatenate([base] * (tm // 128), axis=1)
    for k in range(TOP_K):
        sel = ei == idxs[k]
        te_ref[k:k + 1, :] = idxs[k]
        tw_ref[k:k + 1, :] = wts[k] / wsum * ROUTE_SCALE
        rk_ref[k:k + 1, :] = jnp.sum(jnp.where(sel, rank_in, 0.0), axis=0, keepdims=True).astype(I32)
    total = base + _dot(msel.astype(BF16), jnp.ones((tm, 128), BF16))
    carry[...] = total
    cnt_ref[...] = total


def _topk(lg_t, router_bias, tm):
    ne, n = lg_t.shape
    row8 = lambda dt: jax.ShapeDtypeStruct((TOP_K, n), dt)
    return pl.pallas_call(
        _topk_kernel,
        grid=(n // tm,),
        in_specs=[pl.BlockSpec((ne, tm), lambda i: (0, i)),
                  pl.BlockSpec((ne, 1), lambda i: (0, 0))],
        out_specs=[pl.BlockSpec((TOP_K, tm), lambda i: (0, i))] * 3
        + [pl.BlockSpec((ne, 128), lambda i: (0, 0))],
        out_shape=[row8(I32), row8(F32), row8(I32), jax.ShapeDtypeStruct((ne, 128), F32)],
        scratch_shapes=[pltpu.VMEM((ne, 128), F32)],
        compiler_params=_cparams(("arbitrary",)),
        name="topk",
    )(lg_t, router_bias.reshape(ne, 1).astype(F32))


def _log2(n):
    assert n & (n - 1) == 0
    return n.bit_length() - 1


def _gather_groups(nvalid):
    return (nvalid + (GATHER_GROUP - 1)) >> _log2(GATHER_GROUP)


def _gather_rows(nvalid):
    return _gather_groups(nvalid) << _log2(GATHER_GROUP)


def _moe_kernel(be_ref, nv_ref, a0_ref, tok_ref, tokn_ref, h_hbm, wg_ref, wu_ref, wd_ref, y_hbm,
                rows, stage, wg_s, wu_s, wd_s, gsem, osem):
    i = pl.program_id(0)
    nsteps = pl.num_programs(0)
    slot = i % 2
    id_group = SUBLANES
    lines = GATHER_GROUP * SUBLANES

    def issue(tref, a0, nvalid, dst):
        lead = a0 & (id_group - 1)
        w0 = (a0 & (LANES - 1)) - lead
        ngrp = gather_groups(a0, nvalid)

        def body(g, carry):
            w = w0 + g * id_group
            q = w >> _log2(LANES)
            l0 = w & (LANES - 1)
            for u in range(id_group):
                tok = tref[q, 0, l0 + u]
                dst_row = ROW_PAD - lead + g * id_group + u
                pltpu.make_async_copy(
                    h_hbm.at[pl.ds(pl.multiple_of(tok * SUBLANES, SUBLANES), SUBLANES), :],
                    rows.at[dst, pl.ds(pl.multiple_of(dst_row * SUBLANES, SUBLANES), SUBLANES), :],
                    gsem.at[dst]).start()
            return carry
        lax.fori_loop(0, ngrp, body, 0)

    def gather_groups(a0, nvalid):
        lead = a0 & (id_group - 1)
        return (lead + _gather_rows(nvalid) + id_group - 1) >> _log2(id_group)

    def out_copy(g, a0, src_slot):
        return pltpu.make_async_copy(
            stage.at[src_slot, pl.ds(pl.multiple_of(g * lines, lines), lines), :],
            y_hbm.at[pl.ds(pl.multiple_of((a0 + g * GATHER_GROUP) * SUBLANES, SUBLANES), lines), :],
            osem.at[src_slot])

    @pl.when(i == 0)
    def _():
        rows[...] = jnp.zeros_like(rows)
        stage[1, 0:lines, :] = jnp.zeros((lines, LANES), U32)
        tail = pltpu.make_async_copy(stage.at[1, pl.ds(0, lines), :],
                                     y_hbm.at[pl.ds(y_hbm.shape[0] - lines, lines), :], osem.at[1])
        tail.start()
        tail.wait()

        @pl.when(nv_ref[0] > 0)
        def _():
            issue(tok_ref, a0_ref[0], nv_ref[0], 0)

    @pl.when(i + 1 < nsteps)
    def _():
        @pl.when(nv_ref[i + 1] > 0)
        def _():
            issue(tokn_ref, a0_ref[i + 1], nv_ref[i + 1], 1 - slot)

    nv = nv_ref[i]
    a0 = a0_ref[i]
    half = wg_s.shape[0] // 2

    @pl.when(nv > 0)
    def _():
        prev_e = be_ref[jnp.maximum(i - 1, 0)]

        @pl.when((i == 0) | (be_ref[i] != prev_e))
        def _():
            wg_s[...] = wg_ref[0].astype(BF16)
            wu_s[...] = wu_ref[0].astype(BF16)
            wd_s[...] = wd_ref[0].astype(BF16)

        def wait_body(g, carry):
            pltpu.make_async_copy(h_hbm.at[pl.ds(0, id_group * SUBLANES), :],
                                  rows.at[slot, pl.ds(0, id_group * SUBLANES), :], gsem.at[slot]).wait()
            return carry
        lax.fori_loop(0, gather_groups(a0, nv), wait_body, 0)

        def sub(sb, carry):
            x_lo, x_hi = _unpack_bf16_pair(_load_row_tiles(rows.at[slot], ROW_PAD + sb * MOE_SUB, MOE_SUB))
            x_lo = x_lo.astype(BF16)
            x_hi = x_hi.astype(BF16)
            g = _dot(x_lo, wg_s[:half, :]) + _dot(x_hi, wg_s[half:, :])
            u = _dot(x_lo, wu_s[:half, :]) + _dot(x_hi, wu_s[half:, :])
            act = (_silu(g) * u).astype(BF16)
            yb = _dot(act, wd_s[...]).astype(BF16).astype(F32)
            _store_row_tiles(stage.at[slot], sb * MOE_SUB, _pack_bf16_pair(yb[:, :half], yb[:, half:]))
            return carry
        lax.fori_loop(0, (nv + MOE_SUB - 1) >> _log2(MOE_SUB), sub, 0)

    @pl.when(i > 0)
    def _():
        prev = jnp.maximum(i - 1, 0)

        def wait_prev(g, carry):
            out_copy(0, 0, 1 - slot).wait()
            return carry
        lax.fori_loop(0, _gather_groups(nv_ref[prev]), wait_prev, 0)

    ngo = _gather_groups(nv)

    def start_out(g, carry):
        out_copy(g, a0, slot).start()
        return carry
    lax.fori_loop(0, ngo, start_out, 0)

    @pl.when(i == nsteps - 1)
    def _():
        def wait_own(g, carry):
            out_copy(0, 0, slot).wait()
            return carry
        lax.fori_loop(0, ngo, wait_own, 0)


def _moe(block_e, nvalid, a0, sorted_tok, h2p, w_gate, w_up, w_down, n_slots):
    nblk = block_e.shape[0]
    mb = MOE_BLOCK
    ne, d, f = w_gate.shape
    assert d == 2 * SUBLANES * LANES and mb % MOE_SUB == 0 and MOE_SUB % GATHER_GROUP == 0
    shift = LANES.bit_length() - 1

    def win(nxt):
        def imap(i, be, nv, a0):
            j = jnp.minimum(i + 1, nblk - 1) if nxt else i
            return (a0[j] >> shift, 0, 0)
        return pl.BlockSpec((pl.Element(TOK_WINDOW_ROWS), pl.Element(1), pl.Element(LANES)), imap,
                            memory_space=pltpu.SMEM)

    wspec = lambda r, c: pl.BlockSpec((1, r, c), lambda i, be, nv, a0: (be[i], 0, 0))
    grid_spec = pltpu.PrefetchScalarGridSpec(
        num_scalar_prefetch=3,
        grid=(nblk,),
        in_specs=[win(False), win(True), pl.BlockSpec(memory_space=pl.ANY), wspec(d, f), wspec(d, f), wspec(f, d)],
        out_specs=pl.BlockSpec(memory_space=pl.ANY),
        scratch_shapes=[pltpu.VMEM((2, (mb + 2 * ROW_PAD) * SUBLANES, LANES), U32),
                        pltpu.VMEM((2, mb * SUBLANES, LANES), U32),
                        pltpu.VMEM((d, f), BF16), pltpu.VMEM((d, f), BF16), pltpu.VMEM((f, d), BF16),
                        pltpu.SemaphoreType.DMA((2,)), pltpu.SemaphoreType.DMA((2,))],
    )
    return pl.pallas_call(
        _moe_kernel,
        grid_spec=grid_spec,
        out_shape=jax.ShapeDtypeStruct(((n_slots + GATHER_GROUP) * SUBLANES, LANES), U32),
        compiler_params=_cparams(("arbitrary",)),
        name="moe",
    )(block_e, nvalid, a0, sorted_tok, sorted_tok, h2p, w_gate, w_up, w_down)


def _combine_kernel(es_ref, te_ref, rk_ref, ten_ref, rkn_ref, y_hbm, w_ref, xs1_ref, m5_ref, gf_ref, o_ref,
                    rows, sems):
    i = pl.program_id(0)
    nsteps = pl.num_programs(0)
    slot = i % 2
    tm = o_ref.shape[0]
    nrow = TOP_K * tm

    def issue(e_ref, r_ref, dst_slot):
        def body(r, carry):
            p = es_ref[e_ref[0, 0, r]] + r_ref[0, 0, r]
            pltpu.make_async_copy(
                y_hbm.at[pl.ds(pl.multiple_of(p * SUBLANES, SUBLANES), SUBLANES), :],
                rows.at[dst_slot, pl.ds(pl.multiple_of(r * SUBLANES, SUBLANES), SUBLANES), :],
                sems.at[dst_slot]).start()
            return carry
        lax.fori_loop(0, nrow, body, 0, unroll=8)

    @pl.when(i == 0)
    def _():
        issue(te_ref, rk_ref, 0)

    @pl.when(i + 1 < nsteps)
    def _():
        issue(ten_ref, rkn_ref, 1 - slot)

    pltpu.make_async_copy(y_hbm.at[pl.ds(0, nrow * SUBLANES), :], rows.at[slot], sems.at[slot]).wait()
    w = w_ref[...]
    acc_lo = acc_hi = None
    for k in range(TOP_K):
        lo, hi = _unpack_bf16_pair(_load_row_tiles(rows.at[slot], k * tm, tm))
        wk = w[:, k:k + 1]
        acc_lo = wk * lo if acc_lo is None else acc_lo + wk * lo
        acc_hi = wk * hi if acc_hi is None else acc_hi + wk * hi
    routed = jnp.concatenate([acc_lo, acc_hi], axis=1)
    x = xs1_ref[...] + m5_ref[0] * routed
    ms = jnp.mean(x * x, axis=-1, keepdims=True)
    o_ref[...] = x * lax.rsqrt(ms + EPS) * gf_ref[...]


def _combine(expert_slot0, top_e, rank, y_packed, w_tok, xs1, m5, g_final, rows_per_batch, tm):
    n, d = xs1.shape
    assert d == 2 * SUBLANES * LANES
    nt = n // tm
    tpb = rows_per_batch // tm
    tiles = lambda a: jnp.transpose(a.reshape(TOP_K, nt, tm), (1, 0, 2)).reshape(nt, 1, TOP_K * tm)
    te3, rk3 = tiles(top_e), tiles(rank)
    cur = pl.BlockSpec((1, 1, TOP_K * tm), lambda i, es: (i, 0, 0), memory_space=pltpu.SMEM)
    nxt = pl.BlockSpec((1, 1, TOP_K * tm), lambda i, es: (jnp.minimum(i + 1, nt - 1), 0, 0),
                       memory_space=pltpu.SMEM)
    grid_spec = pltpu.PrefetchScalarGridSpec(
        num_scalar_prefetch=1,
        grid=(nt,),
        in_specs=[cur, cur, nxt, nxt,
                  pl.BlockSpec(memory_space=pl.ANY),
                  pl.BlockSpec((tm, TOP_K), lambda i, es: (i, 0)),
                  pl.BlockSpec((tm, d), lambda i, es: (i, 0)),
                  pl.BlockSpec((1, 1, d), lambda i, es: (i // tpb, 0, 0)),
                  pl.BlockSpec((1, d), lambda i, es: (0, 0))],
        out_specs=pl.BlockSpec((tm, d), lambda i, es: (i, 0)),
        scratch_shapes=[pltpu.VMEM((2, TOP_K * tm * SUBLANES, LANES), U32), pltpu.SemaphoreType.DMA((2,))],
    )
    return pl.pallas_call(
        _combine_kernel,
        grid_spec=grid_spec,
        out_shape=jax.ShapeDtypeStruct((n, d), F32),
        compiler_params=_cparams(("arbitrary",)),
        name="combine",
    )(expert_slot0, te3, rk3, te3, rk3, y_packed, w_tok, xs1, m5, g_final.reshape(1, d))


def _dispatch_plan(top_e, counts, n_tok):
    mb = MOE_BLOCK
    ne = counts.shape[0]
    n_asg = TOP_K * n_tok
    nblk = -(-(n_asg + ne * (mb - 1)) // mb)
    tok_ids = jnp.broadcast_to(jnp.arange(n_tok, dtype=I32)[None, :], top_e.shape)
    keys = jnp.sort((top_e * n_tok + tok_ids).reshape(-1))
    sorted_tok = jnp.concatenate([keys % n_tok, jnp.zeros((TOK_WINDOW_ROWS * LANES,), I32)])
    sorted_tok = sorted_tok.reshape(-1, 1, LANES)
    start = jnp.cumsum(counts) - counts
    eblk = (counts + mb - 1) // mb
    pend = jnp.cumsum(eblk)
    pstart = pend - eblk
    n_real = pend[-1]
    bid = jnp.arange(nblk, dtype=I32)
    last_real = jnp.maximum(n_real - 1, 0)
    bsrc = jnp.minimum(bid, last_real)
    block_e = jnp.minimum(jnp.searchsorted(pend, bsrc, side="right"), ne - 1).astype(I32)
    off = (bsrc - pstart[block_e]) * mb
    nvalid = jnp.where(bid < n_real, jnp.clip(counts[block_e] - off, 0, mb), 0).astype(I32)
    a0 = (start[block_e] + off).astype(I32)
    return block_e, nvalid, a0, sorted_tok, start.astype(I32)


def kernel(x, c, ctx, c_ctx, w_ada, b_ada, g_mix, w_in, conv_w, conv_b, dt_bias, a_log, d_skip, g_ssd,
           pool_w, pool_scale, w_out, g_ffn, w_router, router_bias, w_exp_gate, w_exp_up, w_exp_down,
           w_sh_gate, w_sh_up, w_sh_down, g_final):
    bsz, seq, d = x.shape
    ctx_len = ctx.shape[1]
    assert w_ada.shape[0] == 1, "single-layer block"
    d_ssm = g_ssd.shape[1]
    heads = d_skip.shape[1]
    d_pool = pool_scale.shape[1]
    d_xbc = conv_w.shape[2]
    n = bsz * seq

    cvec = jnp.zeros((8, d), F32).at[:bsz].set(c).at[bsz].set(c_ctx)
    mod_all = _ada(cvec, w_ada[0], b_ada[0]).reshape(8, N_MOD, d)
    mod = mod_all[:bsz]
    mod_c = mod_all[bsz:bsz + 1]
    mk = lambda m, k: m[:, k:k + 1, :]

    w = w_in[0]
    c_dt = d_ssm + d_xbc
    w_main = jnp.concatenate([w[:, :c_dt], w[:, c_dt + 2 * heads:]], axis=1).astype(BF16)
    w_dt = jnp.pad(w[:, c_dt:c_dt + 2 * heads], ((0, 0), (0, 128 - 2 * heads)))

    main_c, dt_c = _inproj(ctx.reshape(bsz * ctx_len, d), g_mix[0], mk(mod_c, 0), mk(mod_c, 1),
                           w_main, w_dt, bsz * ctx_len, tm=256, tn=1024)
    h_zero = jnp.zeros((bsz, SSM_GROUPS, 2, SSM_STATE, 4 * SSM_HEADDIM), F32)
    h_ctx = _ssd(main_c.reshape(bsz, ctx_len, -1), dt_c.reshape(bsz, ctx_len, 128), conv_w[0], conv_b[0],
                 dt_bias[0], a_log[0], None, h_zero, False, d_ssm, d_ssm)

    x2d = x.reshape(n, d)
    main, dt_raw = _inproj(x2d, g_mix[0], mk(mod, 0), mk(mod, 1), w_main, w_dt, seq, tm=min(1024, seq), tn=1024)
    main3 = main.reshape(bsz, seq, -1)
    dsk = jnp.repeat(d_skip[0].astype(F32), SSM_HEADDIM).reshape(1, d_ssm)
    yg = _ssd(main3, dt_raw.reshape(bsz, seq, 128), conv_w[0], conv_b[0], dt_bias[0], a_log[0], dsk,
              h_ctx, True, d_ssm, d_ssm)
    yp = _pool(main3, pool_w[0], pool_scale[0], d_ssm + d_xbc)

    wo = w_out[0].astype(BF16)
    wr = w_router[0].T
    wr_hi = wr.astype(BF16)
    wr_lo = (wr - wr_hi.astype(F32)).astype(BF16)
    xs1, h2p, lg_t = _outproj(
        yg.reshape(n, d_ssm), yp.reshape(n, d_pool), x2d, g_ssd[0],
        (mk(mod, 2), mk(mod, 3), mk(mod, 4), mk(mod, 5)), g_ffn[0],
        wo[:d_ssm], wo[d_ssm:], wr_hi, wr_lo,
        w_sh_gate[0].astype(BF16), w_sh_up[0].astype(BF16), w_sh_down[0].astype(BF16), seq, tm=256)

    top_e, top_w, rank, cnt = _topk(lg_t, router_bias[0], tm=512)
    counts = cnt[:, 0].astype(I32)
    block_e, nvalid, a0, sorted_tok, expert_slot0 = _dispatch_plan(top_e, counts, n)
    y_packed = _moe(block_e, nvalid, a0, sorted_tok, h2p, w_exp_gate[0], w_exp_up[0], w_exp_down[0],
                    TOP_K * n)
    out = _combine(expert_slot0, top_e, rank, y_packed, top_w.T, xs1, mk(mod, 5), g_final, seq, tm=128)
    return out.reshape(bsz, seq, d)
```

```python
import functools

import jax
import jax.numpy as jnp
from jax import lax
from jax.experimental import pallas as pl
from jax.experimental.pallas import tpu as pltpu

F32 = jnp.float32
BF16 = jnp.bfloat16
I32 = jnp.int32
U32 = jnp.uint32

EPS = 1e-6
GRID_W = 64
SSM_HEADDIM = 64
SSM_GROUPS = 8
SSM_STATE = 128
SSD_CHUNK = 128
POOL_WINDOWS = (2, 4, 8, 16)
TOP_K = 8
N_EXPERT_GROUPS = 8
TOPK_GROUPS = 4
ROUTE_SCALE = 2.5
N_MOD = 6
LANES = 128
SUBLANES = 8
MOE_BLOCK = 512
GATHER_GROUP = 64
MOE_SUB = 128
MOE_SUB_GROUPS = 16
ROW_PAD = SUBLANES
TOK_WINDOW = pl.next_power_of_2(LANES + MOE_BLOCK + 2 * ROW_PAD)
V7X_VMEM_LIMIT = 56 * 1024 * 1024


def _cparams(sem, vmem=V7X_VMEM_LIMIT):
    return pltpu.CompilerParams(dimension_semantics=sem, vmem_limit_bytes=vmem)


def _sigmoid(x):
    return 1.0 / (1.0 + jnp.exp(-x))


def _silu(x):
    return x * _sigmoid(x)


def _split2(x):
    hi = x.astype(BF16)
    lo = (x - hi.astype(F32)).astype(BF16)
    return hi, lo


def _dot(a, b):
    return jnp.dot(a, b, preferred_element_type=F32)


def _dot_nt(a, b):
    return lax.dot_general(a, b, (((1,), (1,)), ((), ())), preferred_element_type=F32)


def _dot_tn(a, b):
    return lax.dot_general(a, b, (((0,), (0,)), ((), ())), preferred_element_type=F32)


def _dot_w(a, w):
    return lax.dot_general(a, w, (((1,), (0,)), ((), ())), preferred_element_type=F32)


def _dot3(a, b):
    a_hi, a_lo = _split2(a)
    b_hi, b_lo = _split2(b)
    return _dot(a_hi, b_hi) + _dot(a_lo, b_hi) + _dot(a_hi, b_lo)


def _ada_kernel(c_ref, w_ref, b_ref, o_ref):
    c = c_ref[...]
    o_ref[...] = _dot3(_silu(c), w_ref[...]) + b_ref[...]


def _ada(cvec, w_ada, b_ada, tn=1024):
    d, n = w_ada.shape
    return pl.pallas_call(
        _ada_kernel,
        grid=(n // tn,),
        in_specs=[pl.BlockSpec((8, d), lambda j: (0, 0)),
                  pl.BlockSpec((d, tn), lambda j: (0, j)),
                  pl.BlockSpec((1, tn), lambda j: (0, j))],
        out_specs=pl.BlockSpec((8, tn), lambda j: (0, j)),
        out_shape=jax.ShapeDtypeStruct((8, n), F32),
        compiler_params=_cparams(("parallel",)),
        name="ada",
    )(cvec, w_ada, b_ada.reshape(1, n))


def _inproj_kernel(x_ref, g_ref, sh_ref, sc_ref, w_ref, wdt_ref, o_ref, dt_ref, h_scr):
    @pl.when(pl.program_id(1) == 0)
    def _():
        x = x_ref[...]
        ms = jnp.mean(x * x, axis=-1, keepdims=True)
        y = x * lax.rsqrt(ms + EPS) * g_ref[...]
        h = y * (1.0 + sc_ref[0]) + sh_ref[0]
        h_hi, h_lo = _split2(h)
        h_scr[...] = h_hi
        w_hi, w_lo = _split2(wdt_ref[...])
        dt_ref[...] = _dot_nt(w_hi, h_hi) + _dot_nt(w_lo, h_hi) + _dot_nt(w_hi, h_lo)

    o_ref[...] = _dot(h_scr[...], w_ref[...]).astype(o_ref.dtype)


def _inproj(x2d, g, shift, scale, w_main, w_dt_t, rows_per_batch, tm, tn):
    n, d = x2d.shape
    nc = w_main.shape[1]
    tpb = rows_per_batch // tm
    return pl.pallas_call(
        _inproj_kernel,
        grid=(n // tm, nc // tn),
        in_specs=[pl.BlockSpec((tm, d), lambda i, j: (i, 0)),
                  pl.BlockSpec((1, d), lambda i, j: (0, 0)),
                  pl.BlockSpec((1, 1, d), lambda i, j: (i // tpb, 0, 0)),
                  pl.BlockSpec((1, 1, d), lambda i, j: (i // tpb, 0, 0)),
                  pl.BlockSpec((d, tn), lambda i, j: (0, j)),
                  pl.BlockSpec((LANES, d), lambda i, j: (0, 0))],
        out_specs=[pl.BlockSpec((tm, tn), lambda i, j: (i, j)),
                   pl.BlockSpec((LANES, tm), lambda i, j: (0, i))],
        out_shape=[jax.ShapeDtypeStruct((n, nc), BF16),
                   jax.ShapeDtypeStruct((LANES, n), F32)],
        scratch_shapes=[pltpu.VMEM((tm, d), BF16)],
        compiler_params=_cparams(("parallel", "arbitrary")),
        name="inproj",
    )(x2d, g.reshape(1, d), shift, scale, w_main, w_dt_t)


def _expand_heads(v, base, width):
    t = v.shape[0]
    lane = lax.broadcasted_iota(I32, (t, 4 * width), 1)
    out = jnp.broadcast_to(v[:, base + 3:base + 4], (t, 4 * width))
    for j in (2, 1, 0):
        out = jnp.where(lane < (j + 1) * width, v[:, base + j:base + j + 1], out)
    return out


def _conv_silu(src_ref, w_ref, b_ref, dst_ref, length, tile):
    c = src_ref.shape[-1]
    w = w_ref[...]
    b = b_ref[...]
    rid = lax.broadcasted_iota(I32, (tile, c), 0)
    for r0 in range(0, length, tile):
        cur = src_ref[0, r0:r0 + tile, :].astype(F32)
        if r0 == 0:
            prev_row = jnp.zeros((1, c), F32)
        else:
            prev_row = src_ref[0, r0 - 16:r0, :].astype(F32)[15:16, :]
        if r0 + tile == length:
            next_row = jnp.zeros((1, c), F32)
        else:
            next_row = src_ref[0, r0 + tile:r0 + tile + 16, :].astype(F32)[0:1, :]
        up = jnp.where(rid == 0, prev_row, pltpu.roll(cur, 1, 0))
        dn = jnp.where(rid == tile - 1, next_row, pltpu.roll(cur, tile - 1, 0))
        o = up * w[0:1, :] + cur * w[1:2, :] + dn * w[2:3, :] + b
        dst_ref[r0:r0 + tile, :] = _silu(o).astype(dst_ref.dtype)


def _softplus(x):
    return jnp.maximum(x, 0.0) + jnp.log(1.0 + jnp.exp(-jnp.abs(x)))


def _ssd_kernel(need_y, length, *refs):
    t = SSD_CHUNK
    nch = length // t
    hw = 4 * SSM_HEADDIM
    if need_y:
        (xr, br, cr, zr, cwx, cbx, cwb, cbb, cwc, cbc, dtr, bias_r, a_r, dsk, h0,
         y_ref, xs_s, b_s, c_s, e_s, dtr_s, csr_s, csc_s, st_s, dec_s) = refs
    else:
        (xr, br, cwx, cbx, cwb, cbb, dtr, bias_r, a_r, h0,
         hfin, xs_s, b_s, dtr_s, csr_s, csc_s, st_s, dec_s) = refs

    ctile = min(256, length)
    _conv_silu(xr, cwx, cbx, xs_s, length, ctile)
    _conv_silu(br, cwb, cbb, b_s, length, ctile)
    if need_y:
        _conv_silu(cr, cwc, cbc, c_s, length, ctile)

    dt_r = _softplus(dtr[0, 0] + bias_r[0])
    dtr_s[...] = dt_r
    da2 = (dt_r * a_r[0]).reshape(nch * 8, t)
    kk = lax.broadcasted_iota(I32, (t, 2 * t), 0)
    ll = lax.broadcasted_iota(I32, (t, 2 * t), 1)
    tri = jnp.where(ll < t, jnp.where(kk <= ll, 1.0, 0.0), jnp.where(kk >= ll - t, 1.0, 0.0)).astype(BF16)
    p0 = da2.astype(BF16)
    r1 = da2 - p0.astype(F32)
    p1 = r1.astype(BF16)
    p2 = (r1 - p1.astype(F32)).astype(BF16)
    cum = _dot(p0, tri) + _dot(p1, tri) + _dot(p2, tri)
    rowj = lax.broadcasted_iota(I32, (nch * 8, t), 0) & 7
    csr = jnp.where(rowj < 4, cum[:, :t], cum[:, t:])
    csr_s[...] = csr.reshape(nch, 8, t)

    fwd_col = lax.broadcasted_iota(I32, (1, 8), 1) < 4

    def phase_a(c, carry):
        r0 = pl.multiple_of(c * t, t)
        rows16 = jnp.concatenate([csr_s[c], dtr_s[c], jnp.zeros((t - 16, t), F32)], axis=0)
        cols = rows16.T
        csc = cols[:, 0:8]
        csc_s[c] = csc
        if need_y:
            e_s[c] = jnp.exp(csc)
        edge = jnp.where(fwd_col, csc[t - 1:t, :], csc[0:1, :])
        wcol = cols[:, 8:16] * jnp.exp(edge - csc)
        dec = jnp.exp(edge)
        xs = xs_s[pl.ds(r0, t), :]
        bc = b_s[pl.ds(r0, t), :]
        xw_f = (xs * _expand_heads(wcol, 0, SSM_HEADDIM)).astype(BF16)
        xw_b = (xs * _expand_heads(wcol, 4, SSM_HEADDIM)).astype(BF16)
        st_s[c, 0] = _dot_tn(bc, xw_f)
        st_s[c, 1] = _dot_tn(bc, xw_b)
        dec_s[c, 0:1, :] = _expand_heads(dec, 0, SSM_HEADDIM)
        dec_s[c, 1:2, :] = _expand_heads(dec, 4, SSM_HEADDIM)
        return carry

    lax.fori_loop(0, nch, phase_a, 0, unroll=2)

    def rec_f(c, s):
        loc = st_s[c, 0]
        st_s[c, 0] = s
        return dec_s[c, 0:1, :] * s + loc

    def rec_b(k, s):
        c = nch - 1 - k
        loc = st_s[c, 1]
        st_s[c, 1] = s
        return dec_s[c, 1:2, :] * s + loc

    s_f = lax.fori_loop(0, nch, rec_f, h0[0, 0, 0])
    s_b = lax.fori_loop(0, nch, rec_b, h0[0, 0, 1])
    if not need_y:
        hfin[0, 0, 0] = s_f
        hfin[0, 0, 1] = s_b
        return

    li = lax.broadcasted_iota(I32, (t, t), 0)
    si = lax.broadcasted_iota(I32, (t, t), 1)
    dskip = dsk[...]

    def phase_c(c, carry):
        r0 = pl.multiple_of(c * t, t)
        csr_c = csr_s[c]
        csc_c = csc_s[c]
        dtr_c = dtr_s[c]
        e_c = e_s[c]
        xs = xs_s[pl.ds(r0, t), :]
        xs_b = xs.astype(BF16)
        bc = b_s[pl.ds(r0, t), :]
        cc = c_s[pl.ds(r0, t), :]
        cb = _dot_nt(cc, bc)
        ys = []
        for j in range(4):
            d_f = csc_c[:, j:j + 1] - csr_c[j:j + 1, :]
            att_f = jnp.where(si <= li, (cb * dtr_c[j:j + 1, :]) * jnp.exp(jnp.minimum(d_f, 0.0)), 0.0)
            d_b = csc_c[:, 4 + j:5 + j] - csr_c[4 + j:5 + j, :]
            att_b = jnp.where(si >= li, (cb * dtr_c[4 + j:5 + j, :]) * jnp.exp(jnp.minimum(d_b, 0.0)), 0.0)
            lhs = jnp.concatenate([att_f, att_b], axis=1).astype(BF16)
            xj = xs_b[:, j * SSM_HEADDIM:(j + 1) * SSM_HEADDIM]
            ys.append(_dot(lhs, jnp.concatenate([xj, xj], axis=0)))
        y = jnp.concatenate(ys, axis=1)
        y = y + _dot(cc, st_s[c, 0].astype(BF16)) * _expand_heads(e_c, 0, SSM_HEADDIM)
        y = y + _dot(cc, st_s[c, 1].astype(BF16)) * _expand_heads(e_c, 4, SSM_HEADDIM)
        y = y + xs * dskip
        z = zr[0, pl.ds(r0, t), :].astype(F32)
        y_ref[0, pl.ds(r0, t), :] = (y * _silu(z)).astype(y_ref.dtype)
        return carry

    lax.fori_loop(0, nch, phase_c, 0, unroll=2)


def _ssd(main3, dt_t, conv_w, conv_b, dt_bias, a_log, d_skip, h0, need_y, d_ssm, xbc_col0):
    nb, length, _ = main3.shape
    g_n, t = SSM_GROUPS, SSD_CHUNK
    nch = length // t
    heads = d_ssm // SSM_HEADDIM
    hpg = heads // g_n
    assert hpg == 4 and SSM_STATE == 128
    hw = hpg * SSM_HEADDIM
    dtr = jnp.transpose(dt_t[:2 * heads].reshape(g_n, 2 * hpg, nb, nch, t), (2, 0, 3, 1, 4))
    par = lambda p: jnp.transpose(p.reshape(2, g_n, hpg), (1, 0, 2)).reshape(g_n, 8)
    bias_r = par(dt_bias.astype(F32)).reshape(g_n, 8, 1)
    a_r = par(-jnp.exp(a_log.astype(F32))).reshape(g_n, 8, 1)
    nx = d_ssm
    nbc = g_n * SSM_STATE
    cwx, cwb, cwc = conv_w[:, :nx], conv_w[:, nx:nx + nbc], conv_w[:, nx + nbc:]
    cb2 = conv_b.reshape(1, -1)
    cbx, cbb, cbc = cb2[:, :nx], cb2[:, nx:nx + nbc], cb2[:, nx + nbc:]
    xcol = xbc_col0 // hw
    bcol = (xbc_col0 + nx) // SSM_STATE
    ccol = (xbc_col0 + nx + nbc) // SSM_STATE

    seq = lambda shape, imap: pl.BlockSpec(shape, imap)
    x_spec = seq((1, length, hw), lambda b, g: (b, 0, xcol + g))
    b_spec = seq((1, length, SSM_STATE), lambda b, g: (b, 0, bcol + g))
    c_spec = seq((1, length, SSM_STATE), lambda b, g: (b, 0, ccol + g))
    z_spec = seq((1, length, hw), lambda b, g: (b, 0, g))
    wx_spec = [seq((3, hw), lambda b, g: (0, g)), seq((1, hw), lambda b, g: (0, g))]
    wn_spec = [seq((3, SSM_STATE), lambda b, g: (0, g)), seq((1, SSM_STATE), lambda b, g: (0, g))]
    dt_specs = [seq((1, 1, nch, 8, t), lambda b, g: (b, g, 0, 0, 0)),
                seq((1, 8, 1), lambda b, g: (g, 0, 0)), seq((1, 8, 1), lambda b, g: (g, 0, 0))]
    h_spec = seq((1, 1, 2, SSM_STATE, hw), lambda b, g: (b, g, 0, 0, 0))
    common_scr = [pltpu.VMEM((nch, 8, t), F32),
                  pltpu.VMEM((nch, 8, t), F32),
                  pltpu.VMEM((nch, t, 8), F32),
                  pltpu.VMEM((nch, 2, SSM_STATE, hw), F32),
                  pltpu.VMEM((nch, 8, hw), F32)]
    if need_y:
        in_specs = ([x_spec, b_spec, c_spec, z_spec] + wx_spec + wn_spec + wn_spec + dt_specs
                    + [seq((1, hw), lambda b, g: (0, g)), h_spec])
        args = (main3, main3, main3, main3, cwx, cbx, cwb, cbb, cwc, cbc, dtr, bias_r, a_r, d_skip, h0)
        out_specs = seq((1, length, hw), lambda b, g: (b, 0, g))
        out_shape = jax.ShapeDtypeStruct((nb, length, d_ssm), BF16)
        scratch = [pltpu.VMEM((length, hw), F32), pltpu.VMEM((length, SSM_STATE), BF16),
                   pltpu.VMEM((length, SSM_STATE), BF16),
                   pltpu.VMEM((nch, t, 8), F32)] + common_scr
    else:
        in_specs = [x_spec, b_spec] + wx_spec + wn_spec + dt_specs + [h_spec]
        args = (main3, main3, cwx, cbx, cwb, cbb, dtr, bias_r, a_r, h0)
        out_specs = h_spec
        out_shape = jax.ShapeDtypeStruct((nb, g_n, 2, SSM_STATE, hw), F32)
        scratch = [pltpu.VMEM((length, hw), F32), pltpu.VMEM((length, SSM_STATE), BF16)] + common_scr
    return pl.pallas_call(
        functools.partial(_ssd_kernel, need_y, length),
        grid=(nb, g_n),
        in_specs=in_specs,
        out_specs=out_specs,
        out_shape=out_shape,
        scratch_shapes=scratch,
        compiler_params=_cparams(("parallel", "parallel")),
        name="ssd_y" if need_y else "ssd_state",
    )(*args)


def _pool_kernel(length, u_ref, pw_ref, ps_ref, o_ref, pad_s):
    gw = GRID_W
    rows = length // gw
    cg = pw_ref.shape[-1]
    halo = (max(POOL_WINDOWS) // 2) * gw
    tile = min(256, length)
    zeros = jnp.zeros((halo, cg), F32)
    pad_s[0:halo, :] = zeros
    pad_s[halo + length:halo + length + halo, :] = zeros
    for g, w in enumerate(POOL_WINDOWS):
        lo = -(w // 2)
        csl = slice(g * cg, (g + 1) * cg)
        for r0 in range(0, length, tile):
            pad_s[halo + r0:halo + r0 + tile, :] = u_ref[0, r0:r0 + tile, csl].astype(F32)
        pw = pw_ref[g]
        scale = ps_ref[:, csl]

        def body(i, carry, lo=lo, w=w, csl=csl, pw=pw, scale=scale):
            r0 = pl.multiple_of(i * tile, tile)
            acc = pad_s[pl.ds(halo + r0 + lo * gw, tile), :]
            for k in range(lo + 1, lo + w):
                acc = acc + pad_s[pl.ds(halo + r0 + k * gw, tile), :]
            l_idx = r0 + lax.broadcasted_iota(I32, (tile, cg), 0)
            ii = l_idx >> (gw.bit_length() - 1)
            jj = l_idx & (gw - 1)
            tot = acc
            for k in range(lo, lo + w):
                if k == 0:
                    continue
                sh = pltpu.roll(acc, (-k) % tile, 0)
                ok = (jj >= -k) if k < 0 else (jj < gw - k)
                tot = tot + jnp.where(ok, sh, 0.0)
            cnt_i = jnp.minimum(ii + lo + w, rows) - jnp.maximum(ii + lo, 0)
            cnt_j = jnp.minimum(jj + lo + w, gw) - jnp.maximum(jj + lo, 0)
            u = pad_s[pl.ds(halo + r0, tile), :]
            d = tot / (cnt_i * cnt_j).astype(F32) - u
            y = _dot(d.astype(BF16), pw) * scale
            o_ref[0, pl.ds(r0, tile), csl] = y.astype(o_ref.dtype)
            return carry

        lax.fori_loop(0, length // tile, body, 0)


def _pool(main3, pool_w, pool_scale, col0):
    nb, length, _ = main3.shape
    ng, cg, _ = pool_w.shape
    dp = ng * cg
    halo = (max(POOL_WINDOWS) // 2) * GRID_W
    return pl.pallas_call(
        functools.partial(_pool_kernel, length),
        grid=(nb,),
        in_specs=[pl.BlockSpec((1, length, dp), lambda b: (b, 0, col0 // dp)),
                  pl.BlockSpec((ng, cg, cg), lambda b: (0, 0, 0)),
                  pl.BlockSpec((1, dp), lambda b: (0, 0))],
        out_specs=pl.BlockSpec((1, length, dp), lambda b: (b, 0, 0)),
        out_shape=jax.ShapeDtypeStruct((nb, length, dp), BF16),
        scratch_shapes=[pltpu.VMEM((length + 2 * halo, cg), F32)],
        compiler_params=_cparams(("parallel",)),
        name="pool",
    )(main3, pool_w.astype(BF16), pool_scale.reshape(1, dp))


def _pack_bf16_pair(lo, hi):
    lo_b = lax.bitcast_convert_type(lo, U32) >> 16
    hi_b = lax.bitcast_convert_type(hi, U32) & jnp.uint32(0xFFFF0000)
    return lo_b | hi_b


def _unpack_bf16_pair(p):
    lo = lax.bitcast_convert_type(p << 16, F32)
    hi = lax.bitcast_convert_type(p & jnp.uint32(0xFFFF0000), F32)
    return lo, hi


def _store_row_tiles(ref, row0, value):
    m = value.shape[0]
    for c in range(SUBLANES):
        ref[pl.ds(row0 * SUBLANES + c, m, stride=SUBLANES), :] = value[:, c * LANES:(c + 1) * LANES]


def _load_row_tiles(ref, row0, m):
    return jnp.concatenate(
        [ref[pl.ds(row0 * SUBLANES + c, m, stride=SUBLANES), :] for c in range(SUBLANES)], axis=1)


def _outproj_kernel(yg_ref, yp_ref, x_ref, gs_ref, m2_ref, m3_ref, m4_ref, m5_ref, gf_ref,
                    ws_ref, wp_ref, wrh_ref, wrl_ref, wg_ref, wu_ref, wd_ref,
                    xs1_ref, h2p_ref, lg_ref):
    yg = yg_ref[...].astype(F32)
    ms = jnp.mean(yg * yg, axis=-1, keepdims=True)
    a = (yg * lax.rsqrt(ms + EPS) * gs_ref[...]).astype(BF16)
    o = _dot(a, ws_ref[...]) + _dot(yp_ref[...], wp_ref[...])
    x1 = x_ref[...] + m2_ref[0] * o
    ms1 = jnp.mean(x1 * x1, axis=-1, keepdims=True)
    h2 = x1 * lax.rsqrt(ms1 + EPS) * gf_ref[...]
    h2 = h2 * (1.0 + m4_ref[0]) + m3_ref[0]
    h_hi, h_lo = _split2(h2)
    wrh = wrh_ref[...]
    lg_ref[...] = _dot_nt(wrh, h_hi) + _dot_nt(wrl_ref[...], h_hi) + _dot_nt(wrh, h_lo)
    act = (_silu(_dot(h_hi, wg_ref[...])) * _dot(h_hi, wu_ref[...])).astype(BF16)
    xs1_ref[...] = x1 + m5_ref[0] * _dot(act, wd_ref[...])
    hf = h_hi.astype(F32)
    half = hf.shape[1] // 2
    _store_row_tiles(h2p_ref, 0, _pack_bf16_pair(hf[:, :half], hf[:, half:]))


def _outproj(yg, yp, x2d, g_ssd, mods, g_ffn, w_ssd, w_pool, wr_hi, wr_lo, wsg, wsu, wsd,
             rows_per_batch, tm):
    n, d = x2d.shape
    dp = yp.shape[1]
    ne = wr_hi.shape[0]
    dsh = wsg.shape[1]
    tpb = rows_per_batch // tm
    row = lambda c: pl.BlockSpec((tm, c), lambda i: (i, 0))
    vec = pl.BlockSpec((1, d), lambda i: (0, 0))
    mod = pl.BlockSpec((1, 1, d), lambda i: (i // tpb, 0, 0))
    res = lambda r, c: pl.BlockSpec((r, c), lambda i: (0, 0), pipeline_mode=pl.Buffered(1))
    m2, m3, m4, m5 = mods
    return pl.pallas_call(
        _outproj_kernel,
        grid=(n // tm,),
        in_specs=[row(d), row(dp), row(d), vec, mod, mod, mod, mod, vec,
                  res(d, d), res(dp, d), res(ne, d), res(ne, d), res(d, dsh), res(d, dsh), res(dsh, d)],
        out_specs=[row(d), pl.BlockSpec((tm * SUBLANES, LANES), lambda i: (i, 0)),
                   pl.BlockSpec((ne, tm), lambda i: (0, i))],
        out_shape=[jax.ShapeDtypeStruct((n, d), F32),
                   jax.ShapeDtypeStruct((n * SUBLANES, LANES), U32),
                   jax.ShapeDtypeStruct((ne, n), F32)],
        compiler_params=_cparams(("parallel",)),
        name="outproj",
    )(yg, yp, x2d, g_ssd.reshape(1, d), m2, m3, m4, m5, g_ffn.reshape(1, d),
      w_ssd, w_pool, wr_hi, wr_lo, wsg, wsu, wsd)


def _topk_kernel(lg_ref, rb_ref, te_ref, tw_ref, rk_ref, cnt_ref, carry):
    ne, tm = lg_ref.shape
    gsz = ne // N_EXPERT_GROUPS

    @pl.when(pl.program_id(0) == 0)
    def _():
        carry[...] = jnp.zeros_like(carry)

    s = _sigmoid(lg_ref[...])
    biased = s + rb_ref[...]
    neg = jnp.float32(-jnp.inf)
    big = jnp.int32(1 << 20)
    gi = lax.broadcasted_iota(I32, (gsz, tm), 0)
    gscore = []
    for g in range(N_EXPERT_GROUPS):
        v = biased[g * gsz:(g + 1) * gsz, :]
        m1 = jnp.max(v, axis=0, keepdims=True)
        i1 = jnp.min(jnp.where(v == m1, gi, big), axis=0, keepdims=True)
        m2 = jnp.max(jnp.where(gi == i1, neg, v), axis=0, keepdims=True)
        gscore.append(m1 + m2)
    parts = []
    for g in range(N_EXPERT_GROUPS):
        rank = jnp.zeros((1, tm), I32)
        for h in range(N_EXPERT_GROUPS):
            if h == g:
                continue
            ahead = (gscore[h] > gscore[g]) | ((gscore[h] == gscore[g]) & (h < g))
            rank = rank + ahead.astype(I32)
        keep = rank < TOPK_GROUPS
        parts.append(jnp.where(keep, biased[g * gsz:(g + 1) * gsz, :], neg))
    masked = jnp.concatenate(parts, axis=0)
    ei = lax.broadcasted_iota(I32, (ne, tm), 0)
    idxs, wts = [], []
    msel = jnp.zeros((ne, tm), F32)
    for _ in range(TOP_K):
        m = jnp.max(masked, axis=0, keepdims=True)
        idx = jnp.min(jnp.where(masked == m, ei, big), axis=0, keepdims=True)
        sel = ei == idx
        wts.append(jnp.sum(jnp.where(sel, s, 0.0), axis=0, keepdims=True))
        idxs.append(idx)
        masked = jnp.where(sel, neg, masked)
        msel = jnp.where(sel, 1.0, msel)
    wsum = wts[0]
    for w in wts[1:]:
        wsum = wsum + w
    ti = lax.broadcasted_iota(I32, (tm, tm), 0)
    tj = lax.broadcasted_iota(I32, (tm, tm), 1)
    before = jnp.where(ti < tj, 1.0, 0.0).astype(BF16)
    base = carry[...]
    rank_in = _dot(msel.astype(BF16), before) + jnp.concatenate([base] * (tm // 128), axis=1)
    for k in range(TOP_K):
        sel = ei == idxs[k]
        te_ref[k:k + 1, :] = idxs[k]
        tw_ref[k:k + 1, :] = wts[k] / wsum * ROUTE_SCALE
        rk_ref[k:k + 1, :] = jnp.sum(jnp.where(sel, rank_in, 0.0), axis=0, keepdims=True).astype(I32)
    total = base + _dot(msel.astype(BF16), jnp.ones((tm, 128), BF16))
    carry[...] = total
    cnt_ref[...] = total


def _topk(lg_t, router_bias, tm):
    ne, n = lg_t.shape
    row8 = lambda dt: jax.ShapeDtypeStruct((TOP_K, n), dt)
    return pl.pallas_call(
        _topk_kernel,
        grid=(n // tm,),
        in_specs=[pl.BlockSpec((ne, tm), lambda i: (0, i)),
                  pl.BlockSpec((ne, 1), lambda i: (0, 0))],
        out_specs=[pl.BlockSpec((TOP_K, tm), lambda i: (0, i))] * 3
        + [pl.BlockSpec((ne, 128), lambda i: (0, 0))],
        out_shape=[row8(I32), row8(F32), row8(I32), jax.ShapeDtypeStruct((ne, 128), F32)],
        scratch_shapes=[pltpu.VMEM((ne, 128), F32)],
        compiler_params=_cparams(("arbitrary",)),
        name="topk",
    )(lg_t, router_bias.reshape(ne, 1).astype(F32))


def _log2(n):
    assert n & (n - 1) == 0
    return n.bit_length() - 1


def _gather_groups(nvalid):
    return (nvalid + (GATHER_GROUP - 1)) >> _log2(GATHER_GROUP)


def _gather_rows(nvalid):
    return _gather_groups(nvalid) << _log2(GATHER_GROUP)


def _moe_kernel(be_ref, nv_ref, a0_ref, tok_ref, tokn_ref, h_hbm, wg_ref, wu_ref, wd_ref, y_hbm,
                rows, stage, gsem, osem):
    del be_ref
    i = pl.program_id(0)
    nsteps = pl.num_programs(0)
    slot = i % 2
    id_group = SUBLANES
    lines = GATHER_GROUP * SUBLANES

    def start_group(tref, a0, g, dst):
        lead = a0 & (id_group - 1)
        w = (a0 & (LANES - 1)) - lead + g * id_group
        for u in range(id_group):
            tok = tref[w + u]
            dst_row = ROW_PAD - lead + g * id_group + u
            pltpu.make_async_copy(
                h_hbm.at[pl.ds(pl.multiple_of(tok * SUBLANES, SUBLANES), SUBLANES), :],
                rows.at[dst, pl.ds(pl.multiple_of(dst_row * SUBLANES, SUBLANES), SUBLANES), :],
                gsem.at[dst]).start()

    def issue(tref, a0, g_lo, g_hi, dst):
        def body(g, carry):
            start_group(tref, a0, g, dst)
            return carry
        lax.fori_loop(g_lo, g_hi, body, 0)

    def gather_groups(a0, nvalid):
        lead = a0 & (id_group - 1)
        return jnp.where(nvalid > 0, (lead + _gather_rows(nvalid) + id_group - 1) >> _log2(id_group), 0)

    def pieces(nvalid):
        return (nvalid + MOE_SUB - 1) >> _log2(MOE_SUB)

    def out_copy(g, a0, src_slot):
        return pltpu.make_async_copy(
            stage.at[src_slot, pl.ds(pl.multiple_of(g * lines, lines), lines), :],
            y_hbm.at[pl.ds(pl.multiple_of((a0 + g * GATHER_GROUP) * SUBLANES, SUBLANES), lines), :],
            osem.at[src_slot])

    nv = nv_ref[i]
    a0 = a0_ref[i]
    nxt = jnp.minimum(i + 1, nsteps - 1)
    nv_n = jnp.where(i + 1 < nsteps, nv_ref[nxt], 0)
    a0_n = a0_ref[nxt]
    groups_n = gather_groups(a0_n, nv_n)
    covered_n = pieces(nv) * MOE_SUB_GROUPS

    @pl.when(i == 0)
    def _():
        rows[...] = jnp.zeros_like(rows)
        stage[1, 0:lines, :] = jnp.zeros((lines, LANES), U32)
        tail = pltpu.make_async_copy(stage.at[1, pl.ds(0, lines), :],
                                     y_hbm.at[pl.ds(y_hbm.shape[0] - lines, lines), :], osem.at[1])
        tail.start()
        tail.wait()
        issue(tok_ref, a0, 0, gather_groups(a0, nv), 0)

    issue(tokn_ref, a0_n, covered_n, groups_n, 1 - slot)

    prev = jnp.maximum(i - 1, 0)
    covered_here = jnp.where(i > 0, pieces(nv_ref[prev]) * MOE_SUB_GROUPS, 0)

    def wait_body(g, carry):
        pltpu.make_async_copy(h_hbm.at[pl.ds(0, id_group * SUBLANES), :],
                              rows.at[slot, pl.ds(0, id_group * SUBLANES), :], gsem.at[slot]).wait()
        return carry
    lax.fori_loop(0, jnp.maximum(gather_groups(a0, nv), covered_here), wait_body, 0)

    half = wg_ref.shape[1] // 2

    def sub(sb, carry):
        def start_next(k0, k1):
            for k in range(k0, k1):
                start_group(tokn_ref, a0_n, sb * MOE_SUB_GROUPS + k, 1 - slot)

        q = MOE_SUB_GROUPS // 4
        x_lo, x_hi = _unpack_bf16_pair(_load_row_tiles(rows.at[slot], ROW_PAD + sb * MOE_SUB, MOE_SUB))
        x_lo = x_lo.astype(BF16)
        x_hi = x_hi.astype(BF16)
        start_next(0, q)
        g = _dot_w(x_lo, wg_ref[0, :half, :])
        start_next(q, 2 * q)
        g = g + _dot_w(x_hi, wg_ref[0, half:, :])
        start_next(2 * q, 3 * q)
        u = _dot_w(x_lo, wu_ref[0, :half, :])
        start_next(3 * q, MOE_SUB_GROUPS)
        u = u + _dot_w(x_hi, wu_ref[0, half:, :])
        act = (_silu(g) * u).astype(BF16)
        yb = _dot_w(act, wd_ref[0]).astype(BF16).astype(F32)
        _store_row_tiles(stage.at[slot], sb * MOE_SUB, _pack_bf16_pair(yb[:, :half], yb[:, half:]))
        return carry
    lax.fori_loop(0, pieces(nv), sub, 0)

    @pl.when(i > 0)
    def _():
        prev = jnp.maximum(i - 1, 0)

        def wait_prev(g, carry):
            out_copy(0, 0, 1 - slot).wait()
            return carry
        lax.fori_loop(0, _gather_groups(nv_ref[prev]), wait_prev, 0)

    ngo = _gather_groups(nv)

    def start_out(g, carry):
        out_copy(g, a0, slot).start()
        return carry
    lax.fori_loop(0, ngo, start_out, 0)

    @pl.when(i == nsteps - 1)
    def _():
        def wait_own(g, carry):
            out_copy(0, 0, slot).wait()
            return carry
        lax.fori_loop(0, ngo, wait_own, 0)


def _moe(block_e, nvalid, a0, sorted_tok, h2p, w_gate, w_up, w_down, n_slots):
    nblk = block_e.shape[0]
    mb = MOE_BLOCK
    ne, d, f = w_gate.shape
    assert d == 2 * SUBLANES * LANES and mb % MOE_SUB == 0 and MOE_SUB % GATHER_GROUP == 0
    assert MOE_SUB_GROUPS % 4 == 0

    def win(nxt):
        def imap(i, be, nv, a0):
            j = jnp.minimum(i + 1, nblk - 1) if nxt else i
            return (pl.multiple_of((a0[j] >> _log2(LANES)) << _log2(LANES), LANES),)
        return pl.BlockSpec((pl.Element(TOK_WINDOW),), imap, memory_space=pltpu.SMEM)

    wspec = lambda r, c: pl.BlockSpec((1, r, c), lambda i, be, nv, a0: (be[i], 0, 0))
    grid_spec = pltpu.PrefetchScalarGridSpec(
        num_scalar_prefetch=3,
        grid=(nblk,),
        in_specs=[win(False), win(True), pl.BlockSpec(memory_space=pl.ANY), wspec(d, f), wspec(d, f), wspec(f, d)],
        out_specs=pl.BlockSpec(memory_space=pl.ANY),
        scratch_shapes=[pltpu.VMEM((2, (mb + 2 * ROW_PAD) * SUBLANES, LANES), U32),
                        pltpu.VMEM((2, mb * SUBLANES, LANES), U32),
                        pltpu.SemaphoreType.DMA((2,)), pltpu.SemaphoreType.DMA((2,))],
    )
    return pl.pallas_call(
        _moe_kernel,
        grid_spec=grid_spec,
        out_shape=jax.ShapeDtypeStruct(((n_slots + GATHER_GROUP) * SUBLANES, LANES), U32),
        compiler_params=_cparams(("arbitrary",)),
        name="moe",
    )(block_e, nvalid, a0, sorted_tok, sorted_tok, h2p, w_gate, w_up, w_down)


def _combine_kernel(es_ref, te_ref, rk_ref, ten_ref, rkn_ref, y_hbm, w_ref, xs1_ref, m5_ref, gf_ref, o_ref,
                    rows, sems):
    i = pl.program_id(0)
    nsteps = pl.num_programs(0)
    slot = i % 2
    tm = o_ref.shape[0]
    nrow = TOP_K * tm

    def issue(e_ref, r_ref, dst_slot):
        def body(r, carry):
            p = es_ref[e_ref[0, 0, r]] + r_ref[0, 0, r]
            pltpu.make_async_copy(
                y_hbm.at[pl.ds(pl.multiple_of(p * SUBLANES, SUBLANES), SUBLANES), :],
                rows.at[dst_slot, pl.ds(pl.multiple_of(r * SUBLANES, SUBLANES), SUBLANES), :],
                sems.at[dst_slot]).start()
            return carry
        lax.fori_loop(0, nrow, body, 0, unroll=8)

    @pl.when(i == 0)
    def _():
        issue(te_ref, rk_ref, 0)

    @pl.when(i + 1 < nsteps)
    def _():
        issue(ten_ref, rkn_ref, 1 - slot)

    pltpu.make_async_copy(y_hbm.at[pl.ds(0, nrow * SUBLANES), :], rows.at[slot], sems.at[slot]).wait()
    w = w_ref[...]
    acc_lo = acc_hi = None
    for k in range(TOP_K):
        lo, hi = _unpack_bf16_pair(_load_row_tiles(rows.at[slot], k * tm, tm))
        wk = w[:, k:k + 1]
        acc_lo = wk * lo if acc_lo is None else acc_lo + wk * lo
        acc_hi = wk * hi if acc_hi is None else acc_hi + wk * hi
    routed = jnp.concatenate([acc_lo, acc_hi], axis=1)
    x = xs1_ref[...] + m5_ref[0] * routed
    ms = jnp.mean(x * x, axis=-1, keepdims=True)
    o_ref[...] = x * lax.rsqrt(ms + EPS) * gf_ref[...]


def _combine(expert_slot0, top_e, rank, y_packed, w_tok, xs1, m5, g_final, rows_per_batch, tm):
    n, d = xs1.shape
    assert d == 2 * SUBLANES * LANES
    nt = n // tm
    tpb = rows_per_batch // tm
    tiles = lambda a: jnp.transpose(a.reshape(TOP_K, nt, tm), (1, 0, 2)).reshape(nt, 1, TOP_K * tm)
    te3, rk3 = tiles(top_e), tiles(rank)
    cur = pl.BlockSpec((1, 1, TOP_K * tm), lambda i, es: (i, 0, 0), memory_space=pltpu.SMEM)
    nxt = pl.BlockSpec((1, 1, TOP_K * tm), lambda i, es: (jnp.minimum(i + 1, nt - 1), 0, 0),
                       memory_space=pltpu.SMEM)
    grid_spec = pltpu.PrefetchScalarGridSpec(
        num_scalar_prefetch=1,
        grid=(nt,),
        in_specs=[cur, cur, nxt, nxt,
                  pl.BlockSpec(memory_space=pl.ANY),
                  pl.BlockSpec((tm, TOP_K), lambda i, es: (i, 0)),
                  pl.BlockSpec((tm, d), lambda i, es: (i, 0)),
                  pl.BlockSpec((1, 1, d), lambda i, es: (i // tpb, 0, 0)),
                  pl.BlockSpec((1, d), lambda i, es: (0, 0))],
        out_specs=pl.BlockSpec((tm, d), lambda i, es: (i, 0)),
        scratch_shapes=[pltpu.VMEM((2, TOP_K * tm * SUBLANES, LANES), U32), pltpu.SemaphoreType.DMA((2,))],
    )
    return pl.pallas_call(
        _combine_kernel,
        grid_spec=grid_spec,
        out_shape=jax.ShapeDtypeStruct((n, d), F32),
        compiler_params=_cparams(("arbitrary",)),
        name="combine",
    )(expert_slot0, te3, rk3, te3, rk3, y_packed, w_tok, xs1, m5, g_final.reshape(1, d))


def _dispatch_plan(top_e, counts, n_tok):
    mb = MOE_BLOCK
    ne = counts.shape[0]
    n_asg = TOP_K * n_tok
    nblk = -(-(n_asg + ne * (mb - 1)) // mb) + 1
    tok_ids = jnp.broadcast_to(jnp.arange(n_tok, dtype=I32)[None, :], top_e.shape)
    keys = jnp.sort((top_e * n_tok + tok_ids).reshape(-1))
    sorted_tok = jnp.concatenate([keys % n_tok, jnp.zeros((TOK_WINDOW,), I32)])
    start = jnp.cumsum(counts) - counts
    eblk = (counts + mb - 1) // mb
    pend = jnp.cumsum(eblk)
    pstart = pend - eblk
    n_real = pend[-1]
    bid = jnp.arange(nblk, dtype=I32)
    last_real = jnp.maximum(n_real - 1, 0)
    bsrc = jnp.minimum(bid, last_real)
    block_e = jnp.minimum(jnp.searchsorted(pend, bsrc, side="right"), ne - 1).astype(I32)
    off = (bsrc - pstart[block_e]) * mb
    nvalid = jnp.where(bid < n_real, jnp.clip(counts[block_e] - off, 0, mb), 0).astype(I32)
    a0 = (start[block_e] + off).astype(I32)
    return block_e, nvalid, a0, sorted_tok, start.astype(I32)


def kernel(x, c, ctx, c_ctx, w_ada, b_ada, g_mix, w_in, conv_w, conv_b, dt_bias, a_log, d_skip, g_ssd,
           pool_w, pool_scale, w_out, g_ffn, w_router, router_bias, w_exp_gate, w_exp_up, w_exp_down,
           w_sh_gate, w_sh_up, w_sh_down, g_final):
    bsz, seq, d = x.shape
    ctx_len = ctx.shape[1]
    assert w_ada.shape[0] == 1, "single-layer block"
    d_ssm = g_ssd.shape[1]
    heads = d_skip.shape[1]
    d_pool = pool_scale.shape[1]
    d_xbc = conv_w.shape[2]
    n = bsz * seq

    cvec = jnp.zeros((8, d), F32).at[:bsz].set(c).at[bsz].set(c_ctx)
    mod_all = _ada(cvec, w_ada[0], b_ada[0]).reshape(8, N_MOD, d)
    mod = mod_all[:bsz]
    mod_c = mod_all[bsz:bsz + 1]
    mk = lambda m, k: m[:, k:k + 1, :]

    w = w_in[0]
    c_dt = d_ssm + d_xbc
    w_main = jnp.concatenate([w[:, :c_dt], w[:, c_dt + 2 * heads:]], axis=1).astype(BF16)
    w_dt = w[:, c_dt:c_dt + 2 * heads].reshape(d, 2, SSM_GROUPS, heads // SSM_GROUPS)
    w_dt = jnp.transpose(w_dt, (2, 1, 3, 0)).reshape(2 * heads, d)
    w_dt = jnp.pad(w_dt, ((0, LANES - 2 * heads), (0, 0)))

    main_c, dt_c = _inproj(ctx.reshape(bsz * ctx_len, d), g_mix[0], mk(mod_c, 0), mk(mod_c, 1),
                           w_main, w_dt, bsz * ctx_len, tm=256, tn=1024)
    h_zero = jnp.zeros((bsz, SSM_GROUPS, 2, SSM_STATE, 4 * SSM_HEADDIM), F32)
    h_ctx = _ssd(main_c.reshape(bsz, ctx_len, -1), dt_c, conv_w[0], conv_b[0],
                 dt_bias[0], a_log[0], None, h_zero, False, d_ssm, d_ssm)

    x2d = x.reshape(n, d)
    main, dt_raw = _inproj(x2d, g_mix[0], mk(mod, 0), mk(mod, 1), w_main, w_dt, seq, tm=min(1024, seq), tn=1024)
    main3 = main.reshape(bsz, seq, -1)
    dsk = jnp.repeat(d_skip[0].astype(F32), SSM_HEADDIM).reshape(1, d_ssm)
    yg = _ssd(main3, dt_raw, conv_w[0], conv_b[0], dt_bias[0], a_log[0], dsk, h_ctx, True, d_ssm, d_ssm)
    yp = _pool(main3, pool_w[0], pool_scale[0], d_ssm + d_xbc)

    wo = w_out[0].astype(BF16)
    wr = w_router[0].T
    wr_hi = wr.astype(BF16)
    wr_lo = (wr - wr_hi.astype(F32)).astype(BF16)
    xs1, h2p, lg_t = _outproj(
        yg.reshape(n, d_ssm), yp.reshape(n, d_pool), x2d, g_ssd[0],
        (mk(mod, 2), mk(mod, 3), mk(mod, 4), mk(mod, 5)), g_ffn[0],
        wo[:d_ssm], wo[d_ssm:], wr_hi, wr_lo,
        w_sh_gate[0].astype(BF16), w_sh_up[0].astype(BF16), w_sh_down[0].astype(BF16), seq, tm=256)

    top_e, top_w, rank, cnt = _topk(lg_t, router_bias[0], tm=512)
    counts = cnt[:, 0].astype(I32)
    block_e, nvalid, a0, sorted_tok, expert_slot0 = _dispatch_plan(top_e, counts, n)
    y_packed = _moe(block_e, nvalid, a0, sorted_tok, h2p, w_exp_gate[0], w_exp_up[0], w_exp_down[0],
                    TOP_K * n)
    out = _combine(expert_slot0, top_e, rank, y_packed, top_w.T, xs1, mk(mod, 5), g_final, seq, tm=128)
    return out.reshape(bsz, seq, d)
```

```python
import functools

import jax
import jax.numpy as jnp
from jax import lax
from jax.experimental import pallas as pl
from jax.experimental.pallas import tpu as pltpu

F32 = jnp.float32
BF16 = jnp.bfloat16
I32 = jnp.int32
U32 = jnp.uint32

EPS = 1e-6
GRID_W = 64
SSM_HEADDIM = 64
SSM_GROUPS = 8
SSM_STATE = 128
SSD_CHUNK = 128
POOL_WINDOWS = (2, 4, 8, 16)
TOP_K = 8
N_EXPERT_GROUPS = 8
TOPK_GROUPS = 4
ROUTE_SCALE = 2.5
N_MOD = 6
LANES = 128
SUBLANES = 8
MOE_BLOCK = 512
GATHER_GROUP = 64
MOE_SUB = 128
MOE_SUB_GROUPS = 16
ROW_PAD = SUBLANES
TOK_WINDOW = pl.next_power_of_2(LANES + MOE_BLOCK + 2 * ROW_PAD)
V7X_VMEM_LIMIT = 56 * 1024 * 1024


def _cparams(sem, vmem=V7X_VMEM_LIMIT):
    return pltpu.CompilerParams(dimension_semantics=sem, vmem_limit_bytes=vmem)


def _sigmoid(x):
    return 1.0 / (1.0 + jnp.exp(-x))


def _silu(x):
    return x * _sigmoid(x)


def _split2(x):
    hi = x.astype(BF16)
    lo = (x - hi.astype(F32)).astype(BF16)
    return hi, lo


def _dot(a, b):
    return jnp.dot(a, b, preferred_element_type=F32)


def _dot_nt(a, b):
    return lax.dot_general(a, b, (((1,), (1,)), ((), ())), preferred_element_type=F32)


def _dot_tn(a, b):
    return lax.dot_general(a, b, (((0,), (0,)), ((), ())), preferred_element_type=F32)


def _dot_w(a, w):
    return lax.dot_general(a, w, (((1,), (0,)), ((), ())), preferred_element_type=F32)


def _dot3(a, b):
    a_hi, a_lo = _split2(a)
    b_hi, b_lo = _split2(b)
    return _dot(a_hi, b_hi) + _dot(a_lo, b_hi) + _dot(a_hi, b_lo)


def _ada_kernel(c_ref, w_ref, b_ref, o_ref):
    c = c_ref[...]
    o_ref[...] = _dot3(_silu(c), w_ref[...]) + b_ref[...]


def _ada(cvec, w_ada, b_ada, tn=1024):
    d, n = w_ada.shape
    return pl.pallas_call(
        _ada_kernel,
        grid=(n // tn,),
        in_specs=[pl.BlockSpec((8, d), lambda j: (0, 0)),
                  pl.BlockSpec((d, tn), lambda j: (0, j)),
                  pl.BlockSpec((1, tn), lambda j: (0, j))],
        out_specs=pl.BlockSpec((8, tn), lambda j: (0, j)),
        out_shape=jax.ShapeDtypeStruct((8, n), F32),
        compiler_params=_cparams(("parallel",)),
        name="ada",
    )(cvec, w_ada, b_ada.reshape(1, n))


def _inproj_kernel(x_ref, g_ref, sh_ref, sc_ref, w_ref, wdt_ref, o_ref, dt_ref, h_scr):
    @pl.when(pl.program_id(1) == 0)
    def _():
        x = x_ref[...]
        ms = jnp.mean(x * x, axis=-1, keepdims=True)
        y = x * lax.rsqrt(ms + EPS) * g_ref[...]
        h = y * (1.0 + sc_ref[0]) + sh_ref[0]
        h_hi, h_lo = _split2(h)
        h_scr[...] = h_hi
        w_hi, w_lo = _split2(wdt_ref[...])
        dt_ref[...] = _dot_nt(w_hi, h_hi) + _dot_nt(w_lo, h_hi) + _dot_nt(w_hi, h_lo)

    o_ref[...] = _dot_nt(h_scr[...], w_ref[...]).astype(o_ref.dtype)


def _inproj(x2d, g, shift, scale, w_main_t, w_dt_t, rows_per_batch, tm, tn):
    n, d = x2d.shape
    nc = w_main_t.shape[0]
    tpb = rows_per_batch // tm
    return pl.pallas_call(
        _inproj_kernel,
        grid=(n // tm, nc // tn),
        in_specs=[pl.BlockSpec((tm, d), lambda i, j: (i, 0)),
                  pl.BlockSpec((1, d), lambda i, j: (0, 0)),
                  pl.BlockSpec((1, 1, d), lambda i, j: (i // tpb, 0, 0)),
                  pl.BlockSpec((1, 1, d), lambda i, j: (i // tpb, 0, 0)),
                  pl.BlockSpec((tn, d), lambda i, j: (j, 0)),
                  pl.BlockSpec((LANES, d), lambda i, j: (0, 0))],
        out_specs=[pl.BlockSpec((tm, tn), lambda i, j: (i, j)),
                   pl.BlockSpec((LANES, tm), lambda i, j: (0, i))],
        out_shape=[jax.ShapeDtypeStruct((n, nc), BF16),
                   jax.ShapeDtypeStruct((LANES, n), F32)],
        scratch_shapes=[pltpu.VMEM((tm, d), BF16)],
        compiler_params=_cparams(("parallel", "arbitrary")),
        name="inproj",
    )(x2d, g.reshape(1, d), shift, scale, w_main_t, w_dt_t)


def _expand_heads(v, base, width):
    t = v.shape[0]
    lane = lax.broadcasted_iota(I32, (t, 4 * width), 1)
    out = jnp.broadcast_to(v[:, base + 3:base + 4], (t, 4 * width))
    for j in (2, 1, 0):
        out = jnp.where(lane < (j + 1) * width, v[:, base + j:base + j + 1], out)
    return out


def _conv_silu(src_ref, w_ref, b_ref, dst_ref, length, tile):
    c = src_ref.shape[-1]
    w = w_ref[...]
    b = b_ref[...]
    rid = lax.broadcasted_iota(I32, (tile, c), 0)
    for r0 in range(0, length, tile):
        cur = src_ref[0, r0:r0 + tile, :].astype(F32)
        if r0 == 0:
            prev_row = jnp.zeros((1, c), F32)
        else:
            prev_row = src_ref[0, r0 - 16:r0, :].astype(F32)[15:16, :]
        if r0 + tile == length:
            next_row = jnp.zeros((1, c), F32)
        else:
            next_row = src_ref[0, r0 + tile:r0 + tile + 16, :].astype(F32)[0:1, :]
        up = jnp.where(rid == 0, prev_row, pltpu.roll(cur, 1, 0))
        dn = jnp.where(rid == tile - 1, next_row, pltpu.roll(cur, tile - 1, 0))
        o = up * w[0:1, :] + cur * w[1:2, :] + dn * w[2:3, :] + b
        dst_ref[r0:r0 + tile, :] = _silu(o).astype(dst_ref.dtype)


def _softplus(x):
    return jnp.maximum(x, 0.0) + jnp.log(1.0 + jnp.exp(-jnp.abs(x)))


def _ssd_kernel(need_y, length, *refs):
    t = SSD_CHUNK
    nch = length // t
    hw = 4 * SSM_HEADDIM
    if need_y:
        (xr, br, cr, zr, cwx, cbx, cwb, cbb, cwc, cbc, dtr, bias_r, a_r, dsk, h0,
         y_ref, xs_s, b_s, c_s, e_s, dtr_s, csr_s, csc_s, st_s, dec_s) = refs
    else:
        (xr, br, cwx, cbx, cwb, cbb, dtr, bias_r, a_r, h0,
         hfin, xs_s, b_s, dtr_s, csr_s, csc_s, st_s, dec_s) = refs

    ctile = min(256, length)
    _conv_silu(xr, cwx, cbx, xs_s, length, ctile)
    _conv_silu(br, cwb, cbb, b_s, length, ctile)
    if need_y:
        _conv_silu(cr, cwc, cbc, c_s, length, ctile)

    dt_r = _softplus(dtr[0, 0] + bias_r[0])
    dtr_s[...] = dt_r
    da2 = (dt_r * a_r[0]).reshape(nch * 8, t)
    kk = lax.broadcasted_iota(I32, (t, 2 * t), 0)
    ll = lax.broadcasted_iota(I32, (t, 2 * t), 1)
    tri = jnp.where(ll < t, jnp.where(kk <= ll, 1.0, 0.0), jnp.where(kk >= ll - t, 1.0, 0.0)).astype(BF16)
    p0 = da2.astype(BF16)
    r1 = da2 - p0.astype(F32)
    p1 = r1.astype(BF16)
    p2 = (r1 - p1.astype(F32)).astype(BF16)
    cum = _dot(p0, tri) + _dot(p1, tri) + _dot(p2, tri)
    rowj = lax.broadcasted_iota(I32, (nch * 8, t), 0) & 7
    csr = jnp.where(rowj < 4, cum[:, :t], cum[:, t:])
    csr_s[...] = csr.reshape(nch, 8, t)

    fwd_col = lax.broadcasted_iota(I32, (1, 8), 1) < 4

    def phase_a(c, carry):
        r0 = pl.multiple_of(c * t, t)
        rows16 = jnp.concatenate([csr_s[c], dtr_s[c], jnp.zeros((t - 16, t), F32)], axis=0)
        cols = rows16.T
        csc = cols[:, 0:8]
        csc_s[c] = csc
        if need_y:
            e_s[c] = jnp.exp(csc)
        edge = jnp.where(fwd_col, csc[t - 1:t, :], csc[0:1, :])
        wcol = cols[:, 8:16] * jnp.exp(edge - csc)
        dec = jnp.exp(edge)
        xs = xs_s[pl.ds(r0, t), :]
        bc = b_s[pl.ds(r0, t), :]
        xw_f = (xs * _expand_heads(wcol, 0, SSM_HEADDIM)).astype(BF16)
        xw_b = (xs * _expand_heads(wcol, 4, SSM_HEADDIM)).astype(BF16)
        st_s[c, 0] = _dot_tn(bc, xw_f)
        st_s[c, 1] = _dot_tn(bc, xw_b)
        dec_s[c, 0:1, :] = _expand_heads(dec, 0, SSM_HEADDIM)
        dec_s[c, 1:2, :] = _expand_heads(dec, 4, SSM_HEADDIM)
        return carry

    lax.fori_loop(0, nch, phase_a, 0, unroll=2)

    def rec_f(c, s):
        loc = st_s[c, 0]
        st_s[c, 0] = s
        return dec_s[c, 0:1, :] * s + loc

    def rec_b(k, s):
        c = nch - 1 - k
        loc = st_s[c, 1]
        st_s[c, 1] = s
        return dec_s[c, 1:2, :] * s + loc

    s_f = lax.fori_loop(0, nch, rec_f, h0[0, 0, 0])
    s_b = lax.fori_loop(0, nch, rec_b, h0[0, 0, 1])
    if not need_y:
        hfin[0, 0, 0] = s_f
        hfin[0, 0, 1] = s_b
        return

    li = lax.broadcasted_iota(I32, (t, t), 0)
    si = lax.broadcasted_iota(I32, (t, t), 1)
    dskip = dsk[...]

    def phase_c(c, carry):
        r0 = pl.multiple_of(c * t, t)
        csr_c = csr_s[c]
        csc_c = csc_s[c]
        dtr_c = dtr_s[c]
        e_c = e_s[c]
        xs = xs_s[pl.ds(r0, t), :]
        xs_b = xs.astype(BF16)
        bc = b_s[pl.ds(r0, t), :]
        cc = c_s[pl.ds(r0, t), :]
        cb = _dot_nt(cc, bc)
        ys = []
        for j in range(4):
            d_f = csc_c[:, j:j + 1] - csr_c[j:j + 1, :]
            att_f = jnp.where(si <= li, (cb * dtr_c[j:j + 1, :]) * jnp.exp(jnp.minimum(d_f, 0.0)), 0.0)
            d_b = csc_c[:, 4 + j:5 + j] - csr_c[4 + j:5 + j, :]
            att_b = jnp.where(si >= li, (cb * dtr_c[4 + j:5 + j, :]) * jnp.exp(jnp.minimum(d_b, 0.0)), 0.0)
            lhs = jnp.concatenate([att_f, att_b], axis=1).astype(BF16)
            xj = xs_b[:, j * SSM_HEADDIM:(j + 1) * SSM_HEADDIM]
            ys.append(_dot(lhs, jnp.concatenate([xj, xj], axis=0)))
        y = jnp.concatenate(ys, axis=1)
        y = y + _dot(cc, st_s[c, 0].astype(BF16)) * _expand_heads(e_c, 0, SSM_HEADDIM)
        y = y + _dot(cc, st_s[c, 1].astype(BF16)) * _expand_heads(e_c, 4, SSM_HEADDIM)
        y = y + xs * dskip
        z = zr[0, pl.ds(r0, t), :].astype(F32)
        y_ref[0, pl.ds(r0, t), :] = (y * _silu(z)).astype(y_ref.dtype)
        return carry

    lax.fori_loop(0, nch, phase_c, 0, unroll=2)


def _ssd(main3, dt_t, conv_w, conv_b, dt_bias, a_log, d_skip, h0, need_y, d_ssm, xbc_col0):
    nb, length, _ = main3.shape
    g_n, t = SSM_GROUPS, SSD_CHUNK
    nch = length // t
    heads = d_ssm // SSM_HEADDIM
    hpg = heads // g_n
    assert hpg == 4 and SSM_STATE == 128
    hw = hpg * SSM_HEADDIM
    dtr = jnp.transpose(dt_t[:2 * heads].reshape(g_n, 2 * hpg, nb, nch, t), (2, 0, 3, 1, 4))
    par = lambda p: jnp.transpose(p.reshape(2, g_n, hpg), (1, 0, 2)).reshape(g_n, 8)
    bias_r = par(dt_bias.astype(F32)).reshape(g_n, 8, 1)
    a_r = par(-jnp.exp(a_log.astype(F32))).reshape(g_n, 8, 1)
    nx = d_ssm
    nbc = g_n * SSM_STATE
    cwx, cwb, cwc = conv_w[:, :nx], conv_w[:, nx:nx + nbc], conv_w[:, nx + nbc:]
    cb2 = conv_b.reshape(1, -1)
    cbx, cbb, cbc = cb2[:, :nx], cb2[:, nx:nx + nbc], cb2[:, nx + nbc:]
    xcol = xbc_col0 // hw
    bcol = (xbc_col0 + nx) // SSM_STATE
    ccol = (xbc_col0 + nx + nbc) // SSM_STATE

    seq = lambda shape, imap: pl.BlockSpec(shape, imap)
    x_spec = seq((1, length, hw), lambda b, g: (b, 0, xcol + g))
    b_spec = seq((1, length, SSM_STATE), lambda b, g: (b, 0, bcol + g))
    c_spec = seq((1, length, SSM_STATE), lambda b, g: (b, 0, ccol + g))
    z_spec = seq((1, length, hw), lambda b, g: (b, 0, g))
    wx_spec = [seq((3, hw), lambda b, g: (0, g)), seq((1, hw), lambda b, g: (0, g))]
    wn_spec = [seq((3, SSM_STATE), lambda b, g: (0, g)), seq((1, SSM_STATE), lambda b, g: (0, g))]
    dt_specs = [seq((1, 1, nch, 8, t), lambda b, g: (b, g, 0, 0, 0)),
                seq((1, 8, 1), lambda b, g: (g, 0, 0)), seq((1, 8, 1), lambda b, g: (g, 0, 0))]
    h_spec = seq((1, 1, 2, SSM_STATE, hw), lambda b, g: (b, g, 0, 0, 0))
    common_scr = [pltpu.VMEM((nch, 8, t), F32),
                  pltpu.VMEM((nch, 8, t), F32),
                  pltpu.VMEM((nch, t, 8), F32),
                  pltpu.VMEM((nch, 2, SSM_STATE, hw), F32),
                  pltpu.VMEM((nch, 8, hw), F32)]
    if need_y:
        in_specs = ([x_spec, b_spec, c_spec, z_spec] + wx_spec + wn_spec + wn_spec + dt_specs
                    + [seq((1, hw), lambda b, g: (0, g)), h_spec])
        args = (main3, main3, main3, main3, cwx, cbx, cwb, cbb, cwc, cbc, dtr, bias_r, a_r, d_skip, h0)
        out_specs = seq((1, length, hw), lambda b, g: (b, 0, g))
        out_shape = jax.ShapeDtypeStruct((nb, length, d_ssm), BF16)
        scratch = [pltpu.VMEM((length, hw), F32), pltpu.VMEM((length, SSM_STATE), BF16),
                   pltpu.VMEM((length, SSM_STATE), BF16),
                   pltpu.VMEM((nch, t, 8), F32)] + common_scr
    else:
        in_specs = [x_spec, b_spec] + wx_spec + wn_spec + dt_specs + [h_spec]
        args = (main3, main3, cwx, cbx, cwb, cbb, dtr, bias_r, a_r, h0)
        out_specs = h_spec
        out_shape = jax.ShapeDtypeStruct((nb, g_n, 2, SSM_STATE, hw), F32)
        scratch = [pltpu.VMEM((length, hw), F32), pltpu.VMEM((length, SSM_STATE), BF16)] + common_scr
    return pl.pallas_call(
        functools.partial(_ssd_kernel, need_y, length),
        grid=(nb, g_n),
        in_specs=in_specs,
        out_specs=out_specs,
        out_shape=out_shape,
        scratch_shapes=scratch,
        compiler_params=_cparams(("parallel", "parallel")),
        name="ssd_y" if need_y else "ssd_state",
    )(*args)


def _pool_kernel(length, u_ref, pw_ref, ps_ref, o_ref, pad_s):
    gw = GRID_W
    rows = length // gw
    cg = pw_ref.shape[-1]
    halo = (max(POOL_WINDOWS) // 2) * gw
    tile = min(256, length)
    zeros = jnp.zeros((halo, cg), F32)
    pad_s[0:halo, :] = zeros
    pad_s[halo + length:halo + length + halo, :] = zeros
    for g, w in enumerate(POOL_WINDOWS):
        lo = -(w // 2)
        csl = slice(g * cg, (g + 1) * cg)
        for r0 in range(0, length, tile):
            pad_s[halo + r0:halo + r0 + tile, :] = u_ref[0, r0:r0 + tile, csl].astype(F32)
        pw = pw_ref[g]
        scale = ps_ref[:, csl]

        def body(i, carry, lo=lo, w=w, csl=csl, pw=pw, scale=scale):
            r0 = pl.multiple_of(i * tile, tile)
            acc = pad_s[pl.ds(halo + r0 + lo * gw, tile), :]
            for k in range(lo + 1, lo + w):
                acc = acc + pad_s[pl.ds(halo + r0 + k * gw, tile), :]
            l_idx = r0 + lax.broadcasted_iota(I32, (tile, cg), 0)
            ii = l_idx >> (gw.bit_length() - 1)
            jj = l_idx & (gw - 1)
            tot = acc
            for k in range(lo, lo + w):
                if k == 0:
                    continue
                sh = pltpu.roll(acc, (-k) % tile, 0)
                ok = (jj >= -k) if k < 0 else (jj < gw - k)
                tot = tot + jnp.where(ok, sh, 0.0)
            cnt_i = jnp.minimum(ii + lo + w, rows) - jnp.maximum(ii + lo, 0)
            cnt_j = jnp.minimum(jj + lo + w, gw) - jnp.maximum(jj + lo, 0)
            u = pad_s[pl.ds(halo + r0, tile), :]
            d = tot / (cnt_i * cnt_j).astype(F32) - u
            y = _dot(d.astype(BF16), pw) * scale
            o_ref[0, pl.ds(r0, tile), csl] = y.astype(o_ref.dtype)
            return carry

        lax.fori_loop(0, length // tile, body, 0)


def _pool(main3, pool_w, pool_scale, col0):
    nb, length, _ = main3.shape
    ng, cg, _ = pool_w.shape
    dp = ng * cg
    halo = (max(POOL_WINDOWS) // 2) * GRID_W
    return pl.pallas_call(
        functools.partial(_pool_kernel, length),
        grid=(nb,),
        in_specs=[pl.BlockSpec((1, length, dp), lambda b: (b, 0, col0 // dp)),
                  pl.BlockSpec((ng, cg, cg), lambda b: (0, 0, 0)),
                  pl.BlockSpec((1, dp), lambda b: (0, 0))],
        out_specs=pl.BlockSpec((1, length, dp), lambda b: (b, 0, 0)),
        out_shape=jax.ShapeDtypeStruct((nb, length, dp), BF16),
        scratch_shapes=[pltpu.VMEM((length + 2 * halo, cg), F32)],
        compiler_params=_cparams(("parallel",)),
        name="pool",
    )(main3, pool_w.astype(BF16), pool_scale.reshape(1, dp))


def _pack_bf16_pair(lo, hi):
    lo_b = lax.bitcast_convert_type(lo, U32) >> 16
    hi_b = lax.bitcast_convert_type(hi, U32) & jnp.uint32(0xFFFF0000)
    return lo_b | hi_b


def _unpack_bf16_pair(p):
    lo = lax.bitcast_convert_type(p << 16, F32)
    hi = lax.bitcast_convert_type(p & jnp.uint32(0xFFFF0000), F32)
    return lo, hi


def _store_row_tiles(ref, row0, value):
    m = value.shape[0]
    for c in range(SUBLANES):
        ref[pl.ds(row0 * SUBLANES + c, m, stride=SUBLANES), :] = value[:, c * LANES:(c + 1) * LANES]


def _load_row_tiles(ref, row0, m):
    return jnp.concatenate(
        [ref[pl.ds(row0 * SUBLANES + c, m, stride=SUBLANES), :] for c in range(SUBLANES)], axis=1)


def _outproj_kernel(yg_ref, yp_ref, x_ref, gs_ref, m2_ref, m3_ref, m4_ref, m5_ref, gf_ref,
                    ws_ref, wp_ref, wrh_ref, wrl_ref, wg_ref, wu_ref, wd_ref,
                    xs1_ref, h2p_ref, lg_ref):
    yg = yg_ref[...].astype(F32)
    ms = jnp.mean(yg * yg, axis=-1, keepdims=True)
    a = (yg * lax.rsqrt(ms + EPS) * gs_ref[...]).astype(BF16)
    o = _dot(a, ws_ref[...]) + _dot(yp_ref[...], wp_ref[...])
    x1 = x_ref[...] + m2_ref[0] * o
    ms1 = jnp.mean(x1 * x1, axis=-1, keepdims=True)
    h2 = x1 * lax.rsqrt(ms1 + EPS) * gf_ref[...]
    h2 = h2 * (1.0 + m4_ref[0]) + m3_ref[0]
    h_hi, h_lo = _split2(h2)
    wrh = wrh_ref[...]
    lg_ref[...] = _dot_nt(wrh, h_hi) + _dot_nt(wrl_ref[...], h_hi) + _dot_nt(wrh, h_lo)
    act = (_silu(_dot(h_hi, wg_ref[...])) * _dot(h_hi, wu_ref[...])).astype(BF16)
    xs1_ref[...] = x1 + m5_ref[0] * _dot(act, wd_ref[...])
    hf = h_hi.astype(F32)
    half = hf.shape[1] // 2
    _store_row_tiles(h2p_ref, 0, _pack_bf16_pair(hf[:, :half], hf[:, half:]))


def _outproj(yg, yp, x2d, g_ssd, mods, g_ffn, w_ssd, w_pool, wr_hi, wr_lo, wsg, wsu, wsd,
             rows_per_batch, tm):
    n, d = x2d.shape
    dp = yp.shape[1]
    ne = wr_hi.shape[0]
    dsh = wsg.shape[1]
    tpb = rows_per_batch // tm
    row = lambda c: pl.BlockSpec((tm, c), lambda i: (i, 0))
    vec = pl.BlockSpec((1, d), lambda i: (0, 0))
    mod = pl.BlockSpec((1, 1, d), lambda i: (i // tpb, 0, 0))
    res = lambda r, c: pl.BlockSpec((r, c), lambda i: (0, 0), pipeline_mode=pl.Buffered(1))
    m2, m3, m4, m5 = mods
    return pl.pallas_call(
        _outproj_kernel,
        grid=(n // tm,),
        in_specs=[row(d), row(dp), row(d), vec, mod, mod, mod, mod, vec,
                  res(d, d), res(dp, d), res(ne, d), res(ne, d), res(d, dsh), res(d, dsh), res(dsh, d)],
        out_specs=[row(d), pl.BlockSpec((tm * SUBLANES, LANES), lambda i: (i, 0)),
                   pl.BlockSpec((ne, tm), lambda i: (0, i))],
        out_shape=[jax.ShapeDtypeStruct((n, d), F32),
                   jax.ShapeDtypeStruct((n * SUBLANES, LANES), U32),
                   jax.ShapeDtypeStruct((ne, n), F32)],
        compiler_params=_cparams(("parallel",)),
        name="outproj",
    )(yg, yp, x2d, g_ssd.reshape(1, d), m2, m3, m4, m5, g_ffn.reshape(1, d),
      w_ssd, w_pool, wr_hi, wr_lo, wsg, wsu, wsd)


def _topk_kernel(lg_ref, rb_ref, te_ref, tw_ref, rk_ref, cnt_ref, carry):
    ne, tm = lg_ref.shape
    gsz = ne // N_EXPERT_GROUPS

    @pl.when(pl.program_id(0) == 0)
    def _():
        carry[...] = jnp.zeros_like(carry)

    s = _sigmoid(lg_ref[...])
    biased = s + rb_ref[...]
    neg = jnp.float32(-jnp.inf)
    big = jnp.int32(1 << 20)
    gi = lax.broadcasted_iota(I32, (gsz, tm), 0)
    gscore = []
    for g in range(N_EXPERT_GROUPS):
        v = biased[g * gsz:(g + 1) * gsz, :]
        m1 = jnp.max(v, axis=0, keepdims=True)
        i1 = jnp.min(jnp.where(v == m1, gi, big), axis=0, keepdims=True)
        m2 = jnp.max(jnp.where(gi == i1, neg, v), axis=0, keepdims=True)
        gscore.append(m1 + m2)
    parts = []
    for g in range(N_EXPERT_GROUPS):
        rank = jnp.zeros((1, tm), I32)
        for h in range(N_EXPERT_GROUPS):
            if h == g:
                continue
            ahead = (gscore[h] > gscore[g]) | ((gscore[h] == gscore[g]) & (h < g))
            rank = rank + ahead.astype(I32)
        keep = rank < TOPK_GROUPS
        parts.append(jnp.where(keep, biased[g * gsz:(g + 1) * gsz, :], neg))
    masked = jnp.concatenate(parts, axis=0)
    ei = lax.broadcasted_iota(I32, (ne, tm), 0)
    idxs, wts = [], []
    msel = jnp.zeros((ne, tm), F32)
    for _ in range(TOP_K):
        m = jnp.max(masked, axis=0, keepdims=True)
        idx = jnp.min(jnp.where(masked == m, ei, big), axis=0, keepdims=True)
        sel = ei == idx
        wts.append(jnp.sum(jnp.where(sel, s, 0.0), axis=0, keepdims=True))
        idxs.append(idx)
        masked = jnp.where(sel, neg, masked)
        msel = jnp.where(sel, 1.0, msel)
    wsum = wts[0]
    for w in wts[1:]:
        wsum = wsum + w
    ti = lax.broadcasted_iota(I32, (tm, tm), 0)
    tj = lax.broadcasted_iota(I32, (tm, tm), 1)
    before = jnp.where(ti < tj, 1.0, 0.0).astype(BF16)
    base = carry[...]
    rank_in = _dot(msel.astype(BF16), before) + jnp.concatenate([base] * (tm // 128), axis=1)
    for k in range(TOP_K):
        sel = ei == idxs[k]
        te_ref[k:k + 1, :] = idxs[k]
        tw_ref[k:k + 1, :] = wts[k] / wsum * ROUTE_SCALE
        rk_ref[k:k + 1, :] = jnp.sum(jnp.where(sel, rank_in, 0.0), axis=0, keepdims=True).astype(I32)
    total = base + _dot(msel.astype(BF16), jnp.ones((tm, 128), BF16))
    carry[...] = total
    cnt_ref[...] = total


def _topk(lg_t, router_bias, tm):
    ne, n = lg_t.shape
    row8 = lambda dt: jax.ShapeDtypeStruct((TOP_K, n), dt)
    return pl.pallas_call(
        _topk_kernel,
        grid=(n // tm,),
        in_specs=[pl.BlockSpec((ne, tm), lambda i: (0, i)),
                  pl.BlockSpec((ne, 1), lambda i: (0, 0))],
        out_specs=[pl.BlockSpec((TOP_K, tm), lambda i: (0, i))] * 3
        + [pl.BlockSpec((ne, 128), lambda i: (0, 0))],
        out_shape=[row8(I32), row8(F32), row8(I32), jax.ShapeDtypeStruct((ne, 128), F32)],
        scratch_shapes=[pltpu.VMEM((ne, 128), F32)],
        compiler_params=_cparams(("arbitrary",)),
        name="topk",
    )(lg_t, router_bias.reshape(ne, 1).astype(F32))


def _log2(n):
    assert n & (n - 1) == 0
    return n.bit_length() - 1


def _gather_groups(nvalid):
    return (nvalid + (GATHER_GROUP - 1)) >> _log2(GATHER_GROUP)


def _gather_rows(nvalid):
    return _gather_groups(nvalid) << _log2(GATHER_GROUP)


def _moe_kernel(be_ref, nv_ref, a0_ref, tok_ref, tokn_ref, h_hbm, wg_ref, wu_ref, wd_ref, y_hbm,
                rows, stage, gsem, osem):
    del be_ref
    i = pl.program_id(0)
    nsteps = pl.num_programs(0)
    slot = i % 2
    id_group = SUBLANES
    lines = GATHER_GROUP * SUBLANES

    def start_group(tref, a0, g, dst):
        lead = a0 & (id_group - 1)
        w = (a0 & (LANES - 1)) - lead + g * id_group
        for u in range(id_group):
            tok = tref[w + u]
            dst_row = ROW_PAD - lead + g * id_group + u
            pltpu.make_async_copy(
                h_hbm.at[pl.ds(pl.multiple_of(tok * SUBLANES, SUBLANES), SUBLANES), :],
                rows.at[dst, pl.ds(pl.multiple_of(dst_row * SUBLANES, SUBLANES), SUBLANES), :],
                gsem.at[dst]).start()

    def issue(tref, a0, g_lo, g_hi, dst):
        def body(g, carry):
            start_group(tref, a0, g, dst)
            return carry
        lax.fori_loop(g_lo, g_hi, body, 0)

    def gather_groups(a0, nvalid):
        lead = a0 & (id_group - 1)
        return jnp.where(nvalid > 0, (lead + _gather_rows(nvalid) + id_group - 1) >> _log2(id_group), 0)

    def pieces(nvalid):
        rem = nvalid & (2 * MOE_SUB - 1)
        big = (nvalid >> _log2(2 * MOE_SUB)) + jnp.where(rem > MOE_SUB, 1, 0)
        small = jnp.where((rem > 0) & (rem <= MOE_SUB), 1, 0)
        return big, small

    def covered(nvalid):
        big, small = pieces(nvalid)
        return (2 * big + small) * MOE_SUB_GROUPS

    def out_copy(g, a0, src_slot):
        return pltpu.make_async_copy(
            stage.at[src_slot, pl.ds(pl.multiple_of(g * lines, lines), lines), :],
            y_hbm.at[pl.ds(pl.multiple_of((a0 + g * GATHER_GROUP) * SUBLANES, SUBLANES), lines), :],
            osem.at[src_slot])

    nv = nv_ref[i]
    a0 = a0_ref[i]
    nxt = jnp.minimum(i + 1, nsteps - 1)
    nv_n = jnp.where(i + 1 < nsteps, nv_ref[nxt], 0)
    a0_n = a0_ref[nxt]
    groups_n = gather_groups(a0_n, nv_n)
    covered_n = covered(nv)

    @pl.when(i == 0)
    def _():
        rows[...] = jnp.zeros_like(rows)
        stage[1, 0:lines, :] = jnp.zeros((lines, LANES), U32)
        tail = pltpu.make_async_copy(stage.at[1, pl.ds(0, lines), :],
                                     y_hbm.at[pl.ds(y_hbm.shape[0] - lines, lines), :], osem.at[1])
        tail.start()
        tail.wait()
        issue(tok_ref, a0, 0, gather_groups(a0, nv), 0)

    issue(tokn_ref, a0_n, covered_n, groups_n, 1 - slot)

    prev = jnp.maximum(i - 1, 0)
    covered_here = jnp.where(i > 0, covered(nv_ref[prev]), 0)

    def wait_body(g, carry):
        pltpu.make_async_copy(h_hbm.at[pl.ds(0, id_group * SUBLANES), :],
                              rows.at[slot, pl.ds(0, id_group * SUBLANES), :], gsem.at[slot]).wait()
        return carry
    lax.fori_loop(0, jnp.maximum(gather_groups(a0, nv), covered_here), wait_body, 0)

    half = wg_ref.shape[1] // 2

    def piece(row0, m, g0):
        ng = (m // MOE_SUB) * MOE_SUB_GROUPS

        def start_next(k0, k1):
            for k in range(k0, k1):
                start_group(tokn_ref, a0_n, g0 + k, 1 - slot)

        q = ng // 4
        x_lo, x_hi = _unpack_bf16_pair(_load_row_tiles(rows.at[slot], ROW_PAD + row0, m))
        x_lo = x_lo.astype(BF16)
        x_hi = x_hi.astype(BF16)
        start_next(0, q)
        g = _dot_w(x_lo, wg_ref[0, :half, :])
        start_next(q, 2 * q)
        g = g + _dot_w(x_hi, wg_ref[0, half:, :])
        start_next(2 * q, 3 * q)
        u = _dot_w(x_lo, wu_ref[0, :half, :])
        start_next(3 * q, ng)
        u = u + _dot_w(x_hi, wu_ref[0, half:, :])
        act = (_silu(g) * u).astype(BF16)
        yb = _dot_w(act, wd_ref[0]).astype(BF16).astype(F32)
        _store_row_tiles(stage.at[slot], row0, _pack_bf16_pair(yb[:, :half], yb[:, half:]))

    n_big, n_small = pieces(nv)

    def big_piece(p, carry):
        piece(p * (2 * MOE_SUB), 2 * MOE_SUB, p * (2 * MOE_SUB_GROUPS))
        return carry
    lax.fori_loop(0, n_big, big_piece, 0)

    @pl.when(n_small > 0)
    def _():
        piece(n_big * (2 * MOE_SUB), MOE_SUB, n_big * (2 * MOE_SUB_GROUPS))

    @pl.when(i > 0)
    def _():
        prev = jnp.maximum(i - 1, 0)

        def wait_prev(g, carry):
            out_copy(0, 0, 1 - slot).wait()
            return carry
        lax.fori_loop(0, _gather_groups(nv_ref[prev]), wait_prev, 0)

    ngo = _gather_groups(nv)

    def start_out(g, carry):
        out_copy(g, a0, slot).start()
        return carry
    lax.fori_loop(0, ngo, start_out, 0)

    @pl.when(i == nsteps - 1)
    def _():
        def wait_own(g, carry):
            out_copy(0, 0, slot).wait()
            return carry
        lax.fori_loop(0, ngo, wait_own, 0)


def _moe(block_e, nvalid, a0, sorted_tok, h2p, w_gate, w_up, w_down, n_slots):
    nblk = block_e.shape[0]
    mb = MOE_BLOCK
    ne, d, f = w_gate.shape
    assert d == 2 * SUBLANES * LANES and mb % MOE_SUB == 0 and MOE_SUB % GATHER_GROUP == 0
    assert MOE_SUB_GROUPS % 4 == 0

    def win(nxt):
        def imap(i, be, nv, a0):
            j = jnp.minimum(i + 1, nblk - 1) if nxt else i
            return (pl.multiple_of((a0[j] >> _log2(LANES)) << _log2(LANES), LANES),)
        return pl.BlockSpec((pl.Element(TOK_WINDOW),), imap, memory_space=pltpu.SMEM)

    wspec = lambda r, c: pl.BlockSpec((1, r, c), lambda i, be, nv, a0: (be[i], 0, 0))
    grid_spec = pltpu.PrefetchScalarGridSpec(
        num_scalar_prefetch=3,
        grid=(nblk,),
        in_specs=[win(False), win(True), pl.BlockSpec(memory_space=pl.ANY), wspec(d, f), wspec(d, f), wspec(f, d)],
        out_specs=pl.BlockSpec(memory_space=pl.ANY),
        scratch_shapes=[pltpu.VMEM((2, (mb + 2 * ROW_PAD) * SUBLANES, LANES), U32),
                        pltpu.VMEM((2, mb * SUBLANES, LANES), U32),
                        pltpu.SemaphoreType.DMA((2,)), pltpu.SemaphoreType.DMA((2,))],
    )
    return pl.pallas_call(
        _moe_kernel,
        grid_spec=grid_spec,
        out_shape=jax.ShapeDtypeStruct(((n_slots + GATHER_GROUP) * SUBLANES, LANES), U32),
        compiler_params=_cparams(("arbitrary",)),
        name="moe",
    )(block_e, nvalid, a0, sorted_tok, sorted_tok, h2p, w_gate, w_up, w_down)


def _combine_kernel(es_ref, te_ref, rk_ref, ten_ref, rkn_ref, y_hbm, w_ref, xs1_ref, m5_ref, gf_ref, o_ref,
                    rows, sems):
    i = pl.program_id(0)
    nsteps = pl.num_programs(0)
    slot = i % 2
    tm = o_ref.shape[0]
    nrow = TOP_K * tm

    def start_row(e_ref, r_ref, r, dst_slot):
        p = es_ref[e_ref[0, 0, r]] + r_ref[0, 0, r]
        pltpu.make_async_copy(
            y_hbm.at[pl.ds(pl.multiple_of(p * SUBLANES, SUBLANES), SUBLANES), :],
            rows.at[dst_slot, pl.ds(pl.multiple_of(r * SUBLANES, SUBLANES), SUBLANES), :],
            sems.at[dst_slot]).start()

    def wait_tile(s):
        pltpu.make_async_copy(y_hbm.at[pl.ds(0, nrow * SUBLANES), :], rows.at[s], sems.at[s]).wait()

    def issue(e_ref, r_ref, dst_slot):
        def body(r, carry):
            start_row(e_ref, r_ref, r, dst_slot)
            return carry
        lax.fori_loop(0, nrow, body, 0, unroll=8)

    @pl.when(i == 0)
    def _():
        issue(te_ref, rk_ref, 0)

    @pl.when(i + 1 < nsteps)
    def _():
        issue(ten_ref, rkn_ref, 1 - slot)

    wait_tile(slot)
    w = w_ref[...]
    acc_lo = acc_hi = None
    for k in range(TOP_K):
        lo, hi = _unpack_bf16_pair(_load_row_tiles(rows.at[slot], k * tm, tm))
        wk = w[:, k:k + 1]
        acc_lo = wk * lo if acc_lo is None else acc_lo + wk * lo
        acc_hi = wk * hi if acc_hi is None else acc_hi + wk * hi
    routed = jnp.concatenate([acc_lo, acc_hi], axis=1)
    x = xs1_ref[...] + m5_ref[0] * routed
    ms = jnp.mean(x * x, axis=-1, keepdims=True)
    o_ref[...] = x * lax.rsqrt(ms + EPS) * gf_ref[...]


def _combine(expert_slot0, top_e, rank, y_packed, w_tok, xs1, m5, g_final, rows_per_batch, tm):
    n, d = xs1.shape
    assert d == 2 * SUBLANES * LANES
    nt = n // tm
    tpb = rows_per_batch // tm
    tiles = lambda a: jnp.transpose(a.reshape(TOP_K, nt, tm), (1, 0, 2)).reshape(nt, 1, TOP_K * tm)
    te3, rk3 = tiles(top_e), tiles(rank)
    cur = pl.BlockSpec((1, 1, TOP_K * tm), lambda i, es: (i, 0, 0), memory_space=pltpu.SMEM)
    nxt = pl.BlockSpec((1, 1, TOP_K * tm), lambda i, es: (jnp.minimum(i + 1, nt - 1), 0, 0),
                       memory_space=pltpu.SMEM)
    grid_spec = pltpu.PrefetchScalarGridSpec(
        num_scalar_prefetch=1,
        grid=(nt,),
        in_specs=[cur, cur, nxt, nxt,
                  pl.BlockSpec(memory_space=pl.ANY),
                  pl.BlockSpec((tm, TOP_K), lambda i, es: (i, 0)),
                  pl.BlockSpec((tm, d), lambda i, es: (i, 0)),
                  pl.BlockSpec((1, 1, d), lambda i, es: (i // tpb, 0, 0)),
                  pl.BlockSpec((1, d), lambda i, es: (0, 0))],
        out_specs=pl.BlockSpec((tm, d), lambda i, es: (i, 0)),
        scratch_shapes=[pltpu.VMEM((2, TOP_K * tm * SUBLANES, LANES), U32), pltpu.SemaphoreType.DMA((2,))],
    )
    return pl.pallas_call(
        _combine_kernel,
        grid_spec=grid_spec,
        out_shape=jax.ShapeDtypeStruct((n, d), F32),
        compiler_params=_cparams(("arbitrary",)),
        name="combine",
    )(expert_slot0, te3, rk3, te3, rk3, y_packed, w_tok, xs1, m5, g_final.reshape(1, d))


def _dispatch_plan(top_e, counts, n_tok):
    mb = MOE_BLOCK
    ne = counts.shape[0]
    n_asg = TOP_K * n_tok
    nblk = -(-(n_asg + ne * (mb - 1)) // mb) + 1
    tok_ids = jnp.broadcast_to(jnp.arange(n_tok, dtype=I32)[None, :], top_e.shape)
    keys = jnp.sort((top_e * n_tok + tok_ids).reshape(-1))
    sorted_tok = jnp.concatenate([keys % n_tok, jnp.zeros((TOK_WINDOW,), I32)])
    start = jnp.cumsum(counts) - counts
    eblk = (counts + mb - 1) // mb
    pend = jnp.cumsum(eblk)
    pstart = pend - eblk
    n_real = pend[-1]
    bid = jnp.arange(nblk, dtype=I32)
    last_real = jnp.maximum(n_real - 1, 0)
    bsrc = jnp.minimum(bid, last_real)
    block_e = jnp.minimum(jnp.sum((pend[None, :] <= bsrc[:, None]).astype(I32), axis=1), ne - 1)
    onehot = block_e[:, None] == jnp.arange(ne, dtype=I32)[None, :]
    look = lambda table: jnp.sum(jnp.where(onehot, table[None, :], 0), axis=1)
    off = (bsrc - look(pstart)) * mb
    nvalid = jnp.where(bid < n_real, jnp.clip(look(counts) - off, 0, mb), 0).astype(I32)
    a0 = (look(start) + off).astype(I32)
    return block_e.astype(I32), nvalid, a0, sorted_tok, start.astype(I32)


def kernel(x, c, ctx, c_ctx, w_ada, b_ada, g_mix, w_in, conv_w, conv_b, dt_bias, a_log, d_skip, g_ssd,
           pool_w, pool_scale, w_out, g_ffn, w_router, router_bias, w_exp_gate, w_exp_up, w_exp_down,
           w_sh_gate, w_sh_up, w_sh_down, g_final):
    bsz, seq, d = x.shape
    ctx_len = ctx.shape[1]
    assert w_ada.shape[0] == 1, "single-layer block"
    d_ssm = g_ssd.shape[1]
    heads = d_skip.shape[1]
    d_pool = pool_scale.shape[1]
    d_xbc = conv_w.shape[2]
    n = bsz * seq

    cvec = jnp.zeros((8, d), F32).at[:bsz].set(c).at[bsz].set(c_ctx)
    mod_all = _ada(cvec, w_ada[0], b_ada[0]).reshape(8, N_MOD, d)
    mod = mod_all[:bsz]
    mod_c = mod_all[bsz:bsz + 1]
    mk = lambda m, k: m[:, k:k + 1, :]

    wt = jnp.transpose(w_in[0])
    c_dt = d_ssm + d_xbc
    w_main = jnp.concatenate([wt[:c_dt], wt[c_dt + 2 * heads:]], axis=0).astype(BF16)
    w_dt = wt[c_dt:c_dt + 2 * heads].reshape(2, SSM_GROUPS, heads // SSM_GROUPS, d)
    w_dt = jnp.transpose(w_dt, (1, 0, 2, 3)).reshape(2 * heads, d)
    w_dt = jnp.pad(w_dt, ((0, LANES - 2 * heads), (0, 0)))

    main_c, dt_c = _inproj(ctx.reshape(bsz * ctx_len, d), g_mix[0], mk(mod_c, 0), mk(mod_c, 1),
                           w_main, w_dt, bsz * ctx_len, tm=256, tn=1024)
    h_zero = jnp.zeros((bsz, SSM_GROUPS, 2, SSM_STATE, 4 * SSM_HEADDIM), F32)
    h_ctx = _ssd(main_c.reshape(bsz, ctx_len, -1), dt_c, conv_w[0], conv_b[0],
                 dt_bias[0], a_log[0], None, h_zero, False, d_ssm, d_ssm)

    x2d = x.reshape(n, d)
    main, dt_raw = _inproj(x2d, g_mix[0], mk(mod, 0), mk(mod, 1), w_main, w_dt, seq, tm=min(1024, seq), tn=1024)
    main3 = main.reshape(bsz, seq, -1)
    dsk = jnp.repeat(d_skip[0].astype(F32), SSM_HEADDIM).reshape(1, d_ssm)
    yg = _ssd(main3, dt_raw, conv_w[0], conv_b[0], dt_bias[0], a_log[0], dsk, h_ctx, True, d_ssm, d_ssm)
    yp = _pool(main3, pool_w[0], pool_scale[0], d_ssm + d_xbc)

    wo = w_out[0].astype(BF16)
    wr = w_router[0].T
    wr_hi = wr.astype(BF16)
    wr_lo = (wr - wr_hi.astype(F32)).astype(BF16)
    xs1, h2p, lg_t = _outproj(
        yg.reshape(n, d_ssm), yp.reshape(n, d_pool), x2d, g_ssd[0],
        (mk(mod, 2), mk(mod, 3), mk(mod, 4), mk(mod, 5)), g_ffn[0],
        wo[:d_ssm], wo[d_ssm:], wr_hi, wr_lo,
        w_sh_gate[0].astype(BF16), w_sh_up[0].astype(BF16), w_sh_down[0].astype(BF16), seq, tm=256)

    top_e, top_w, rank, cnt = _topk(lg_t, router_bias[0], tm=512)
    counts = cnt[:, 0].astype(I32)
    block_e, nvalid, a0, sorted_tok, expert_slot0 = _dispatch_plan(top_e, counts, n)
    y_packed = _moe(block_e, nvalid, a0, sorted_tok, h2p, w_exp_gate[0], w_exp_up[0], w_exp_down[0],
                    TOP_K * n)
    out = _combine(expert_slot0, top_e, rank, y_packed, top_w.T, xs1, mk(mod, 5), g_final, seq, tm=128)
    return out.reshape(bsz, seq, d)
```

```python
import functools

import jax
import jax.numpy as jnp
from jax import lax
from jax.experimental import pallas as pl
from jax.experimental.pallas import tpu as pltpu

F32 = jnp.float32
BF16 = jnp.bfloat16
I32 = jnp.int32
U32 = jnp.uint32

EPS = 1e-6
GRID_W = 64
SSM_HEADDIM = 64
SSM_GROUPS = 8
SSM_STATE = 128
SSD_CHUNK = 128
POOL_WINDOWS = (2, 4, 8, 16)
TOP_K = 8
N_EXPERT_GROUPS = 8
TOPK_GROUPS = 4
ROUTE_SCALE = 2.5
N_MOD = 6
LANES = 128
SUBLANES = 8
MOE_BLOCK = 512
GATHER_GROUP = 64
MOE_SUB = 128
MOE_SUB_GROUPS = 16
ROW_PAD = SUBLANES
TOK_WINDOW = pl.next_power_of_2(LANES + MOE_BLOCK + 2 * ROW_PAD)
V7X_VMEM_LIMIT = 56 * 1024 * 1024


def _cparams(sem, vmem=V7X_VMEM_LIMIT):
    return pltpu.CompilerParams(dimension_semantics=sem, vmem_limit_bytes=vmem)


def _sigmoid(x):
    return 1.0 / (1.0 + jnp.exp(-x))


def _silu(x):
    return x * _sigmoid(x)


def _split2(x):
    hi = x.astype(BF16)
    lo = (x - hi.astype(F32)).astype(BF16)
    return hi, lo


def _dot(a, b):
    return jnp.dot(a, b, preferred_element_type=F32)


def _dot_nt(a, b):
    return lax.dot_general(a, b, (((1,), (1,)), ((), ())), preferred_element_type=F32)


def _dot_tn(a, b):
    return lax.dot_general(a, b, (((0,), (0,)), ((), ())), preferred_element_type=F32)


def _dot_w(a, w):
    return lax.dot_general(a, w, (((1,), (0,)), ((), ())), preferred_element_type=F32)


def _dot3(a, b):
    a_hi, a_lo = _split2(a)
    b_hi, b_lo = _split2(b)
    return _dot(a_hi, b_hi) + _dot(a_lo, b_hi) + _dot(a_hi, b_lo)


def _ada_kernel(c_ref, w_ref, b_ref, o_ref):
    c = c_ref[...]
    o_ref[...] = _dot3(_silu(c), w_ref[...]) + b_ref[...]


def _ada(cvec, w_ada, b_ada, tn=1024):
    d, n = w_ada.shape
    return pl.pallas_call(
        _ada_kernel,
        grid=(n // tn,),
        in_specs=[pl.BlockSpec((8, d), lambda j: (0, 0)),
                  pl.BlockSpec((d, tn), lambda j: (0, j)),
                  pl.BlockSpec((1, tn), lambda j: (0, j))],
        out_specs=pl.BlockSpec((8, tn), lambda j: (0, j)),
        out_shape=jax.ShapeDtypeStruct((8, n), F32),
        compiler_params=_cparams(("parallel",)),
        name="ada",
    )(cvec, w_ada, b_ada.reshape(1, n))


def _inproj_kernel(na, x_ref, g_ref, sh_ref, sc_ref, wa_ref, wb_ref, wdt_ref, o_ref, dt_ref, h_scr):
    j = pl.program_id(1)

    @pl.when(j == 0)
    def _():
        x = x_ref[...]
        ms = jnp.mean(x * x, axis=-1, keepdims=True)
        y = x * lax.rsqrt(ms + EPS) * g_ref[...]
        h = y * (1.0 + sc_ref[0]) + sh_ref[0]
        h_hi, h_lo = _split2(h)
        h_scr[...] = h_hi
        w_hi, w_lo = _split2(wdt_ref[...])
        dt_ref[...] = _dot_nt(w_hi, h_hi) + _dot_nt(w_lo, h_hi) + _dot_nt(w_hi, h_lo)

    @pl.when(j < na)
    def _():
        o_ref[...] = _dot_nt(h_scr[...], wa_ref[...]).astype(o_ref.dtype)

    @pl.when(j >= na)
    def _():
        o_ref[...] = _dot_nt(h_scr[...], wb_ref[...]).astype(o_ref.dtype)


def _inproj(x2d, g, shift, scale, w_a_t, w_b_t, w_dt_t, rows_per_batch, tm, tn):
    n, d = x2d.shape
    na, nbk = w_a_t.shape[0] // tn, w_b_t.shape[0] // tn
    nc = (na + nbk) * tn
    tpb = rows_per_batch // tm
    return pl.pallas_call(
        functools.partial(_inproj_kernel, na),
        grid=(n // tm, nc // tn),
        in_specs=[pl.BlockSpec((tm, d), lambda i, j: (i, 0)),
                  pl.BlockSpec((1, d), lambda i, j: (0, 0)),
                  pl.BlockSpec((1, 1, d), lambda i, j: (i // tpb, 0, 0)),
                  pl.BlockSpec((1, 1, d), lambda i, j: (i // tpb, 0, 0)),
                  pl.BlockSpec((tn, d), lambda i, j: (jnp.minimum(j, na - 1), 0)),
                  pl.BlockSpec((tn, d), lambda i, j: (jnp.maximum(j - na, 0), 0)),
                  pl.BlockSpec((LANES, d), lambda i, j: (0, 0))],
        out_specs=[pl.BlockSpec((tm, tn), lambda i, j: (i, j)),
                   pl.BlockSpec((LANES, tm), lambda i, j: (0, i))],
        out_shape=[jax.ShapeDtypeStruct((n, nc), BF16),
                   jax.ShapeDtypeStruct((LANES, n), F32)],
        scratch_shapes=[pltpu.VMEM((tm, d), BF16)],
        compiler_params=_cparams(("parallel", "arbitrary")),
        name="inproj",
    )(x2d, g.reshape(1, d), shift, scale, w_a_t, w_b_t, w_dt_t)


def _expand_heads(v, base, width):
    t = v.shape[0]
    lane = lax.broadcasted_iota(I32, (t, 4 * width), 1)
    out = jnp.broadcast_to(v[:, base + 3:base + 4], (t, 4 * width))
    for j in (2, 1, 0):
        out = jnp.where(lane < (j + 1) * width, v[:, base + j:base + j + 1], out)
    return out


def _conv_silu(src_ref, w_ref, b_ref, dst_ref, length, tile):
    c = src_ref.shape[-1]
    w = w_ref[...]
    b = b_ref[...]
    rid = lax.broadcasted_iota(I32, (tile, c), 0)
    for r0 in range(0, length, tile):
        cur = src_ref[0, r0:r0 + tile, :].astype(F32)
        if r0 == 0:
            prev_row = jnp.zeros((1, c), F32)
        else:
            prev_row = src_ref[0, r0 - 16:r0, :].astype(F32)[15:16, :]
        if r0 + tile == length:
            next_row = jnp.zeros((1, c), F32)
        else:
            next_row = src_ref[0, r0 + tile:r0 + tile + 16, :].astype(F32)[0:1, :]
        up = jnp.where(rid == 0, prev_row, pltpu.roll(cur, 1, 0))
        dn = jnp.where(rid == tile - 1, next_row, pltpu.roll(cur, tile - 1, 0))
        o = up * w[0:1, :] + cur * w[1:2, :] + dn * w[2:3, :] + b
        dst_ref[r0:r0 + tile, :] = _silu(o).astype(dst_ref.dtype)


def _softplus(x):
    return jnp.maximum(x, 0.0) + jnp.log(1.0 + jnp.exp(-jnp.abs(x)))


def _ssd_kernel(need_y, length, *refs):
    t = SSD_CHUNK
    nch = length // t
    hw = 4 * SSM_HEADDIM
    if need_y:
        (xr, br, cr, zr, cwx, cbx, cwb, cbb, cwc, cbc, dtr, bias_r, a_r, dsk, h0,
         y_ref, xs_s, b_s, c_s, e_s, dtr_s, csr_s, csc_s, st_s, dec_s) = refs
    else:
        (xr, br, cwx, cbx, cwb, cbb, dtr, bias_r, a_r, h0,
         hfin, xs_s, b_s, dtr_s, csr_s, csc_s, st_s, dec_s) = refs

    ctile = min(256, length)
    _conv_silu(xr, cwx, cbx, xs_s, length, ctile)
    _conv_silu(br, cwb, cbb, b_s, length, ctile)
    if need_y:
        _conv_silu(cr, cwc, cbc, c_s, length, ctile)

    dt_r = _softplus(dtr[0, 0] + bias_r[0])
    dtr_s[...] = dt_r
    da2 = (dt_r * a_r[0]).reshape(nch * 8, t)
    kk = lax.broadcasted_iota(I32, (t, 2 * t), 0)
    ll = lax.broadcasted_iota(I32, (t, 2 * t), 1)
    tri = jnp.where(ll < t, jnp.where(kk <= ll, 1.0, 0.0), jnp.where(kk >= ll - t, 1.0, 0.0)).astype(BF16)
    p0 = da2.astype(BF16)
    r1 = da2 - p0.astype(F32)
    p1 = r1.astype(BF16)
    p2 = (r1 - p1.astype(F32)).astype(BF16)
    cum = _dot(p0, tri) + _dot(p1, tri) + _dot(p2, tri)
    rowj = lax.broadcasted_iota(I32, (nch * 8, t), 0) & 7
    csr = jnp.where(rowj < 4, cum[:, :t], cum[:, t:])
    csr_s[...] = csr.reshape(nch, 8, t)

    fwd_col = lax.broadcasted_iota(I32, (1, 8), 1) < 4

    def phase_a(c, carry):
        r0 = pl.multiple_of(c * t, t)
        rows16 = jnp.concatenate([csr_s[c], dtr_s[c], jnp.zeros((t - 16, t), F32)], axis=0)
        cols = rows16.T
        csc = cols[:, 0:8]
        csc_s[c] = csc
        if need_y:
            e_s[c] = jnp.exp(csc)
        edge = jnp.where(fwd_col, csc[t - 1:t, :], csc[0:1, :])
        wcol = cols[:, 8:16] * jnp.exp(edge - csc)
        dec = jnp.exp(edge)
        xs = xs_s[pl.ds(r0, t), :]
        bc = b_s[pl.ds(r0, t), :]
        xw_f = (xs * _expand_heads(wcol, 0, SSM_HEADDIM)).astype(BF16)
        xw_b = (xs * _expand_heads(wcol, 4, SSM_HEADDIM)).astype(BF16)
        st_s[c, 0] = _dot_tn(bc, xw_f)
        st_s[c, 1] = _dot_tn(bc, xw_b)
        dec_s[c, 0:1, :] = _expand_heads(dec, 0, SSM_HEADDIM)
        dec_s[c, 1:2, :] = _expand_heads(dec, 4, SSM_HEADDIM)
        return carry

    lax.fori_loop(0, nch, phase_a, 0, unroll=2)

    def rec_f(c, s):
        loc = st_s[c, 0]
        st_s[c, 0] = s
        return dec_s[c, 0:1, :] * s + loc

    def rec_b(k, s):
        c = nch - 1 - k
        loc = st_s[c, 1]
        st_s[c, 1] = s
        return dec_s[c, 1:2, :] * s + loc

    s_f = lax.fori_loop(0, nch, rec_f, h0[0, 0, 0])
    s_b = lax.fori_loop(0, nch, rec_b, h0[0, 0, 1])
    if not need_y:
        hfin[0, 0, 0] = s_f
        hfin[0, 0, 1] = s_b
        return

    li = lax.broadcasted_iota(I32, (t, t), 0)
    si = lax.broadcasted_iota(I32, (t, t), 1)
    dskip = dsk[...]

    def phase_c(c, carry):
        r0 = pl.multiple_of(c * t, t)
        csr_c = csr_s[c]
        csc_c = csc_s[c]
        dtr_c = dtr_s[c]
        e_c = e_s[c]
        xs = xs_s[pl.ds(r0, t), :]
        xs_b = xs.astype(BF16)
        bc = b_s[pl.ds(r0, t), :]
        cc = c_s[pl.ds(r0, t), :]
        cb = _dot_nt(cc, bc)
        ys = []
        for j in range(4):
            d_f = csc_c[:, j:j + 1] - csr_c[j:j + 1, :]
            att_f = jnp.where(si <= li, (cb * dtr_c[j:j + 1, :]) * jnp.exp(jnp.minimum(d_f, 0.0)), 0.0)
            d_b = csc_c[:, 4 + j:5 + j] - csr_c[4 + j:5 + j, :]
            att_b = jnp.where(si >= li, (cb * dtr_c[4 + j:5 + j, :]) * jnp.exp(jnp.minimum(d_b, 0.0)), 0.0)
            lhs = jnp.concatenate([att_f, att_b], axis=1).astype(BF16)
            xj = xs_b[:, j * SSM_HEADDIM:(j + 1) * SSM_HEADDIM]
            ys.append(_dot(lhs, jnp.concatenate([xj, xj], axis=0)))
        y = jnp.concatenate(ys, axis=1)
        y = y + _dot(cc, st_s[c, 0].astype(BF16)) * _expand_heads(e_c, 0, SSM_HEADDIM)
        y = y + _dot(cc, st_s[c, 1].astype(BF16)) * _expand_heads(e_c, 4, SSM_HEADDIM)
        y = y + xs * dskip
        z = zr[0, pl.ds(r0, t), :].astype(F32)
        y_ref[0, pl.ds(r0, t), :] = (y * _silu(z)).astype(y_ref.dtype)
        return carry

    lax.fori_loop(0, nch, phase_c, 0, unroll=2)


def _ssd(main3, dt_t, conv_w, conv_b, dt_bias, a_log, d_skip, h0, need_y, d_ssm, xbc_col0):
    nb, length, _ = main3.shape
    g_n, t = SSM_GROUPS, SSD_CHUNK
    nch = length // t
    heads = d_ssm // SSM_HEADDIM
    hpg = heads // g_n
    assert hpg == 4 and SSM_STATE == 128
    hw = hpg * SSM_HEADDIM
    dtr = jnp.transpose(dt_t[:2 * heads].reshape(g_n, 2 * hpg, nb, nch, t), (2, 0, 3, 1, 4))
    par = lambda p: jnp.transpose(p.reshape(2, g_n, hpg), (1, 0, 2)).reshape(g_n, 8)
    bias_r = par(dt_bias.astype(F32)).reshape(g_n, 8, 1)
    a_r = par(-jnp.exp(a_log.astype(F32))).reshape(g_n, 8, 1)
    nx = d_ssm
    nbc = g_n * SSM_STATE
    cwx, cwb, cwc = conv_w[:, :nx], conv_w[:, nx:nx + nbc], conv_w[:, nx + nbc:]
    cb2 = conv_b.reshape(1, -1)
    cbx, cbb, cbc = cb2[:, :nx], cb2[:, nx:nx + nbc], cb2[:, nx + nbc:]
    xcol = xbc_col0 // hw
    bcol = (xbc_col0 + nx) // SSM_STATE
    ccol = (xbc_col0 + nx + nbc) // SSM_STATE

    seq = lambda shape, imap: pl.BlockSpec(shape, imap)
    x_spec = seq((1, length, hw), lambda b, g: (b, 0, xcol + g))
    b_spec = seq((1, length, SSM_STATE), lambda b, g: (b, 0, bcol + g))
    c_spec = seq((1, length, SSM_STATE), lambda b, g: (b, 0, ccol + g))
    z_spec = seq((1, length, hw), lambda b, g: (b, 0, g))
    wx_spec = [seq((3, hw), lambda b, g: (0, g)), seq((1, hw), lambda b, g: (0, g))]
    wn_spec = [seq((3, SSM_STATE), lambda b, g: (0, g)), seq((1, SSM_STATE), lambda b, g: (0, g))]
    dt_specs = [seq((1, 1, nch, 8, t), lambda b, g: (b, g, 0, 0, 0)),
                seq((1, 8, 1), lambda b, g: (g, 0, 0)), seq((1, 8, 1), lambda b, g: (g, 0, 0))]
    h_spec = seq((1, 1, 2, SSM_STATE, hw), lambda b, g: (b, g, 0, 0, 0))
    common_scr = [pltpu.VMEM((nch, 8, t), F32),
                  pltpu.VMEM((nch, 8, t), F32),
                  pltpu.VMEM((nch, t, 8), F32),
                  pltpu.VMEM((nch, 2, SSM_STATE, hw), F32),
                  pltpu.VMEM((nch, 8, hw), F32)]
    if need_y:
        in_specs = ([x_spec, b_spec, c_spec, z_spec] + wx_spec + wn_spec + wn_spec + dt_specs
                    + [seq((1, hw), lambda b, g: (0, g)), h_spec])
        args = (main3, main3, main3, main3, cwx, cbx, cwb, cbb, cwc, cbc, dtr, bias_r, a_r, d_skip, h0)
        out_specs = seq((1, length, hw), lambda b, g: (b, 0, g))
        out_shape = jax.ShapeDtypeStruct((nb, length, d_ssm), BF16)
        scratch = [pltpu.VMEM((length, hw), F32), pltpu.VMEM((length, SSM_STATE), BF16),
                   pltpu.VMEM((length, SSM_STATE), BF16),
                   pltpu.VMEM((nch, t, 8), F32)] + common_scr
    else:
        in_specs = [x_spec, b_spec] + wx_spec + wn_spec + dt_specs + [h_spec]
        args = (main3, main3, cwx, cbx, cwb, cbb, dtr, bias_r, a_r, h0)
        out_specs = h_spec
        out_shape = jax.ShapeDtypeStruct((nb, g_n, 2, SSM_STATE, hw), F32)
        scratch = [pltpu.VMEM((length, hw), F32), pltpu.VMEM((length, SSM_STATE), BF16)] + common_scr
    return pl.pallas_call(
        functools.partial(_ssd_kernel, need_y, length),
        grid=(nb, g_n),
        in_specs=in_specs,
        out_specs=out_specs,
        out_shape=out_shape,
        scratch_shapes=scratch,
        compiler_params=_cparams(("parallel", "parallel")),
        name="ssd_y" if need_y else "ssd_state",
    )(*args)


def _pool_kernel(length, u_ref, pw_ref, ps_ref, o_ref, pad_s):
    gw = GRID_W
    rows = length // gw
    cg = pw_ref.shape[-1]
    halo = (max(POOL_WINDOWS) // 2) * gw
    tile = min(256, length)
    zeros = jnp.zeros((halo, cg), F32)
    pad_s[0:halo, :] = zeros
    pad_s[halo + length:halo + length + halo, :] = zeros
    for g, w in enumerate(POOL_WINDOWS):
        lo = -(w // 2)
        csl = slice(g * cg, (g + 1) * cg)
        for r0 in range(0, length, tile):
            pad_s[halo + r0:halo + r0 + tile, :] = u_ref[0, r0:r0 + tile, csl].astype(F32)
        pw = pw_ref[g]
        scale = ps_ref[:, csl]

        def body(i, carry, lo=lo, w=w, csl=csl, pw=pw, scale=scale):
            r0 = pl.multiple_of(i * tile, tile)
            acc = pad_s[pl.ds(halo + r0 + lo * gw, tile), :]
            for k in range(lo + 1, lo + w):
                acc = acc + pad_s[pl.ds(halo + r0 + k * gw, tile), :]
            l_idx = r0 + lax.broadcasted_iota(I32, (tile, cg), 0)
            ii = l_idx >> (gw.bit_length() - 1)
            jj = l_idx & (gw - 1)
            tot = acc
            for k in range(lo, lo + w):
                if k == 0:
                    continue
                sh = pltpu.roll(acc, (-k) % tile, 0)
                ok = (jj >= -k) if k < 0 else (jj < gw - k)
                tot = tot + jnp.where(ok, sh, 0.0)
            cnt_i = jnp.minimum(ii + lo + w, rows) - jnp.maximum(ii + lo, 0)
            cnt_j = jnp.minimum(jj + lo + w, gw) - jnp.maximum(jj + lo, 0)
            u = pad_s[pl.ds(halo + r0, tile), :]
            d = tot / (cnt_i * cnt_j).astype(F32) - u
            y = _dot(d.astype(BF16), pw) * scale
            o_ref[0, pl.ds(r0, tile), csl] = y.astype(o_ref.dtype)
            return carry

        lax.fori_loop(0, length // tile, body, 0)


def _pool(main3, pool_w, pool_scale, col0):
    nb, length, _ = main3.shape
    ng, cg, _ = pool_w.shape
    dp = ng * cg
    halo = (max(POOL_WINDOWS) // 2) * GRID_W
    return pl.pallas_call(
        functools.partial(_pool_kernel, length),
        grid=(nb,),
        in_specs=[pl.BlockSpec((1, length, dp), lambda b: (b, 0, col0 // dp)),
                  pl.BlockSpec((ng, cg, cg), lambda b: (0, 0, 0)),
                  pl.BlockSpec((1, dp), lambda b: (0, 0))],
        out_specs=pl.BlockSpec((1, length, dp), lambda b: (b, 0, 0)),
        out_shape=jax.ShapeDtypeStruct((nb, length, dp), BF16),
        scratch_shapes=[pltpu.VMEM((length + 2 * halo, cg), F32)],
        compiler_params=_cparams(("parallel",)),
        name="pool",
    )(main3, pool_w.astype(BF16), pool_scale.reshape(1, dp))


def _pack_bf16_pair(lo, hi):
    lo_b = lax.bitcast_convert_type(lo, U32) >> 16
    hi_b = lax.bitcast_convert_type(hi, U32) & jnp.uint32(0xFFFF0000)
    return lo_b | hi_b


def _unpack_bf16_pair(p):
    lo = lax.bitcast_convert_type(p << 16, F32)
    hi = lax.bitcast_convert_type(p & jnp.uint32(0xFFFF0000), F32)
    return lo, hi


def _store_row_tiles(ref, row0, value):
    m = value.shape[0]
    for c in range(SUBLANES):
        ref[pl.ds(row0 * SUBLANES + c, m, stride=SUBLANES), :] = value[:, c * LANES:(c + 1) * LANES]


def _load_row_tiles(ref, row0, m):
    return jnp.concatenate(
        [ref[pl.ds(row0 * SUBLANES + c, m, stride=SUBLANES), :] for c in range(SUBLANES)], axis=1)


def _outproj_kernel(yg_ref, yp_ref, x_ref, gs_ref, m2_ref, m3_ref, m4_ref, m5_ref, gf_ref,
                    ws_ref, wp_ref, wrh_ref, wrl_ref, wg_ref, wu_ref, wd_ref,
                    xs1_ref, h2p_ref, lg_ref):
    yg = yg_ref[...].astype(F32)
    ms = jnp.mean(yg * yg, axis=-1, keepdims=True)
    a = (yg * lax.rsqrt(ms + EPS) * gs_ref[...]).astype(BF16)
    o = _dot(a, ws_ref[...]) + _dot(yp_ref[...], wp_ref[...])
    x1 = x_ref[...] + m2_ref[0] * o
    ms1 = jnp.mean(x1 * x1, axis=-1, keepdims=True)
    h2 = x1 * lax.rsqrt(ms1 + EPS) * gf_ref[...]
    h2 = h2 * (1.0 + m4_ref[0]) + m3_ref[0]
    h_hi, h_lo = _split2(h2)
    wrh = wrh_ref[...]
    lg_ref[...] = _dot_nt(wrh, h_hi) + _dot_nt(wrl_ref[...], h_hi) + _dot_nt(wrh, h_lo)
    act = (_silu(_dot(h_hi, wg_ref[...])) * _dot(h_hi, wu_ref[...])).astype(BF16)
    xs1_ref[...] = x1 + m5_ref[0] * _dot(act, wd_ref[...])
    hf = h_hi.astype(F32)
    half = hf.shape[1] // 2
    _store_row_tiles(h2p_ref, 0, _pack_bf16_pair(hf[:, :half], hf[:, half:]))


def _outproj(yg, yp, x2d, g_ssd, mods, g_ffn, w_ssd, w_pool, wr_hi, wr_lo, wsg, wsu, wsd,
             rows_per_batch, tm):
    n, d = x2d.shape
    dp = yp.shape[1]
    ne = wr_hi.shape[0]
    dsh = wsg.shape[1]
    tpb = rows_per_batch // tm
    row = lambda c: pl.BlockSpec((tm, c), lambda i: (i, 0))
    vec = pl.BlockSpec((1, d), lambda i: (0, 0))
    mod = pl.BlockSpec((1, 1, d), lambda i: (i // tpb, 0, 0))
    res = lambda r, c: pl.BlockSpec((r, c), lambda i: (0, 0), pipeline_mode=pl.Buffered(1))
    m2, m3, m4, m5 = mods
    return pl.pallas_call(
        _outproj_kernel,
        grid=(n // tm,),
        in_specs=[row(d), row(dp), row(d), vec, mod, mod, mod, mod, vec,
                  res(d, d), res(dp, d), res(ne, d), res(ne, d), res(d, dsh), res(d, dsh), res(dsh, d)],
        out_specs=[row(d), pl.BlockSpec((tm * SUBLANES, LANES), lambda i: (i, 0)),
                   pl.BlockSpec((ne, tm), lambda i: (0, i))],
        out_shape=[jax.ShapeDtypeStruct((n, d), F32),
                   jax.ShapeDtypeStruct((n * SUBLANES, LANES), U32),
                   jax.ShapeDtypeStruct((ne, n), F32)],
        compiler_params=_cparams(("parallel",)),
        name="outproj",
    )(yg, yp, x2d, g_ssd.reshape(1, d), m2, m3, m4, m5, g_ffn.reshape(1, d),
      w_ssd, w_pool, wr_hi, wr_lo, wsg, wsu, wsd)


def _topk_kernel(lg_ref, rb_ref, te_ref, tw_ref, cnt_ref, carry):
    ne, tm = lg_ref.shape
    gsz = ne // N_EXPERT_GROUPS

    @pl.when(pl.program_id(0) == 0)
    def _():
        carry[...] = jnp.zeros_like(carry)

    s = _sigmoid(lg_ref[...])
    biased = s + rb_ref[...]
    neg = jnp.float32(-jnp.inf)
    big = jnp.int32(1 << 20)
    gi = lax.broadcasted_iota(I32, (gsz, tm), 0)
    gscore = []
    for g in range(N_EXPERT_GROUPS):
        v = biased[g * gsz:(g + 1) * gsz, :]
        m1 = jnp.max(v, axis=0, keepdims=True)
        i1 = jnp.min(jnp.where(v == m1, gi, big), axis=0, keepdims=True)
        m2 = jnp.max(jnp.where(gi == i1, neg, v), axis=0, keepdims=True)
        gscore.append(m1 + m2)
    parts = []
    for g in range(N_EXPERT_GROUPS):
        rank = jnp.zeros((1, tm), I32)
        for h in range(N_EXPERT_GROUPS):
            if h == g:
                continue
            ahead = (gscore[h] > gscore[g]) | ((gscore[h] == gscore[g]) & (h < g))
            rank = rank + ahead.astype(I32)
        keep = rank < TOPK_GROUPS
        parts.append(jnp.where(keep, biased[g * gsz:(g + 1) * gsz, :], neg))
    masked = jnp.concatenate(parts, axis=0)
    ei = lax.broadcasted_iota(I32, (ne, tm), 0)
    idxs, wts = [], []
    msel = jnp.zeros((ne, tm), F32)
    for _ in range(TOP_K):
        m = jnp.max(masked, axis=0, keepdims=True)
        idx = jnp.min(jnp.where(masked == m, ei, big), axis=0, keepdims=True)
        sel = ei == idx
        wts.append(jnp.sum(jnp.where(sel, s, 0.0), axis=0, keepdims=True))
        idxs.append(idx)
        masked = jnp.where(sel, neg, masked)
        msel = jnp.where(sel, 1.0, msel)
    wsum = wts[0]
    for w in wts[1:]:
        wsum = wsum + w
    for k in range(TOP_K):
        te_ref[k:k + 1, :] = idxs[k]
        tw_ref[k:k + 1, :] = wts[k] / wsum * ROUTE_SCALE
    total = carry[...] + _dot(msel.astype(BF16), jnp.ones((tm, LANES), BF16))
    carry[...] = total
    cnt_ref[...] = total


def _topk(lg_t, router_bias, tm):
    ne, n = lg_t.shape
    row8 = lambda dt: jax.ShapeDtypeStruct((TOP_K, n), dt)
    return pl.pallas_call(
        _topk_kernel,
        grid=(n // tm,),
        in_specs=[pl.BlockSpec((ne, tm), lambda i: (0, i)),
                  pl.BlockSpec((ne, 1), lambda i: (0, 0))],
        out_specs=[pl.BlockSpec((TOP_K, tm), lambda i: (0, i))] * 2
        + [pl.BlockSpec((ne, LANES), lambda i: (0, 0))],
        out_shape=[row8(I32), row8(F32), jax.ShapeDtypeStruct((ne, LANES), F32)],
        scratch_shapes=[pltpu.VMEM((ne, LANES), F32)],
        compiler_params=_cparams(("arbitrary",)),
        name="topk",
    )(lg_t, router_bias.reshape(ne, 1).astype(F32))


def _log2(n):
    assert n & (n - 1) == 0
    return n.bit_length() - 1


def _gather_groups(nvalid):
    return (nvalid + (GATHER_GROUP - 1)) >> _log2(GATHER_GROUP)


def _gather_rows(nvalid):
    return _gather_groups(nvalid) << _log2(GATHER_GROUP)


def _moe_kernel(n_tok, be_ref, nv_ref, a0_ref, tok_ref, tokn_ref, h_hbm, wg_ref, wu_ref, wd_ref, y_hbm,
                rows, stage, gsem, osem):
    del be_ref
    i = pl.program_id(0)
    nsteps = pl.num_programs(0)
    slot = i % 2
    id_group = SUBLANES
    tile = SUBLANES

    def start_group(tref, a0, g, dst):
        lead = a0 & (id_group - 1)
        w = (a0 & (LANES - 1)) - lead + g * id_group
        for u in range(id_group):
            tok = tref[w + u] & (n_tok - 1)
            dst_row = ROW_PAD - lead + g * id_group + u
            pltpu.make_async_copy(
                h_hbm.at[pl.ds(pl.multiple_of(tok * SUBLANES, SUBLANES), SUBLANES), :],
                rows.at[dst, pl.ds(pl.multiple_of(dst_row * SUBLANES, SUBLANES), SUBLANES), :],
                gsem.at[dst]).start()

    def issue(tref, a0, g_lo, g_hi, dst):
        def body(g, carry):
            start_group(tref, a0, g, dst)
            return carry
        lax.fori_loop(g_lo, g_hi, body, 0)

    def gather_groups(a0, nvalid):
        lead = a0 & (id_group - 1)
        return jnp.where(nvalid > 0, (lead + _gather_rows(nvalid) + id_group - 1) >> _log2(id_group), 0)

    def pieces(nvalid):
        rem = nvalid & (2 * MOE_SUB - 1)
        big = (nvalid >> _log2(2 * MOE_SUB)) + jnp.where(rem > MOE_SUB, 1, 0)
        small = jnp.where((rem > 0) & (rem <= MOE_SUB), 1, 0)
        return big, small

    def covered(nvalid):
        big, small = pieces(nvalid)
        return (2 * big + small) * MOE_SUB_GROUPS

    def out_copy(r, dst_row, src_slot, nrows=1):
        return pltpu.make_async_copy(
            stage.at[src_slot, pl.ds(pl.multiple_of(r * tile, tile), nrows * tile), :],
            y_hbm.at[pl.ds(pl.multiple_of(dst_row * tile, tile), nrows * tile), :],
            osem.at[src_slot])

    nv = nv_ref[i]
    a0 = a0_ref[i]
    nxt = jnp.minimum(i + 1, nsteps - 1)
    nv_n = jnp.where(i + 1 < nsteps, nv_ref[nxt], 0)
    a0_n = a0_ref[nxt]
    groups_n = gather_groups(a0_n, nv_n)
    covered_n = covered(nv)

    @pl.when(i == 0)
    def _():
        rows[...] = jnp.zeros_like(rows)
        issue(tok_ref, a0, 0, gather_groups(a0, nv), 0)

    issue(tokn_ref, a0_n, covered_n, groups_n, 1 - slot)

    prev = jnp.maximum(i - 1, 0)
    covered_here = jnp.where(i > 0, covered(nv_ref[prev]), 0)

    def wait_body(g, carry):
        pltpu.make_async_copy(h_hbm.at[pl.ds(0, id_group * SUBLANES), :],
                              rows.at[slot, pl.ds(0, id_group * SUBLANES), :], gsem.at[slot]).wait()
        return carry
    lax.fori_loop(0, jnp.maximum(gather_groups(a0, nv), covered_here), wait_body, 0)

    half = wg_ref.shape[1] // 2

    def piece(row0, m, g0):
        ng = (m // MOE_SUB) * MOE_SUB_GROUPS

        def start_next(k0, k1):
            for k in range(k0, k1):
                start_group(tokn_ref, a0_n, g0 + k, 1 - slot)

        q = ng // 4
        x_lo, x_hi = _unpack_bf16_pair(_load_row_tiles(rows.at[slot], ROW_PAD + row0, m))
        x_lo = x_lo.astype(BF16)
        x_hi = x_hi.astype(BF16)
        start_next(0, q)
        g = _dot_w(x_lo, wg_ref[0, :half, :])
        start_next(q, 2 * q)
        g = g + _dot_w(x_hi, wg_ref[0, half:, :])
        start_next(2 * q, 3 * q)
        u = _dot_w(x_lo, wu_ref[0, :half, :])
        start_next(3 * q, ng)
        u = u + _dot_w(x_hi, wu_ref[0, half:, :])
        act = (_silu(g) * u).astype(BF16)
        yb = _dot_w(act, wd_ref[0]).astype(BF16).astype(F32)
        _store_row_tiles(stage.at[slot], row0, _pack_bf16_pair(yb[:, :half], yb[:, half:]))

    n_big, n_small = pieces(nv)

    def big_piece(p, carry):
        piece(p * (2 * MOE_SUB), 2 * MOE_SUB, p * (2 * MOE_SUB_GROUPS))
        return carry
    lax.fori_loop(0, n_big, big_piece, 0)

    @pl.when(n_small > 0)
    def _():
        piece(n_big * (2 * MOE_SUB), MOE_SUB, n_big * (2 * MOE_SUB_GROUPS))

    nv_p = jnp.where(i > 0, nv_ref[prev], 0)

    def wait_prev_group(g, carry):
        out_copy(0, 0, 1 - slot, id_group).wait()
        return carry
    lax.fori_loop(0, nv_p >> _log2(id_group), wait_prev_group, 0)

    def wait_prev_row(g, carry):
        out_copy(0, 0, 1 - slot).wait()
        return carry
    lax.fori_loop(0, nv_p & (id_group - 1), wait_prev_row, 0)

    w_out0 = a0 & (LANES - 1)

    def start_out_group(g, carry):
        for u in range(id_group):
            r = g * id_group + u
            out_copy(r, tok_ref[w_out0 + r], slot).start()
        return carry
    lax.fori_loop(0, nv >> _log2(id_group), start_out_group, 0)

    def start_out_row(r, carry):
        out_copy(r, tok_ref[w_out0 + r], slot).start()
        return carry
    lax.fori_loop(nv & ~(id_group - 1), nv, start_out_row, 0)


def _moe(block_e, nvalid, a0, sorted_ids, h2p, w_gate, w_up, w_down, n_tok):
    nblk = block_e.shape[0]
    mb = MOE_BLOCK
    ne, d, f = w_gate.shape
    assert d == 2 * SUBLANES * LANES and mb % MOE_SUB == 0 and MOE_SUB % GATHER_GROUP == 0
    assert MOE_SUB_GROUPS % 4 == 0
    assert n_tok & (n_tok - 1) == 0, "token id = assignment id & (n_tok - 1)"

    def win(nxt):
        def imap(i, be, nv, a0):
            j = jnp.minimum(i + 1, nblk - 1) if nxt else i
            return (pl.multiple_of((a0[j] >> _log2(LANES)) << _log2(LANES), LANES),)
        return pl.BlockSpec((pl.Element(TOK_WINDOW),), imap, memory_space=pltpu.SMEM)

    wspec = lambda r, c: pl.BlockSpec((1, r, c), lambda i, be, nv, a0: (be[i], 0, 0))
    grid_spec = pltpu.PrefetchScalarGridSpec(
        num_scalar_prefetch=3,
        grid=(nblk,),
        in_specs=[win(False), win(True), pl.BlockSpec(memory_space=pl.ANY), wspec(d, f), wspec(d, f), wspec(f, d)],
        out_specs=pl.BlockSpec(memory_space=pl.ANY),
        scratch_shapes=[pltpu.VMEM((2, (mb + 2 * ROW_PAD) * SUBLANES, LANES), U32),
                        pltpu.VMEM((2, mb * SUBLANES, LANES), U32),
                        pltpu.SemaphoreType.DMA((2,)), pltpu.SemaphoreType.DMA((2,))],
    )
    return pl.pallas_call(
        functools.partial(_moe_kernel, n_tok),
        grid_spec=grid_spec,
        out_shape=jax.ShapeDtypeStruct((TOP_K * n_tok * SUBLANES, LANES), U32),
        compiler_params=_cparams(("arbitrary",)),
        name="moe",
    )(block_e, nvalid, a0, sorted_ids, sorted_ids, h2p, w_gate, w_up, w_down)


def _combine_kernel(*refs):
    y_refs = refs[:TOP_K]
    w_ref, xs1_ref, m5_ref, gf_ref, o_ref = refs[TOP_K:]
    tm = o_ref.shape[0]
    w = w_ref[...]
    acc_lo = acc_hi = None
    for k in range(TOP_K):
        lo, hi = _unpack_bf16_pair(_load_row_tiles(y_refs[k], 0, tm))
        wk = w[:, k:k + 1]
        acc_lo = wk * lo if acc_lo is None else acc_lo + wk * lo
        acc_hi = wk * hi if acc_hi is None else acc_hi + wk * hi
    routed = jnp.concatenate([acc_lo, acc_hi], axis=1)
    x = xs1_ref[...] + m5_ref[0] * routed
    ms = jnp.mean(x * x, axis=-1, keepdims=True)
    o_ref[...] = x * lax.rsqrt(ms + EPS) * gf_ref[...]


def _combine(y_packed, w_tok, xs1, m5, g_final, rows_per_batch, tm):
    n, d = xs1.shape
    assert d == 2 * SUBLANES * LANES
    nt = n // tm
    tpb = rows_per_batch // tm
    y_spec = lambda k: pl.BlockSpec((tm * SUBLANES, LANES), lambda i: (k * nt + i, 0))
    return pl.pallas_call(
        _combine_kernel,
        grid=(nt,),
        in_specs=[y_spec(k) for k in range(TOP_K)]
        + [pl.BlockSpec((tm, TOP_K), lambda i: (i, 0)),
           pl.BlockSpec((tm, d), lambda i: (i, 0)),
           pl.BlockSpec((1, 1, d), lambda i: (i // tpb, 0, 0)),
           pl.BlockSpec((1, d), lambda i: (0, 0))],
        out_specs=pl.BlockSpec((tm, d), lambda i: (i, 0)),
        out_shape=jax.ShapeDtypeStruct((n, d), F32),
        compiler_params=_cparams(("parallel",)),
        name="combine",
    )(*([y_packed] * TOP_K), w_tok, xs1, m5, g_final.reshape(1, d))


def _dispatch_plan(top_e, counts, n_tok):
    mb = MOE_BLOCK
    ne = counts.shape[0]
    n_asg = TOP_K * n_tok
    nblk = -(-(n_asg + ne * (mb - 1)) // mb) + 1
    asg_ids = jnp.arange(n_asg, dtype=I32).reshape(top_e.shape)
    keys = jnp.sort((top_e * n_asg + asg_ids).reshape(-1))
    sorted_ids = jnp.concatenate([keys % n_asg, jnp.zeros((TOK_WINDOW,), I32)])
    start = jnp.cumsum(counts) - counts
    eblk = (counts + mb - 1) // mb
    pend = jnp.cumsum(eblk)
    pstart = pend - eblk
    n_real = pend[-1]
    bid = jnp.arange(nblk, dtype=I32)
    last_real = jnp.maximum(n_real - 1, 0)
    bsrc = jnp.minimum(bid, last_real)
    block_e = jnp.minimum(jnp.sum((pend[None, :] <= bsrc[:, None]).astype(I32), axis=1), ne - 1)
    onehot = block_e[:, None] == jnp.arange(ne, dtype=I32)[None, :]
    look = lambda table: jnp.sum(jnp.where(onehot, table[None, :], 0), axis=1)
    off = (bsrc - look(pstart)) * mb
    nvalid = jnp.where(bid < n_real, jnp.clip(look(counts) - off, 0, mb), 0).astype(I32)
    a0 = (look(start) + off).astype(I32)
    return block_e.astype(I32), nvalid, a0, sorted_ids


def kernel(x, c, ctx, c_ctx, w_ada, b_ada, g_mix, w_in, conv_w, conv_b, dt_bias, a_log, d_skip, g_ssd,
           pool_w, pool_scale, w_out, g_ffn, w_router, router_bias, w_exp_gate, w_exp_up, w_exp_down,
           w_sh_gate, w_sh_up, w_sh_down, g_final):
    bsz, seq, d = x.shape
    ctx_len = ctx.shape[1]
    assert w_ada.shape[0] == 1, "single-layer block"
    d_ssm = g_ssd.shape[1]
    heads = d_skip.shape[1]
    d_pool = pool_scale.shape[1]
    d_xbc = conv_w.shape[2]
    n = bsz * seq

    cvec = jnp.zeros((8, d), F32).at[:bsz].set(c).at[bsz].set(c_ctx)
    mod_all = _ada(cvec, w_ada[0], b_ada[0]).reshape(8, N_MOD, d)
    mod = mod_all[:bsz]
    mod_c = mod_all[bsz:bsz + 1]
    mk = lambda m, k: m[:, k:k + 1, :]

    wt = jnp.transpose(w_in[0])
    c_dt = d_ssm + d_xbc
    w_zx = wt[:c_dt].astype(BF16)
    w_pl = wt[c_dt + 2 * heads:].astype(BF16)
    w_dt = wt[c_dt:c_dt + 2 * heads].reshape(2, SSM_GROUPS, heads // SSM_GROUPS, d)
    w_dt = jnp.transpose(w_dt, (1, 0, 2, 3)).reshape(2 * heads, d)
    w_dt = jnp.pad(w_dt, ((0, LANES - 2 * heads), (0, 0)))

    main_c, dt_c = _inproj(ctx.reshape(bsz * ctx_len, d), g_mix[0], mk(mod_c, 0), mk(mod_c, 1),
                           w_zx, w_pl, w_dt, bsz * ctx_len, tm=256, tn=1024)
    h_zero = jnp.zeros((bsz, SSM_GROUPS, 2, SSM_STATE, 4 * SSM_HEADDIM), F32)
    h_ctx = _ssd(main_c.reshape(bsz, ctx_len, -1), dt_c, conv_w[0], conv_b[0],
                 dt_bias[0], a_log[0], None, h_zero, False, d_ssm, d_ssm)

    x2d = x.reshape(n, d)
    main, dt_raw = _inproj(x2d, g_mix[0], mk(mod, 0), mk(mod, 1), w_zx, w_pl, w_dt, seq,
                           tm=min(1024, seq), tn=1024)
    main3 = main.reshape(bsz, seq, -1)
    dsk = jnp.repeat(d_skip[0].astype(F32), SSM_HEADDIM).reshape(1, d_ssm)
    yg = _ssd(main3, dt_raw, conv_w[0], conv_b[0], dt_bias[0], a_log[0], dsk, h_ctx, True, d_ssm, d_ssm)
    yp = _pool(main3, pool_w[0], pool_scale[0], d_ssm + d_xbc)

    wo = w_out[0].astype(BF16)
    wr = w_router[0].T
    wr_hi = wr.astype(BF16)
    wr_lo = (wr - wr_hi.astype(F32)).astype(BF16)
    xs1, h2p, lg_t = _outproj(
        yg.reshape(n, d_ssm), yp.reshape(n, d_pool), x2d, g_ssd[0],
        (mk(mod, 2), mk(mod, 3), mk(mod, 4), mk(mod, 5)), g_ffn[0],
        wo[:d_ssm], wo[d_ssm:], wr_hi, wr_lo,
        w_sh_gate[0].astype(BF16), w_sh_up[0].astype(BF16), w_sh_down[0].astype(BF16), seq, tm=256)

    top_e, top_w, cnt = _topk(lg_t, router_bias[0], tm=512)
    counts = cnt[:, 0].astype(I32)
    block_e, nvalid, a0, sorted_ids = _dispatch_plan(top_e, counts, n)
    y_packed = _moe(block_e, nvalid, a0, sorted_ids, h2p, w_exp_gate[0], w_exp_up[0], w_exp_down[0], n)
    out = _combine(y_packed, top_w.T, xs1, mk(mod, 5), g_final, seq, tm=256)
    return out.reshape(bsz, seq, d)
```

```python
import functools

import jax
import jax.numpy as jnp
from jax import lax
from jax.experimental import pallas as pl
from jax.experimental.pallas import tpu as pltpu

F32 = jnp.float32
BF16 = jnp.bfloat16
I32 = jnp.int32
U32 = jnp.uint32

EPS = 1e-6
GRID_W = 64
SSM_HEADDIM = 64
SSM_GROUPS = 8
SSM_STATE = 128
SSD_CHUNK = 128
POOL_WINDOWS = (2, 4, 8, 16)
TOP_K = 8
N_EXPERT_GROUPS = 8
TOPK_GROUPS = 4
ROUTE_SCALE = 2.5
N_MOD = 6
LANES = 128
SUBLANES = 8
MOE_BLOCK = 512
GATHER_GROUP = 64
MOE_SUB = 128
MOE_SUB_GROUPS = 16
ROW_PAD = SUBLANES
TOK_WINDOW = pl.next_power_of_2(LANES + MOE_BLOCK + 2 * ROW_PAD)
V7X_VMEM_LIMIT = 56 * 1024 * 1024


def _cparams(sem, vmem=V7X_VMEM_LIMIT):
    return pltpu.CompilerParams(dimension_semantics=sem, vmem_limit_bytes=vmem)


def _sigmoid(x):
    return 1.0 / (1.0 + jnp.exp(-x))


def _silu(x):
    return x * _sigmoid(x)


def _split2(x):
    hi = x.astype(BF16)
    lo = (x - hi.astype(F32)).astype(BF16)
    return hi, lo


def _dot(a, b):
    return jnp.dot(a, b, preferred_element_type=F32)


def _dot_nt(a, b):
    return lax.dot_general(a, b, (((1,), (1,)), ((), ())), preferred_element_type=F32)


def _dot_tn(a, b):
    return lax.dot_general(a, b, (((0,), (0,)), ((), ())), preferred_element_type=F32)


def _dot_w(a, w):
    return lax.dot_general(a, w, (((1,), (0,)), ((), ())), preferred_element_type=F32)


def _dot3(a, b):
    a_hi, a_lo = _split2(a)
    b_hi, b_lo = _split2(b)
    return _dot(a_hi, b_hi) + _dot(a_lo, b_hi) + _dot(a_hi, b_lo)


def _ada_kernel(c_ref, w_ref, b_ref, o_ref):
    c = c_ref[...]
    o_ref[...] = _dot3(_silu(c), w_ref[...]) + b_ref[...]


def _ada(cvec, w_ada, b_ada, tn=1024):
    d, n = w_ada.shape
    return pl.pallas_call(
        _ada_kernel,
        grid=(n // tn,),
        in_specs=[pl.BlockSpec((8, d), lambda j: (0, 0)),
                  pl.BlockSpec((d, tn), lambda j: (0, j)),
                  pl.BlockSpec((1, tn), lambda j: (0, j))],
        out_specs=pl.BlockSpec((8, tn), lambda j: (0, j)),
        out_shape=jax.ShapeDtypeStruct((8, n), F32),
        compiler_params=_cparams(("parallel",)),
        name="ada",
    )(cvec, w_ada, b_ada.reshape(1, n))


def _inproj_kernel(na, x_ref, g_ref, sh_ref, sc_ref, wa_ref, wb_ref, wdt_ref, o_ref, dt_ref, h_scr):
    j = pl.program_id(1)

    @pl.when(j == 0)
    def _():
        x = x_ref[...]
        ms = jnp.mean(x * x, axis=-1, keepdims=True)
        y = x * lax.rsqrt(ms + EPS) * g_ref[...]
        h = y * (1.0 + sc_ref[0]) + sh_ref[0]
        h_hi, h_lo = _split2(h)
        h_scr[...] = h_hi
        w_hi, w_lo = _split2(wdt_ref[...])
        dt_ref[...] = _dot_nt(w_hi, h_hi) + _dot_nt(w_lo, h_hi) + _dot_nt(w_hi, h_lo)

    @pl.when(j < na)
    def _():
        o_ref[...] = _dot_nt(h_scr[...], wa_ref[...]).astype(o_ref.dtype)

    @pl.when(j >= na)
    def _():
        o_ref[...] = _dot_nt(h_scr[...], wb_ref[...]).astype(o_ref.dtype)


def _inproj(x2d, g, shift, scale, w_a_t, ca, w_b_t, w_dt_t, rows_per_batch, tm, tn):
    n, d = x2d.shape
    na, nbk = ca // tn, w_b_t.shape[0] // tn
    assert ca % tn == 0 and w_b_t.shape[0] % tn == 0
    nc = (na + nbk) * tn
    tpb = rows_per_batch // tm
    return pl.pallas_call(
        functools.partial(_inproj_kernel, na),
        grid=(n // tm, nc // tn),
        in_specs=[pl.BlockSpec((tm, d), lambda i, j: (i, 0)),
                  pl.BlockSpec((1, d), lambda i, j: (0, 0)),
                  pl.BlockSpec((1, 1, d), lambda i, j: (i // tpb, 0, 0)),
                  pl.BlockSpec((1, 1, d), lambda i, j: (i // tpb, 0, 0)),
                  pl.BlockSpec((tn, d), lambda i, j: (jnp.minimum(j, na - 1), 0)),
                  pl.BlockSpec((tn, d), lambda i, j: (jnp.maximum(j - na, 0), 0)),
                  pl.BlockSpec((LANES, d), lambda i, j: (0, 0))],
        out_specs=[pl.BlockSpec((tm, tn), lambda i, j: (i, j)),
                   pl.BlockSpec((LANES, tm), lambda i, j: (0, i))],
        out_shape=[jax.ShapeDtypeStruct((n, nc), BF16),
                   jax.ShapeDtypeStruct((LANES, n), F32)],
        scratch_shapes=[pltpu.VMEM((tm, d), BF16)],
        compiler_params=_cparams(("parallel", "arbitrary")),
        name="inproj",
    )(x2d, g.reshape(1, d), shift, scale, w_a_t, w_b_t, w_dt_t)


def _expand_heads(v, base, width):
    t = v.shape[0]
    lane = lax.broadcasted_iota(I32, (t, 4 * width), 1)
    out = jnp.broadcast_to(v[:, base + 3:base + 4], (t, 4 * width))
    for j in (2, 1, 0):
        out = jnp.where(lane < (j + 1) * width, v[:, base + j:base + j + 1], out)
    return out


def _conv_silu(src_ref, w_ref, b_ref, dst_ref, length, tile):
    c = src_ref.shape[-1]
    w = w_ref[...]
    b = b_ref[...]
    rid = lax.broadcasted_iota(I32, (tile, c), 0)
    for r0 in range(0, length, tile):
        cur = src_ref[0, r0:r0 + tile, :].astype(F32)
        if r0 == 0:
            prev_row = jnp.zeros((1, c), F32)
        else:
            prev_row = src_ref[0, r0 - 16:r0, :].astype(F32)[15:16, :]
        if r0 + tile == length:
            next_row = jnp.zeros((1, c), F32)
        else:
            next_row = src_ref[0, r0 + tile:r0 + tile + 16, :].astype(F32)[0:1, :]
        up = jnp.where(rid == 0, prev_row, pltpu.roll(cur, 1, 0))
        dn = jnp.where(rid == tile - 1, next_row, pltpu.roll(cur, tile - 1, 0))
        o = up * w[0:1, :] + cur * w[1:2, :] + dn * w[2:3, :] + b
        dst_ref[r0:r0 + tile, :] = _silu(o).astype(dst_ref.dtype)


def _softplus(x):
    return jnp.maximum(x, 0.0) + jnp.log(1.0 + jnp.exp(-jnp.abs(x)))


def _ssd_kernel(need_y, length, *refs):
    t = SSD_CHUNK
    nch = length // t
    hw = 4 * SSM_HEADDIM
    if need_y:
        (xr, br, cr, zr, cwx, cbx, cwb, cbb, cwc, cbc, dtr, bias_r, a_r, dsk, h0,
         y_ref, xs_s, b_s, c_s, e_s, dtr_s, csr_s, csc_s, st_s, dec_s) = refs
    else:
        (xr, br, cwx, cbx, cwb, cbb, dtr, bias_r, a_r, h0,
         hfin, xs_s, b_s, dtr_s, csr_s, csc_s, st_s, dec_s) = refs

    ctile = min(256, length)
    _conv_silu(xr, cwx, cbx, xs_s, length, ctile)
    _conv_silu(br, cwb, cbb, b_s, length, ctile)
    if need_y:
        _conv_silu(cr, cwc, cbc, c_s, length, ctile)

    dt_r = _softplus(dtr[0, 0] + bias_r[0])
    dtr_s[...] = dt_r
    da2 = (dt_r * a_r[0]).reshape(nch * 8, t)
    kk = lax.broadcasted_iota(I32, (t, 2 * t), 0)
    ll = lax.broadcasted_iota(I32, (t, 2 * t), 1)
    tri = jnp.where(ll < t, jnp.where(kk <= ll, 1.0, 0.0), jnp.where(kk >= ll - t, 1.0, 0.0)).astype(BF16)
    p0 = da2.astype(BF16)
    r1 = da2 - p0.astype(F32)
    p1 = r1.astype(BF16)
    p2 = (r1 - p1.astype(F32)).astype(BF16)
    cum = _dot(p0, tri) + _dot(p1, tri) + _dot(p2, tri)
    rowj = lax.broadcasted_iota(I32, (nch * 8, t), 0) & 7
    csr = jnp.where(rowj < 4, cum[:, :t], cum[:, t:])
    csr_s[...] = csr.reshape(nch, 8, t)

    fwd_col = lax.broadcasted_iota(I32, (1, 8), 1) < 4

    def phase_a(c, carry):
        r0 = pl.multiple_of(c * t, t)
        rows16 = jnp.concatenate([csr_s[c], dtr_s[c], jnp.zeros((t - 16, t), F32)], axis=0)
        cols = rows16.T
        csc = cols[:, 0:8]
        csc_s[c] = csc
        if need_y:
            e_s[c] = jnp.exp(csc)
        edge = jnp.where(fwd_col, csc[t - 1:t, :], csc[0:1, :])
        wcol = cols[:, 8:16] * jnp.exp(edge - csc)
        dec = jnp.exp(edge)
        xs = xs_s[pl.ds(r0, t), :]
        bc = b_s[pl.ds(r0, t), :]
        xw_f = (xs * _expand_heads(wcol, 0, SSM_HEADDIM)).astype(BF16)
        xw_b = (xs * _expand_heads(wcol, 4, SSM_HEADDIM)).astype(BF16)
        st_s[c, 0] = _dot_tn(bc, xw_f)
        st_s[c, 1] = _dot_tn(bc, xw_b)
        dec_s[c, 0:1, :] = _expand_heads(dec, 0, SSM_HEADDIM)
        dec_s[c, 1:2, :] = _expand_heads(dec, 4, SSM_HEADDIM)
        return carry

    lax.fori_loop(0, nch, phase_a, 0, unroll=2)

    def rec_f(c, s):
        loc = st_s[c, 0]
        st_s[c, 0] = s
        return dec_s[c, 0:1, :] * s + loc

    def rec_b(k, s):
        c = nch - 1 - k
        loc = st_s[c, 1]
        st_s[c, 1] = s
        return dec_s[c, 1:2, :] * s + loc

    s_f = lax.fori_loop(0, nch, rec_f, h0[0, 0, 0])
    s_b = lax.fori_loop(0, nch, rec_b, h0[0, 0, 1])
    if not need_y:
        hfin[0, 0, 0] = s_f
        hfin[0, 0, 1] = s_b
        return

    li = lax.broadcasted_iota(I32, (t, t), 0)
    si = lax.broadcasted_iota(I32, (t, t), 1)
    dskip = dsk[...]

    def phase_c(c, carry):
        r0 = pl.multiple_of(c * t, t)
        csr_c = csr_s[c]
        csc_c = csc_s[c]
        dtr_c = dtr_s[c]
        e_c = e_s[c]
        xs = xs_s[pl.ds(r0, t), :]
        xs_b = xs.astype(BF16)
        bc = b_s[pl.ds(r0, t), :]
        cc = c_s[pl.ds(r0, t), :]
        cb = _dot_nt(cc, bc)
        ys = []
        for j in range(4):
            d_f = csc_c[:, j:j + 1] - csr_c[j:j + 1, :]
            att_f = jnp.where(si <= li, (cb * dtr_c[j:j + 1, :]) * jnp.exp(jnp.minimum(d_f, 0.0)), 0.0)
            d_b = csc_c[:, 4 + j:5 + j] - csr_c[4 + j:5 + j, :]
            att_b = jnp.where(si >= li, (cb * dtr_c[4 + j:5 + j, :]) * jnp.exp(jnp.minimum(d_b, 0.0)), 0.0)
            lhs = jnp.concatenate([att_f, att_b], axis=1).astype(BF16)
            xj = xs_b[:, j * SSM_HEADDIM:(j + 1) * SSM_HEADDIM]
            ys.append(_dot(lhs, jnp.concatenate([xj, xj], axis=0)))
        y = jnp.concatenate(ys, axis=1)
        y = y + _dot(cc, st_s[c, 0].astype(BF16)) * _expand_heads(e_c, 0, SSM_HEADDIM)
        y = y + _dot(cc, st_s[c, 1].astype(BF16)) * _expand_heads(e_c, 4, SSM_HEADDIM)
        y = y + xs * dskip
        z = zr[0, pl.ds(r0, t), :].astype(F32)
        y_ref[0, pl.ds(r0, t), :] = (y * _silu(z)).astype(y_ref.dtype)
        return carry

    lax.fori_loop(0, nch, phase_c, 0, unroll=2)


def _ssd(main3, dt_t, conv_w, conv_b, dt_bias, a_log, d_skip, h0, need_y, d_ssm, xbc_col0):
    nb, length, _ = main3.shape
    g_n, t = SSM_GROUPS, SSD_CHUNK
    nch = length // t
    heads = d_ssm // SSM_HEADDIM
    hpg = heads // g_n
    assert hpg == 4 and SSM_STATE == 128
    hw = hpg * SSM_HEADDIM
    dtr = jnp.transpose(dt_t[:2 * heads].reshape(g_n, 2 * hpg, nb, nch, t), (2, 0, 3, 1, 4))
    par = lambda p: jnp.transpose(p.reshape(2, g_n, hpg), (1, 0, 2)).reshape(g_n, 8)
    bias_r = par(dt_bias.astype(F32)).reshape(g_n, 8, 1)
    a_r = par(-jnp.exp(a_log.astype(F32))).reshape(g_n, 8, 1)
    nx = d_ssm
    nbc = g_n * SSM_STATE
    cwx, cwb, cwc = conv_w[:, :nx], conv_w[:, nx:nx + nbc], conv_w[:, nx + nbc:]
    cb2 = conv_b.reshape(1, -1)
    cbx, cbb, cbc = cb2[:, :nx], cb2[:, nx:nx + nbc], cb2[:, nx + nbc:]
    xcol = xbc_col0 // hw
    bcol = (xbc_col0 + nx) // SSM_STATE
    ccol = (xbc_col0 + nx + nbc) // SSM_STATE

    seq = lambda shape, imap: pl.BlockSpec(shape, imap)
    x_spec = seq((1, length, hw), lambda b, g: (b, 0, xcol + g))
    b_spec = seq((1, length, SSM_STATE), lambda b, g: (b, 0, bcol + g))
    c_spec = seq((1, length, SSM_STATE), lambda b, g: (b, 0, ccol + g))
    z_spec = seq((1, length, hw), lambda b, g: (b, 0, g))
    wx_spec = [seq((3, hw), lambda b, g: (0, g)), seq((1, hw), lambda b, g: (0, g))]
    wn_spec = [seq((3, SSM_STATE), lambda b, g: (0, g)), seq((1, SSM_STATE), lambda b, g: (0, g))]
    dt_specs = [seq((1, 1, nch, 8, t), lambda b, g: (b, g, 0, 0, 0)),
                seq((1, 8, 1), lambda b, g: (g, 0, 0)), seq((1, 8, 1), lambda b, g: (g, 0, 0))]
    h_spec = seq((1, 1, 2, SSM_STATE, hw), lambda b, g: (b, g, 0, 0, 0))
    common_scr = [pltpu.VMEM((nch, 8, t), F32),
                  pltpu.VMEM((nch, 8, t), F32),
                  pltpu.VMEM((nch, t, 8), F32),
                  pltpu.VMEM((nch, 2, SSM_STATE, hw), F32),
                  pltpu.VMEM((nch, 8, hw), F32)]
    if need_y:
        in_specs = ([x_spec, b_spec, c_spec, z_spec] + wx_spec + wn_spec + wn_spec + dt_specs
                    + [seq((1, hw), lambda b, g: (0, g)), h_spec])
        args = (main3, main3, main3, main3, cwx, cbx, cwb, cbb, cwc, cbc, dtr, bias_r, a_r, d_skip, h0)
        out_specs = seq((1, length, hw), lambda b, g: (b, 0, g))
        out_shape = jax.ShapeDtypeStruct((nb, length, d_ssm), BF16)
        scratch = [pltpu.VMEM((length, hw), F32), pltpu.VMEM((length, SSM_STATE), BF16),
                   pltpu.VMEM((length, SSM_STATE), BF16),
                   pltpu.VMEM((nch, t, 8), F32)] + common_scr
    else:
        in_specs = [x_spec, b_spec] + wx_spec + wn_spec + dt_specs + [h_spec]
        args = (main3, main3, cwx, cbx, cwb, cbb, dtr, bias_r, a_r, h0)
        out_specs = h_spec
        out_shape = jax.ShapeDtypeStruct((nb, g_n, 2, SSM_STATE, hw), F32)
        scratch = [pltpu.VMEM((length, hw), F32), pltpu.VMEM((length, SSM_STATE), BF16)] + common_scr
    return pl.pallas_call(
        functools.partial(_ssd_kernel, need_y, length),
        grid=(nb, g_n),
        in_specs=in_specs,
        out_specs=out_specs,
        out_shape=out_shape,
        scratch_shapes=scratch,
        compiler_params=_cparams(("parallel", "parallel")),
        name="ssd_y" if need_y else "ssd_state",
    )(*args)


def _pool_kernel(length, u_ref, pw_ref, ps_ref, o_ref, pad_s):
    gw = GRID_W
    rows = length // gw
    cg = pw_ref.shape[-1]
    halo = (max(POOL_WINDOWS) // 2) * gw
    tile = min(256, length)
    zeros = jnp.zeros((halo, cg), F32)
    pad_s[0:halo, :] = zeros
    pad_s[halo + length:halo + length + halo, :] = zeros
    for g, w in enumerate(POOL_WINDOWS):
        lo = -(w // 2)
        csl = slice(g * cg, (g + 1) * cg)
        for r0 in range(0, length, tile):
            pad_s[halo + r0:halo + r0 + tile, :] = u_ref[0, r0:r0 + tile, csl].astype(F32)
        pw = pw_ref[g]
        scale = ps_ref[:, csl]

        def body(i, carry, lo=lo, w=w, csl=csl, pw=pw, scale=scale):
            r0 = pl.multiple_of(i * tile, tile)
            acc = pad_s[pl.ds(halo + r0 + lo * gw, tile), :]
            for k in range(lo + 1, lo + w):
                acc = acc + pad_s[pl.ds(halo + r0 + k * gw, tile), :]
            l_idx = r0 + lax.broadcasted_iota(I32, (tile, cg), 0)
            ii = l_idx >> (gw.bit_length() - 1)
            jj = l_idx & (gw - 1)
            tot = acc
            for k in range(lo, lo + w):
                if k == 0:
                    continue
                sh = pltpu.roll(acc, (-k) % tile, 0)
                ok = (jj >= -k) if k < 0 else (jj < gw - k)
                tot = tot + jnp.where(ok, sh, 0.0)
            cnt_i = jnp.minimum(ii + lo + w, rows) - jnp.maximum(ii + lo, 0)
            cnt_j = jnp.minimum(jj + lo + w, gw) - jnp.maximum(jj + lo, 0)
            u = pad_s[pl.ds(halo + r0, tile), :]
            d = tot / (cnt_i * cnt_j).astype(F32) - u
            y = _dot(d.astype(BF16), pw) * scale
            o_ref[0, pl.ds(r0, tile), csl] = y.astype(o_ref.dtype)
            return carry

        lax.fori_loop(0, length // tile, body, 0)


def _pool(main3, pool_w, pool_scale, col0):
    nb, length, _ = main3.shape
    ng, cg, _ = pool_w.shape
    dp = ng * cg
    halo = (max(POOL_WINDOWS) // 2) * GRID_W
    return pl.pallas_call(
        functools.partial(_pool_kernel, length),
        grid=(nb,),
        in_specs=[pl.BlockSpec((1, length, dp), lambda b: (b, 0, col0 // dp)),
                  pl.BlockSpec((ng, cg, cg), lambda b: (0, 0, 0)),
                  pl.BlockSpec((1, dp), lambda b: (0, 0))],
        out_specs=pl.BlockSpec((1, length, dp), lambda b: (b, 0, 0)),
        out_shape=jax.ShapeDtypeStruct((nb, length, dp), BF16),
        scratch_shapes=[pltpu.VMEM((length + 2 * halo, cg), F32)],
        compiler_params=_cparams(("parallel",)),
        name="pool",
    )(main3, pool_w.astype(BF16), pool_scale.reshape(1, dp))


def _pack_bf16_pair(lo, hi):
    lo_b = lax.bitcast_convert_type(lo, U32) >> 16
    hi_b = lax.bitcast_convert_type(hi, U32) & jnp.uint32(0xFFFF0000)
    return lo_b | hi_b


def _unpack_bf16_pair(p):
    lo = lax.bitcast_convert_type(p << 16, F32)
    hi = lax.bitcast_convert_type(p & jnp.uint32(0xFFFF0000), F32)
    return lo, hi


def _store_row_tiles(ref, row0, value):
    m = value.shape[0]
    for c in range(SUBLANES):
        ref[pl.ds(row0 * SUBLANES + c, m, stride=SUBLANES), :] = value[:, c * LANES:(c + 1) * LANES]


def _load_row_tiles(ref, row0, m):
    return jnp.concatenate(
        [ref[pl.ds(row0 * SUBLANES + c, m, stride=SUBLANES), :] for c in range(SUBLANES)], axis=1)


def _outproj_kernel(yg_ref, yp_ref, x_ref, gs_ref, m2_ref, m3_ref, m4_ref, m5_ref, gf_ref,
                    ws_ref, wp_ref, wrh_ref, wrl_ref, wg_ref, wu_ref, wd_ref,
                    xs1_ref, h2p_ref, lg_ref):
    yg = yg_ref[...].astype(F32)
    ms = jnp.mean(yg * yg, axis=-1, keepdims=True)
    a = (yg * lax.rsqrt(ms + EPS) * gs_ref[...]).astype(BF16)
    o = _dot(a, ws_ref[...]) + _dot(yp_ref[...], wp_ref[...])
    x1 = x_ref[...] + m2_ref[0] * o
    ms1 = jnp.mean(x1 * x1, axis=-1, keepdims=True)
    h2 = x1 * lax.rsqrt(ms1 + EPS) * gf_ref[...]
    h2 = h2 * (1.0 + m4_ref[0]) + m3_ref[0]
    h_hi, h_lo = _split2(h2)
    wrh = wrh_ref[...]
    lg_ref[...] = _dot_nt(wrh, h_hi) + _dot_nt(wrl_ref[...], h_hi) + _dot_nt(wrh, h_lo)
    act = (_silu(_dot(h_hi, wg_ref[...])) * _dot(h_hi, wu_ref[...])).astype(BF16)
    xs1_ref[...] = x1 + m5_ref[0] * _dot(act, wd_ref[...])
    hf = h_hi.astype(F32)
    half = hf.shape[1] // 2
    _store_row_tiles(h2p_ref, 0, _pack_bf16_pair(hf[:, :half], hf[:, half:]))


def _outproj(yg, yp, x2d, g_ssd, mods, g_ffn, w_ssd, w_pool, wr_hi, wr_lo, wsg, wsu, wsd,
             rows_per_batch, tm):
    n, d = x2d.shape
    dp = yp.shape[1]
    ne = wr_hi.shape[0]
    dsh = wsg.shape[1]
    tpb = rows_per_batch // tm
    row = lambda c: pl.BlockSpec((tm, c), lambda i: (i, 0))
    vec = pl.BlockSpec((1, d), lambda i: (0, 0))
    mod = pl.BlockSpec((1, 1, d), lambda i: (i // tpb, 0, 0))
    res = lambda r, c: pl.BlockSpec((r, c), lambda i: (0, 0), pipeline_mode=pl.Buffered(1))
    m2, m3, m4, m5 = mods
    return pl.pallas_call(
        _outproj_kernel,
        grid=(n // tm,),
        in_specs=[row(d), row(dp), row(d), vec, mod, mod, mod, mod, vec,
                  res(d, d), res(dp, d), res(ne, d), res(ne, d), res(d, dsh), res(d, dsh), res(dsh, d)],
        out_specs=[row(d), pl.BlockSpec((tm * SUBLANES, LANES), lambda i: (i, 0)),
                   pl.BlockSpec((ne, tm), lambda i: (0, i))],
        out_shape=[jax.ShapeDtypeStruct((n, d), F32),
                   jax.ShapeDtypeStruct((n * SUBLANES, LANES), U32),
                   jax.ShapeDtypeStruct((ne, n), F32)],
        compiler_params=_cparams(("parallel",)),
        name="outproj",
    )(yg, yp, x2d, g_ssd.reshape(1, d), m2, m3, m4, m5, g_ffn.reshape(1, d),
      w_ssd, w_pool, wr_hi, wr_lo, wsg, wsu, wsd)


def _topk_kernel(lg_ref, rb_ref, te_ref, tw_ref, cnt_ref, carry):
    ne, tm = lg_ref.shape
    gsz = ne // N_EXPERT_GROUPS

    @pl.when(pl.program_id(0) == 0)
    def _():
        carry[...] = jnp.zeros_like(carry)

    s = _sigmoid(lg_ref[...])
    biased = s + rb_ref[...]
    neg = jnp.float32(-jnp.inf)
    big = jnp.int32(1 << 20)
    gi = lax.broadcasted_iota(I32, (gsz, tm), 0)
    gscore = []
    for g in range(N_EXPERT_GROUPS):
        v = biased[g * gsz:(g + 1) * gsz, :]
        m1 = jnp.max(v, axis=0, keepdims=True)
        i1 = jnp.min(jnp.where(v == m1, gi, big), axis=0, keepdims=True)
        m2 = jnp.max(jnp.where(gi == i1, neg, v), axis=0, keepdims=True)
        gscore.append(m1 + m2)
    parts = []
    for g in range(N_EXPERT_GROUPS):
        rank = jnp.zeros((1, tm), I32)
        for h in range(N_EXPERT_GROUPS):
            if h == g:
                continue
            ahead = (gscore[h] > gscore[g]) | ((gscore[h] == gscore[g]) & (h < g))
            rank = rank + ahead.astype(I32)
        keep = rank < TOPK_GROUPS
        parts.append(jnp.where(keep, biased[g * gsz:(g + 1) * gsz, :], neg))
    masked = jnp.concatenate(parts, axis=0)
    ei = lax.broadcasted_iota(I32, (ne, tm), 0)
    idxs, wts = [], []
    msel = jnp.zeros((ne, tm), F32)
    for _ in range(TOP_K):
        m = jnp.max(masked, axis=0, keepdims=True)
        idx = jnp.min(jnp.where(masked == m, ei, big), axis=0, keepdims=True)
        sel = ei == idx
        wts.append(jnp.sum(jnp.where(sel, s, 0.0), axis=0, keepdims=True))
        idxs.append(idx)
        masked = jnp.where(sel, neg, masked)
        msel = jnp.where(sel, 1.0, msel)
    wsum = wts[0]
    for w in wts[1:]:
        wsum = wsum + w
    for k in range(TOP_K):
        te_ref[k:k + 1, :] = idxs[k]
        tw_ref[k:k + 1, :] = wts[k] / wsum * ROUTE_SCALE
    total = carry[...] + _dot(msel.astype(BF16), jnp.ones((tm, LANES), BF16))
    carry[...] = total
    cnt_ref[...] = total


def _topk(lg_t, router_bias, tm):
    ne, n = lg_t.shape
    row8 = lambda dt: jax.ShapeDtypeStruct((TOP_K, n), dt)
    return pl.pallas_call(
        _topk_kernel,
        grid=(n // tm,),
        in_specs=[pl.BlockSpec((ne, tm), lambda i: (0, i)),
                  pl.BlockSpec((ne, 1), lambda i: (0, 0))],
        out_specs=[pl.BlockSpec((TOP_K, tm), lambda i: (0, i))] * 2
        + [pl.BlockSpec((ne, LANES), lambda i: (0, 0))],
        out_shape=[row8(I32), row8(F32), jax.ShapeDtypeStruct((ne, LANES), F32)],
        scratch_shapes=[pltpu.VMEM((ne, LANES), F32)],
        compiler_params=_cparams(("arbitrary",)),
        name="topk",
    )(lg_t, router_bias.reshape(ne, 1).astype(F32))


def _log2(n):
    assert n & (n - 1) == 0
    return n.bit_length() - 1


def _gather_groups(nvalid):
    return (nvalid + (GATHER_GROUP - 1)) >> _log2(GATHER_GROUP)


def _gather_rows(nvalid):
    return _gather_groups(nvalid) << _log2(GATHER_GROUP)


def _moe_kernel(n_tok, be_ref, nv_ref, a0_ref, tok_ref, tokn_ref, tokp_ref, h_hbm, wg_ref, wu_ref, wd_ref,
                y_hbm, rows, stage, gsem, osem):
    del be_ref
    i = pl.program_id(0)
    nsteps = pl.num_programs(0)
    slot = i % 2
    id_group = SUBLANES
    tile = SUBLANES

    def start_group(tref, a0, g, dst):
        lead = a0 & (id_group - 1)
        w = (a0 & (LANES - 1)) - lead + g * id_group
        for u in range(id_group):
            tok = tref[w + u] & (n_tok - 1)
            dst_row = ROW_PAD - lead + g * id_group + u
            pltpu.make_async_copy(
                h_hbm.at[pl.ds(pl.multiple_of(tok * SUBLANES, SUBLANES), SUBLANES), :],
                rows.at[dst, pl.ds(pl.multiple_of(dst_row * SUBLANES, SUBLANES), SUBLANES), :],
                gsem.at[dst]).start()

    def issue(tref, a0, g_lo, g_hi, dst):
        def body(g, carry):
            start_group(tref, a0, g, dst)
            return carry
        lax.fori_loop(g_lo, g_hi, body, 0)

    def gather_groups(a0, nvalid):
        lead = a0 & (id_group - 1)
        return jnp.where(nvalid > 0, (lead + _gather_rows(nvalid) + id_group - 1) >> _log2(id_group), 0)

    def pieces(nvalid):
        rem = nvalid & (2 * MOE_SUB - 1)
        big = (nvalid >> _log2(2 * MOE_SUB)) + jnp.where(rem > MOE_SUB, 1, 0)
        small = jnp.where((rem > 0) & (rem <= MOE_SUB), 1, 0)
        return big, small

    def covered(nvalid):
        big, small = pieces(nvalid)
        return (2 * big + small) * MOE_SUB_GROUPS

    def out_copy(r, dst_row, src_slot, nrows=1):
        return pltpu.make_async_copy(
            stage.at[src_slot, pl.ds(pl.multiple_of(r * tile, tile), nrows * tile), :],
            y_hbm.at[pl.ds(pl.multiple_of(dst_row * tile, tile), nrows * tile), :],
            osem.at[src_slot])

    def rows_done(nvalid):
        big, small = pieces(nvalid)
        return (2 * big + small) * MOE_SUB

    nv = nv_ref[i]
    a0 = a0_ref[i]
    nxt = jnp.minimum(i + 1, nsteps - 1)
    nv_n = jnp.where(i + 1 < nsteps, nv_ref[nxt], 0)
    a0_n = a0_ref[nxt]
    groups_n = gather_groups(a0_n, nv_n)
    prev = jnp.maximum(i - 1, 0)
    nv_p = jnp.where(i > 0, nv_ref[prev], 0)
    wp0 = a0_ref[prev] & (LANES - 1)
    nv_pp = jnp.where(i > 1, nv_ref[jnp.maximum(i - 2, 0)], 0)
    covered_n = covered(nv)
    spare0 = TOP_K * n_tok

    def start_prev_out(r):
        dst_row = jnp.where(r < nv_p, tokp_ref[wp0 + r], spare0 + r)
        out_copy(r, dst_row, 1 - slot).start()

    @pl.when(i == 0)
    def _():
        rows[...] = jnp.zeros_like(rows)
        stage[...] = jnp.zeros_like(stage)
        spare = pltpu.make_async_copy(stage.at[0], y_hbm.at[pl.ds(spare0 * tile, MOE_BLOCK * tile), :],
                                      osem.at[0])
        spare.start()
        spare.wait()
        issue(tok_ref, a0, 0, gather_groups(a0, nv), 0)

    n_wait = jnp.where(i > 0, jnp.maximum(rows_done(nv_p), nv_pp), 0)

    def wait_out_group(g, carry):
        out_copy(0, 0, slot, id_group).wait()
        return carry
    lax.fori_loop(0, n_wait >> _log2(id_group), wait_out_group, 0)

    def wait_out_row(g, carry):
        out_copy(0, 0, slot).wait()
        return carry
    lax.fori_loop(0, n_wait & (id_group - 1), wait_out_row, 0)

    issue(tokn_ref, a0_n, covered_n, groups_n, 1 - slot)

    covered_here = jnp.where(i > 0, covered(nv_p), 0)

    def wait_body(g, carry):
        pltpu.make_async_copy(h_hbm.at[pl.ds(0, id_group * SUBLANES), :],
                              rows.at[slot, pl.ds(0, id_group * SUBLANES), :], gsem.at[slot]).wait()
        return carry
    lax.fori_loop(0, jnp.maximum(gather_groups(a0, nv), covered_here), wait_body, 0)

    half = wg_ref.shape[1] // 2

    def piece(row0, m, g0):
        ng = (m // MOE_SUB) * MOE_SUB_GROUPS

        def start_copies(k0, k1):
            for k in range(k0, k1):
                start_group(tokn_ref, a0_n, g0 + k, 1 - slot)
                for u in range(id_group):
                    start_prev_out(row0 + k * id_group + u)

        q = ng // 4
        x_lo, x_hi = _unpack_bf16_pair(_load_row_tiles(rows.at[slot], ROW_PAD + row0, m))
        x_lo = x_lo.astype(BF16)
        x_hi = x_hi.astype(BF16)
        start_copies(0, q)
        g = _dot_w(x_lo, wg_ref[0, :half, :])
        start_copies(q, 2 * q)
        g = g + _dot_w(x_hi, wg_ref[0, half:, :])
        start_copies(2 * q, 3 * q)
        u = _dot_w(x_lo, wu_ref[0, :half, :])
        start_copies(3 * q, ng)
        u = u + _dot_w(x_hi, wu_ref[0, half:, :])
        act = (_silu(g) * u).astype(BF16)
        yb = _dot_w(act, wd_ref[0]).astype(BF16).astype(F32)
        _store_row_tiles(stage.at[slot], row0, _pack_bf16_pair(yb[:, :half], yb[:, half:]))

    n_big, n_small = pieces(nv)

    def big_piece(p, carry):
        piece(p * (2 * MOE_SUB), 2 * MOE_SUB, p * (2 * MOE_SUB_GROUPS))
        return carry
    lax.fori_loop(0, n_big, big_piece, 0)

    @pl.when(n_small > 0)
    def _():
        piece(n_big * (2 * MOE_SUB), MOE_SUB, n_big * (2 * MOE_SUB_GROUPS))

    def start_rest(r, carry):
        start_prev_out(r)
        return carry
    lax.fori_loop(rows_done(nv), nv_p, start_rest, 0)


def _moe(block_e, nvalid, a0, sorted_ids, h2p, w_gate, w_up, w_down, n_tok):
    nblk = block_e.shape[0]
    mb = MOE_BLOCK
    ne, d, f = w_gate.shape
    assert d == 2 * SUBLANES * LANES and mb % (2 * MOE_SUB) == 0 and MOE_SUB % GATHER_GROUP == 0
    assert MOE_SUB_GROUPS % 4 == 0 and MOE_SUB_GROUPS * SUBLANES == MOE_SUB
    assert n_tok & (n_tok - 1) == 0, "token id = assignment id & (n_tok - 1)"

    def win(shift):
        def imap(i, be, nv, a0):
            j = jnp.clip(i + shift, 0, nblk - 1)
            return (pl.multiple_of((a0[j] >> _log2(LANES)) << _log2(LANES), LANES),)
        return pl.BlockSpec((pl.Element(TOK_WINDOW),), imap, memory_space=pltpu.SMEM)

    wspec = lambda r, c: pl.BlockSpec((1, r, c), lambda i, be, nv, a0: (be[i], 0, 0))
    grid_spec = pltpu.PrefetchScalarGridSpec(
        num_scalar_prefetch=3,
        grid=(nblk,),
        in_specs=[win(0), win(1), win(-1), pl.BlockSpec(memory_space=pl.ANY), wspec(d, f), wspec(d, f), wspec(f, d)],
        out_specs=pl.BlockSpec(memory_space=pl.ANY),
        scratch_shapes=[pltpu.VMEM((2, (mb + 2 * ROW_PAD) * SUBLANES, LANES), U32),
                        pltpu.VMEM((2, mb * SUBLANES, LANES), U32),
                        pltpu.SemaphoreType.DMA((2,)), pltpu.SemaphoreType.DMA((2,))],
    )
    return pl.pallas_call(
        functools.partial(_moe_kernel, n_tok),
        grid_spec=grid_spec,
        out_shape=jax.ShapeDtypeStruct(((TOP_K * n_tok + mb) * SUBLANES, LANES), U32),
        compiler_params=_cparams(("arbitrary",)),
        name="moe",
    )(block_e, nvalid, a0, sorted_ids, sorted_ids, sorted_ids, h2p, w_gate, w_up, w_down)


def _combine_kernel(*refs):
    y_refs = refs[:TOP_K]
    w_ref, xs1_ref, m5_ref, gf_ref, o_ref = refs[TOP_K:]
    tm = o_ref.shape[0]
    w = w_ref[...]
    acc_lo = acc_hi = None
    for k in range(TOP_K):
        lo, hi = _unpack_bf16_pair(_load_row_tiles(y_refs[k], 0, tm))
        wk = w[:, k:k + 1]
        acc_lo = wk * lo if acc_lo is None else acc_lo + wk * lo
        acc_hi = wk * hi if acc_hi is None else acc_hi + wk * hi
    routed = jnp.concatenate([acc_lo, acc_hi], axis=1)
    x = xs1_ref[...] + m5_ref[0] * routed
    ms = jnp.mean(x * x, axis=-1, keepdims=True)
    o_ref[...] = x * lax.rsqrt(ms + EPS) * gf_ref[...]


def _combine(y_packed, w_tok, xs1, m5, g_final, rows_per_batch, tm):
    n, d = xs1.shape
    assert d == 2 * SUBLANES * LANES
    nt = n // tm
    tpb = rows_per_batch // tm
    y_spec = lambda k: pl.BlockSpec((tm * SUBLANES, LANES), lambda i: (k * nt + i, 0))
    return pl.pallas_call(
        _combine_kernel,
        grid=(nt,),
        in_specs=[y_spec(k) for k in range(TOP_K)]
        + [pl.BlockSpec((tm, TOP_K), lambda i: (i, 0)),
           pl.BlockSpec((tm, d), lambda i: (i, 0)),
           pl.BlockSpec((1, 1, d), lambda i: (i // tpb, 0, 0)),
           pl.BlockSpec((1, d), lambda i: (0, 0))],
        out_specs=pl.BlockSpec((tm, d), lambda i: (i, 0)),
        out_shape=jax.ShapeDtypeStruct((n, d), F32),
        compiler_params=_cparams(("parallel",)),
        name="combine",
    )(*([y_packed] * TOP_K), w_tok, xs1, m5, g_final.reshape(1, d))


def _dispatch_plan(top_e, counts, n_tok):
    mb = MOE_BLOCK
    ne = counts.shape[0]
    n_asg = TOP_K * n_tok
    nblk = -(-(n_asg + ne * (mb - 1)) // mb) + 2
    asg_ids = jnp.arange(n_asg, dtype=I32).reshape(top_e.shape)
    keys = jnp.sort((top_e * n_asg + asg_ids).reshape(-1))
    sorted_ids = jnp.concatenate([keys % n_asg, jnp.zeros((TOK_WINDOW,), I32)])
    start = jnp.cumsum(counts) - counts
    eblk = (counts + mb - 1) // mb
    pend = jnp.cumsum(eblk)
    pstart = pend - eblk
    n_real = pend[-1]
    bid = jnp.arange(nblk, dtype=I32)
    last_real = jnp.maximum(n_real - 1, 0)
    bsrc = jnp.minimum(bid, last_real)
    block_e = jnp.minimum(jnp.sum((pend[None, :] <= bsrc[:, None]).astype(I32), axis=1), ne - 1)
    onehot = block_e[:, None] == jnp.arange(ne, dtype=I32)[None, :]
    look = lambda table: jnp.sum(jnp.where(onehot, table[None, :], 0), axis=1)
    off = (bsrc - look(pstart)) * mb
    nvalid = jnp.where(bid < n_real, jnp.clip(look(counts) - off, 0, mb), 0).astype(I32)
    a0 = (look(start) + off).astype(I32)
    return block_e.astype(I32), nvalid, a0, sorted_ids


def kernel(x, c, ctx, c_ctx, w_ada, b_ada, g_mix, w_in, conv_w, conv_b, dt_bias, a_log, d_skip, g_ssd,
           pool_w, pool_scale, w_out, g_ffn, w_router, router_bias, w_exp_gate, w_exp_up, w_exp_down,
           w_sh_gate, w_sh_up, w_sh_down, g_final):
    bsz, seq, d = x.shape
    ctx_len = ctx.shape[1]
    assert w_ada.shape[0] == 1, "single-layer block"
    d_ssm = g_ssd.shape[1]
    heads = d_skip.shape[1]
    d_pool = pool_scale.shape[1]
    d_xbc = conv_w.shape[2]
    n = bsz * seq

    cvec = jnp.zeros((8, d), F32).at[:bsz].set(c).at[bsz].set(c_ctx)
    mod_all = _ada(cvec, w_ada[0], b_ada[0]).reshape(8, N_MOD, d)
    mod = mod_all[:bsz]
    mod_c = mod_all[bsz:bsz + 1]
    mk = lambda m, k: m[:, k:k + 1, :]

    wt = jnp.transpose(w_in[0])
    c_dt = d_ssm + d_xbc
    wtb = wt.astype(BF16)
    w_pl = wtb[c_dt + 2 * heads:]
    w_dt = wt[c_dt:c_dt + 2 * heads].reshape(2, SSM_GROUPS, heads // SSM_GROUPS, d)
    w_dt = jnp.transpose(w_dt, (1, 0, 2, 3)).reshape(2 * heads, d)
    w_dt = jnp.pad(w_dt, ((0, LANES - 2 * heads), (0, 0)))

    main_c, dt_c = _inproj(ctx.reshape(bsz * ctx_len, d), g_mix[0], mk(mod_c, 0), mk(mod_c, 1),
                           wtb, c_dt, w_pl, w_dt, bsz * ctx_len, tm=256, tn=1024)
    h_zero = jnp.zeros((bsz, SSM_GROUPS, 2, SSM_STATE, 4 * SSM_HEADDIM), F32)
    h_ctx = _ssd(main_c.reshape(bsz, ctx_len, -1), dt_c, conv_w[0], conv_b[0],
                 dt_bias[0], a_log[0], None, h_zero, False, d_ssm, d_ssm)

    x2d = x.reshape(n, d)
    main, dt_raw = _inproj(x2d, g_mix[0], mk(mod, 0), mk(mod, 1), wtb, c_dt, w_pl, w_dt, seq,
                           tm=min(1024, seq), tn=1024)
    main3 = main.reshape(bsz, seq, -1)
    dsk = jnp.repeat(d_skip[0].astype(F32), SSM_HEADDIM).reshape(1, d_ssm)
    yg = _ssd(main3, dt_raw, conv_w[0], conv_b[0], dt_bias[0], a_log[0], dsk, h_ctx, True, d_ssm, d_ssm)
    yp = _pool(main3, pool_w[0], pool_scale[0], d_ssm + d_xbc)

    wo = w_out[0].astype(BF16)
    wr = w_router[0].T
    wr_hi = wr.astype(BF16)
    wr_lo = (wr - wr_hi.astype(F32)).astype(BF16)
    xs1, h2p, lg_t = _outproj(
        yg.reshape(n, d_ssm), yp.reshape(n, d_pool), x2d, g_ssd[0],
        (mk(mod, 2), mk(mod, 3), mk(mod, 4), mk(mod, 5)), g_ffn[0],
        wo[:d_ssm], wo[d_ssm:], wr_hi, wr_lo,
        w_sh_gate[0].astype(BF16), w_sh_up[0].astype(BF16), w_sh_down[0].astype(BF16), seq, tm=256)

    top_e, top_w, cnt = _topk(lg_t, router_bias[0], tm=512)
    counts = cnt[:, 0].astype(I32)
    block_e, nvalid, a0, sorted_ids = _dispatch_plan(top_e, counts, n)
    y_packed = _moe(block_e, nvalid, a0, sorted_ids, h2p, w_exp_gate[0], w_exp_up[0], w_exp_down[0], n)
    out = _combine(y_packed, top_w.T, xs1, mk(mod, 5), g_final, seq, tm=256)
    return out.reshape(bsz, seq, d)
```

```python
import functools

import jax
import jax.numpy as jnp
from jax import lax
from jax.experimental import pallas as pl
from jax.experimental.pallas import tpu as pltpu

F32 = jnp.float32
BF16 = jnp.bfloat16
I32 = jnp.int32
U32 = jnp.uint32

EPS = 1e-6
GRID_W = 64
SSM_HEADDIM = 64
SSM_GROUPS = 8
SSM_STATE = 128
SSD_CHUNK = 128
POOL_WINDOWS = (2, 4, 8, 16)
TOP_K = 8
N_EXPERT_GROUPS = 8
TOPK_GROUPS = 4
ROUTE_SCALE = 2.5
N_MOD = 6
LANES = 128
SUBLANES = 8
BF16_ROWS = 16
CONV_TILE = 128
MOE_BLOCK = 1024
GATHER_GROUP = 64
MOE_SUB = 128
MOE_SUB_GROUPS = 16
ROW_PAD = SUBLANES
TOK_WINDOW = pl.next_power_of_2(LANES + MOE_BLOCK + 2 * ROW_PAD)
V7X_VMEM_LIMIT = 56 * 1024 * 1024


def _cparams(sem, vmem=V7X_VMEM_LIMIT):
    return pltpu.CompilerParams(dimension_semantics=sem, vmem_limit_bytes=vmem)


def _sigmoid(x):
    return 1.0 / (1.0 + jnp.exp(-x))


def _silu(x):
    return x * _sigmoid(x)


def _split2(x):
    hi = x.astype(BF16)
    lo = (x - hi.astype(F32)).astype(BF16)
    return hi, lo


def _dot(a, b):
    return jnp.dot(a, b, preferred_element_type=F32)


def _dot_nt(a, b):
    return lax.dot_general(a, b, (((1,), (1,)), ((), ())), preferred_element_type=F32)


def _dot_tn(a, b):
    return lax.dot_general(a, b, (((0,), (0,)), ((), ())), preferred_element_type=F32)


def _dot_w(a, w):
    return lax.dot_general(a, w, (((1,), (0,)), ((), ())), preferred_element_type=F32)


def _dot3(a, b):
    a_hi, a_lo = _split2(a)
    b_hi, b_lo = _split2(b)
    return _dot(a_hi, b_hi) + _dot(a_lo, b_hi) + _dot(a_hi, b_lo)


def _ada_kernel(c_ref, w_ref, b_ref, o_ref):
    c = c_ref[...]
    o_ref[...] = _dot3(_silu(c), w_ref[...]) + b_ref[...]


def _ada(cvec, w_ada, b_ada, tn=1024):
    d, n = w_ada.shape
    return pl.pallas_call(
        _ada_kernel,
        grid=(n // tn,),
        in_specs=[pl.BlockSpec((8, d), lambda j: (0, 0)),
                  pl.BlockSpec((d, tn), lambda j: (0, j)),
                  pl.BlockSpec((1, tn), lambda j: (0, j))],
        out_specs=pl.BlockSpec((8, tn), lambda j: (0, j)),
        out_shape=jax.ShapeDtypeStruct((8, n), F32),
        compiler_params=_cparams(("parallel",)),
        name="ada",
    )(cvec, w_ada, b_ada.reshape(1, n))


def _inproj_kernel(na, x_ref, g_ref, sh_ref, sc_ref, wa_ref, wb_ref, wdt_ref, o_ref, dt_ref, h_scr):
    j = pl.program_id(1)

    @pl.when(j == 0)
    def _():
        x = x_ref[...]
        ms = jnp.mean(x * x, axis=-1, keepdims=True)
        y = x * lax.rsqrt(ms + EPS) * g_ref[...]
        h = y * (1.0 + sc_ref[0]) + sh_ref[0]
        h_hi, h_lo = _split2(h)
        h_scr[...] = h_hi
        w_hi, w_lo = _split2(wdt_ref[...])
        dt_ref[...] = _dot_nt(w_hi, h_hi) + _dot_nt(w_lo, h_hi) + _dot_nt(w_hi, h_lo)

    @pl.when(j < na)
    def _():
        o_ref[...] = _dot_nt(h_scr[...], wa_ref[...]).astype(o_ref.dtype)

    @pl.when(j >= na)
    def _():
        o_ref[...] = _dot_nt(h_scr[...], wb_ref[...]).astype(o_ref.dtype)


def _inproj(x2d, g, shift, scale, w_a_t, ca, w_b_t, w_dt_t, rows_per_batch, tm, tn):
    n, d = x2d.shape
    na, nbk = ca // tn, w_b_t.shape[0] // tn
    assert ca % tn == 0 and w_b_t.shape[0] % tn == 0
    nc = (na + nbk) * tn
    tpb = rows_per_batch // tm
    return pl.pallas_call(
        functools.partial(_inproj_kernel, na),
        grid=(n // tm, nc // tn),
        in_specs=[pl.BlockSpec((tm, d), lambda i, j: (i, 0)),
                  pl.BlockSpec((1, d), lambda i, j: (0, 0)),
                  pl.BlockSpec((1, 1, d), lambda i, j: (i // tpb, 0, 0)),
                  pl.BlockSpec((1, 1, d), lambda i, j: (i // tpb, 0, 0)),
                  pl.BlockSpec((tn, d), lambda i, j: (jnp.minimum(j, na - 1), 0)),
                  pl.BlockSpec((tn, d), lambda i, j: (jnp.maximum(j - na, 0), 0)),
                  pl.BlockSpec((LANES, d), lambda i, j: (0, 0))],
        out_specs=[pl.BlockSpec((tm, tn), lambda i, j: (i, j)),
                   pl.BlockSpec((LANES, tm), lambda i, j: (0, i))],
        out_shape=[jax.ShapeDtypeStruct((n, nc), BF16),
                   jax.ShapeDtypeStruct((LANES, n), F32)],
        scratch_shapes=[pltpu.VMEM((tm, d), BF16)],
        compiler_params=_cparams(("parallel", "arbitrary")),
        name="inproj",
    )(x2d, g.reshape(1, d), shift, scale, w_a_t, w_b_t, w_dt_t)


def _expand_heads(v, base, width):
    t = v.shape[0]
    lane = lax.broadcasted_iota(I32, (t, 4 * width), 1)
    out = jnp.broadcast_to(v[:, base + 3:base + 4], (t, 4 * width))
    for j in (2, 1, 0):
        out = jnp.where(lane < (j + 1) * width, v[:, base + j:base + j + 1], out)
    return out


def _conv_silu(src_ref, w_ref, b_ref, dst_ref, length):
    c = src_ref.shape[-1]
    tile, grp = CONV_TILE, BF16_ROWS
    w = w_ref[...]
    b = b_ref[...]
    rid = lax.broadcasted_iota(I32, (tile, c), 0)

    def body(k, carry):
        r0 = pl.multiple_of(k * tile, tile)
        cur = src_ref[0, pl.ds(r0, tile), :].astype(F32)
        lo = pl.multiple_of(jnp.maximum(r0 - grp, 0), grp)
        hi = pl.multiple_of(jnp.minimum(r0 + tile, length - grp), grp)
        prev_row = src_ref[0, pl.ds(lo, grp), :].astype(F32)[grp - 1:grp, :]
        next_row = src_ref[0, pl.ds(hi, grp), :].astype(F32)[0:1, :]
        prev_row = jnp.where(r0 > 0, prev_row, 0.0)
        next_row = jnp.where(r0 + tile < length, next_row, 0.0)
        up = jnp.where(rid == 0, prev_row, pltpu.roll(cur, 1, 0))
        dn = jnp.where(rid == tile - 1, next_row, pltpu.roll(cur, tile - 1, 0))
        o = up * w[0:1, :] + cur * w[1:2, :] + dn * w[2:3, :] + b
        dst_ref[pl.ds(r0, tile), :] = _silu(o).astype(dst_ref.dtype)
        return carry
    lax.fori_loop(0, length // tile, body, 0, unroll=2)


def _softplus(x):
    return jnp.maximum(x, 0.0) + jnp.log(1.0 + jnp.exp(-jnp.abs(x)))


def _ssd_kernel(need_y, length, *refs):
    t = SSD_CHUNK
    nch = length // t
    hw = 4 * SSM_HEADDIM
    if need_y:
        (xr, br, cr, zr, cwx, cbx, cwb, cbb, cwc, cbc, dtr, bias_r, a_r, dsk, h0,
         y_ref, xs_s, b_s, c_s, e_s, dtr_s, csr_s, csc_s, st_s, dec_s) = refs
    else:
        (xr, br, cwx, cbx, cwb, cbb, dtr, bias_r, a_r, h0,
         hfin, xs_s, b_s, dtr_s, csr_s, csc_s, st_s, dec_s) = refs

    _conv_silu(xr, cwx, cbx, xs_s, length)
    _conv_silu(br, cwb, cbb, b_s, length)
    if need_y:
        _conv_silu(cr, cwc, cbc, c_s, length)

    dt_r = _softplus(dtr[0, 0] + bias_r[0])
    dtr_s[...] = dt_r
    da2 = (dt_r * a_r[0]).reshape(nch * 8, t)
    kk = lax.broadcasted_iota(I32, (t, 2 * t), 0)
    ll = lax.broadcasted_iota(I32, (t, 2 * t), 1)
    tri = jnp.where(ll < t, jnp.where(kk <= ll, 1.0, 0.0), jnp.where(kk >= ll - t, 1.0, 0.0)).astype(BF16)
    p0 = da2.astype(BF16)
    r1 = da2 - p0.astype(F32)
    p1 = r1.astype(BF16)
    p2 = (r1 - p1.astype(F32)).astype(BF16)
    cum = _dot(p0, tri) + _dot(p1, tri) + _dot(p2, tri)
    rowj = lax.broadcasted_iota(I32, (nch * 8, t), 0) & 7
    csr = jnp.where(rowj < 4, cum[:, :t], cum[:, t:])
    csr_s[...] = csr.reshape(nch, 8, t)

    fwd_col = lax.broadcasted_iota(I32, (1, 8), 1) < 4

    def phase_a(c, carry):
        r0 = pl.multiple_of(c * t, t)
        rows16 = jnp.concatenate([csr_s[c], dtr_s[c], jnp.zeros((t - 16, t), F32)], axis=0)
        cols = rows16.T
        csc = cols[:, 0:8]
        csc_s[c] = csc
        if need_y:
            e_s[c] = jnp.exp(csc)
        edge = jnp.where(fwd_col, csc[t - 1:t, :], csc[0:1, :])
        wcol = cols[:, 8:16] * jnp.exp(edge - csc)
        dec = jnp.exp(edge)
        xs = xs_s[pl.ds(r0, t), :]
        bc = b_s[pl.ds(r0, t), :]
        xw_f = (xs * _expand_heads(wcol, 0, SSM_HEADDIM)).astype(BF16)
        xw_b = (xs * _expand_heads(wcol, 4, SSM_HEADDIM)).astype(BF16)
        st_s[c, 0] = _dot_tn(bc, xw_f)
        st_s[c, 1] = _dot_tn(bc, xw_b)
        dec_s[c, 0:1, :] = _expand_heads(dec, 0, SSM_HEADDIM)
        dec_s[c, 1:2, :] = _expand_heads(dec, 4, SSM_HEADDIM)
        return carry

    lax.fori_loop(0, nch, phase_a, 0, unroll=2)

    def rec_f(c, s):
        loc = st_s[c, 0]
        st_s[c, 0] = s
        return dec_s[c, 0:1, :] * s + loc

    def rec_b(k, s):
        c = nch - 1 - k
        loc = st_s[c, 1]
        st_s[c, 1] = s
        return dec_s[c, 1:2, :] * s + loc

    s_f = lax.fori_loop(0, nch, rec_f, h0[0, 0, 0])
    s_b = lax.fori_loop(0, nch, rec_b, h0[0, 0, 1])
    if not need_y:
        hfin[0, 0, 0] = s_f
        hfin[0, 0, 1] = s_b
        return

    li = lax.broadcasted_iota(I32, (t, t), 0)
    si = lax.broadcasted_iota(I32, (t, t), 1)
    dskip = dsk[...]

    def phase_c(c, carry):
        r0 = pl.multiple_of(c * t, t)
        csr_c = csr_s[c]
        csc_c = csc_s[c]
        dtr_c = dtr_s[c]
        e_c = e_s[c]
        xs = xs_s[pl.ds(r0, t), :]
        xs_b = xs.astype(BF16)
        bc = b_s[pl.ds(r0, t), :]
        cc = c_s[pl.ds(r0, t), :]
        cb = _dot_nt(cc, bc)
        ys = []
        for j in range(4):
            d_f = csc_c[:, j:j + 1] - csr_c[j:j + 1, :]
            att_f = jnp.where(si <= li, (cb * dtr_c[j:j + 1, :]) * jnp.exp(jnp.minimum(d_f, 0.0)), 0.0)
            d_b = csc_c[:, 4 + j:5 + j] - csr_c[4 + j:5 + j, :]
            att_b = jnp.where(si >= li, (cb * dtr_c[4 + j:5 + j, :]) * jnp.exp(jnp.minimum(d_b, 0.0)), 0.0)
            lhs = jnp.concatenate([att_f, att_b], axis=1).astype(BF16)
            xj = xs_b[:, j * SSM_HEADDIM:(j + 1) * SSM_HEADDIM]
            ys.append(_dot(lhs, jnp.concatenate([xj, xj], axis=0)))
        y = jnp.concatenate(ys, axis=1)
        y = y + _dot(cc, st_s[c, 0].astype(BF16)) * _expand_heads(e_c, 0, SSM_HEADDIM)
        y = y + _dot(cc, st_s[c, 1].astype(BF16)) * _expand_heads(e_c, 4, SSM_HEADDIM)
        y = y + xs * dskip
        z = zr[0, pl.ds(r0, t), :].astype(F32)
        y_ref[0, pl.ds(r0, t), :] = (y * _silu(z)).astype(y_ref.dtype)
        return carry

    lax.fori_loop(0, nch, phase_c, 0, unroll=2)


def _ssd(main3, dt_t, conv_w, conv_b, dt_bias, a_log, d_skip, h0, need_y, d_ssm, xbc_col0):
    nb, length, _ = main3.shape
    g_n, t = SSM_GROUPS, SSD_CHUNK
    nch = length // t
    heads = d_ssm // SSM_HEADDIM
    hpg = heads // g_n
    assert hpg == 4 and SSM_STATE == 128
    hw = hpg * SSM_HEADDIM
    dtr = jnp.transpose(dt_t[:2 * heads].reshape(g_n, 2 * hpg, nb, nch, t), (2, 0, 3, 1, 4))
    par = lambda p: jnp.transpose(p.reshape(2, g_n, hpg), (1, 0, 2)).reshape(g_n, 8)
    bias_r = par(dt_bias.astype(F32)).reshape(g_n, 8, 1)
    a_r = par(-jnp.exp(a_log.astype(F32))).reshape(g_n, 8, 1)
    nx = d_ssm
    nbc = g_n * SSM_STATE
    cwx, cwb, cwc = conv_w[:, :nx], conv_w[:, nx:nx + nbc], conv_w[:, nx + nbc:]
    cb2 = conv_b.reshape(1, -1)
    cbx, cbb, cbc = cb2[:, :nx], cb2[:, nx:nx + nbc], cb2[:, nx + nbc:]
    xcol = xbc_col0 // hw
    bcol = (xbc_col0 + nx) // SSM_STATE
    ccol = (xbc_col0 + nx + nbc) // SSM_STATE

    seq = lambda shape, imap: pl.BlockSpec(shape, imap)
    x_spec = seq((1, length, hw), lambda b, g: (b, 0, xcol + g))
    b_spec = seq((1, length, SSM_STATE), lambda b, g: (b, 0, bcol + g))
    c_spec = seq((1, length, SSM_STATE), lambda b, g: (b, 0, ccol + g))
    z_spec = seq((1, length, hw), lambda b, g: (b, 0, g))
    wx_spec = [seq((3, hw), lambda b, g: (0, g)), seq((1, hw), lambda b, g: (0, g))]
    wn_spec = [seq((3, SSM_STATE), lambda b, g: (0, g)), seq((1, SSM_STATE), lambda b, g: (0, g))]
    dt_specs = [seq((1, 1, nch, 8, t), lambda b, g: (b, g, 0, 0, 0)),
                seq((1, 8, 1), lambda b, g: (g, 0, 0)), seq((1, 8, 1), lambda b, g: (g, 0, 0))]
    h_spec = seq((1, 1, 2, SSM_STATE, hw), lambda b, g: (b, g, 0, 0, 0))
    common_scr = [pltpu.VMEM((nch, 8, t), F32),
                  pltpu.VMEM((nch, 8, t), F32),
                  pltpu.VMEM((nch, t, 8), F32),
                  pltpu.VMEM((nch, 2, SSM_STATE, hw), F32),
                  pltpu.VMEM((nch, 8, hw), F32)]
    if need_y:
        in_specs = ([x_spec, b_spec, c_spec, z_spec] + wx_spec + wn_spec + wn_spec + dt_specs
                    + [seq((1, hw), lambda b, g: (0, g)), h_spec])
        args = (main3, main3, main3, main3, cwx, cbx, cwb, cbb, cwc, cbc, dtr, bias_r, a_r, d_skip, h0)
        out_specs = seq((1, length, hw), lambda b, g: (b, 0, g))
        out_shape = jax.ShapeDtypeStruct((nb, length, d_ssm), BF16)
        scratch = [pltpu.VMEM((length, hw), F32), pltpu.VMEM((length, SSM_STATE), BF16),
                   pltpu.VMEM((length, SSM_STATE), BF16),
                   pltpu.VMEM((nch, t, 8), F32)] + common_scr
    else:
        in_specs = [x_spec, b_spec] + wx_spec + wn_spec + dt_specs + [h_spec]
        args = (main3, main3, cwx, cbx, cwb, cbb, dtr, bias_r, a_r, h0)
        out_specs = h_spec
        out_shape = jax.ShapeDtypeStruct((nb, g_n, 2, SSM_STATE, hw), F32)
        scratch = [pltpu.VMEM((length, hw), F32), pltpu.VMEM((length, SSM_STATE), BF16)] + common_scr
    return pl.pallas_call(
        functools.partial(_ssd_kernel, need_y, length),
        grid=(nb, g_n),
        in_specs=in_specs,
        out_specs=out_specs,
        out_shape=out_shape,
        scratch_shapes=scratch,
        compiler_params=_cparams(("parallel", "parallel")),
        name="ssd_y" if need_y else "ssd_state",
    )(*args)


def _pool_kernel(length, u_ref, pw_ref, ps_ref, o_ref, pad_s):
    gw = GRID_W
    rows = length // gw
    cg = pw_ref.shape[-1]
    halo = (max(POOL_WINDOWS) // 2) * gw
    tile = min(256, length)
    zeros = jnp.zeros((halo, cg), F32)
    pad_s[0:halo, :] = zeros
    pad_s[halo + length:halo + length + halo, :] = zeros
    for g, w in enumerate(POOL_WINDOWS):
        lo = -(w // 2)
        csl = slice(g * cg, (g + 1) * cg)
        for r0 in range(0, length, tile):
            pad_s[halo + r0:halo + r0 + tile, :] = u_ref[0, r0:r0 + tile, csl].astype(F32)
        pw = pw_ref[g]
        scale = ps_ref[:, csl]

        def body(i, carry, lo=lo, w=w, csl=csl, pw=pw, scale=scale):
            r0 = pl.multiple_of(i * tile, tile)
            acc = pad_s[pl.ds(halo + r0 + lo * gw, tile), :]
            for k in range(lo + 1, lo + w):
                acc = acc + pad_s[pl.ds(halo + r0 + k * gw, tile), :]
            l_idx = r0 + lax.broadcasted_iota(I32, (tile, cg), 0)
            ii = l_idx >> (gw.bit_length() - 1)
            jj = l_idx & (gw - 1)
            tot = acc
            for k in range(lo, lo + w):
                if k == 0:
                    continue
                sh = pltpu.roll(acc, (-k) % tile, 0)
                ok = (jj >= -k) if k < 0 else (jj < gw - k)
                tot = tot + jnp.where(ok, sh, 0.0)
            cnt_i = jnp.minimum(ii + lo + w, rows) - jnp.maximum(ii + lo, 0)
            cnt_j = jnp.minimum(jj + lo + w, gw) - jnp.maximum(jj + lo, 0)
            u = pad_s[pl.ds(halo + r0, tile), :]
            d = tot / (cnt_i * cnt_j).astype(F32) - u
            y = _dot(d.astype(BF16), pw) * scale
            o_ref[0, pl.ds(r0, tile), csl] = y.astype(o_ref.dtype)
            return carry

        lax.fori_loop(0, length // tile, body, 0)


def _pool(main3, pool_w, pool_scale, col0):
    nb, length, _ = main3.shape
    ng, cg, _ = pool_w.shape
    dp = ng * cg
    halo = (max(POOL_WINDOWS) // 2) * GRID_W
    return pl.pallas_call(
        functools.partial(_pool_kernel, length),
        grid=(nb,),
        in_specs=[pl.BlockSpec((1, length, dp), lambda b: (b, 0, col0 // dp)),
                  pl.BlockSpec((ng, cg, cg), lambda b: (0, 0, 0)),
                  pl.BlockSpec((1, dp), lambda b: (0, 0))],
        out_specs=pl.BlockSpec((1, length, dp), lambda b: (b, 0, 0)),
        out_shape=jax.ShapeDtypeStruct((nb, length, dp), BF16),
        scratch_shapes=[pltpu.VMEM((length + 2 * halo, cg), F32)],
        compiler_params=_cparams(("parallel",)),
        name="pool",
    )(main3, pool_w.astype(BF16), pool_scale.reshape(1, dp))


def _pack_bf16_pair(lo, hi):
    lo_b = lax.bitcast_convert_type(lo, U32) >> 16
    hi_b = lax.bitcast_convert_type(hi, U32) & jnp.uint32(0xFFFF0000)
    return lo_b | hi_b


def _unpack_bf16_pair(p):
    lo = lax.bitcast_convert_type(p << 16, F32)
    hi = lax.bitcast_convert_type(p & jnp.uint32(0xFFFF0000), F32)
    return lo, hi


def _store_row_tiles(ref, row0, value):
    m = value.shape[0]
    for c in range(SUBLANES):
        ref[pl.ds(row0 * SUBLANES + c, m, stride=SUBLANES), :] = value[:, c * LANES:(c + 1) * LANES]


def _load_row_tiles(ref, row0, m):
    return jnp.concatenate(
        [ref[pl.ds(row0 * SUBLANES + c, m, stride=SUBLANES), :] for c in range(SUBLANES)], axis=1)


def _outproj_kernel(yg_ref, yp_ref, x_ref, gs_ref, m2_ref, m3_ref, m4_ref, m5_ref, gf_ref,
                    ws_ref, wp_ref, wrh_ref, wrl_ref, wg_ref, wu_ref, wd_ref,
                    xs1_ref, h2p_ref, lg_ref):
    yg = yg_ref[...].astype(F32)
    ms = jnp.mean(yg * yg, axis=-1, keepdims=True)
    a = (yg * lax.rsqrt(ms + EPS) * gs_ref[...]).astype(BF16)
    o = _dot(a, ws_ref[...]) + _dot(yp_ref[...], wp_ref[...])
    x1 = x_ref[...] + m2_ref[0] * o
    ms1 = jnp.mean(x1 * x1, axis=-1, keepdims=True)
    h2 = x1 * lax.rsqrt(ms1 + EPS) * gf_ref[...]
    h2 = h2 * (1.0 + m4_ref[0]) + m3_ref[0]
    h_hi, h_lo = _split2(h2)
    wrh = wrh_ref[...]
    lg_ref[...] = _dot_nt(wrh, h_hi) + _dot_nt(wrl_ref[...], h_hi) + _dot_nt(wrh, h_lo)
    act = (_silu(_dot(h_hi, wg_ref[...])) * _dot(h_hi, wu_ref[...])).astype(BF16)
    xs1_ref[...] = x1 + m5_ref[0] * _dot(act, wd_ref[...])
    hf = h_hi.astype(F32)
    half = hf.shape[1] // 2
    _store_row_tiles(h2p_ref, 0, _pack_bf16_pair(hf[:, :half], hf[:, half:]))


def _outproj(yg, yp, x2d, g_ssd, mods, g_ffn, w_ssd, w_pool, wr_hi, wr_lo, wsg, wsu, wsd,
             rows_per_batch, tm):
    n, d = x2d.shape
    dp = yp.shape[1]
    ne = wr_hi.shape[0]
    dsh = wsg.shape[1]
    tpb = rows_per_batch // tm
    row = lambda c: pl.BlockSpec((tm, c), lambda i: (i, 0))
    vec = pl.BlockSpec((1, d), lambda i: (0, 0))
    mod = pl.BlockSpec((1, 1, d), lambda i: (i // tpb, 0, 0))
    res = lambda r, c: pl.BlockSpec((r, c), lambda i: (0, 0), pipeline_mode=pl.Buffered(1))
    m2, m3, m4, m5 = mods
    return pl.pallas_call(
        _outproj_kernel,
        grid=(n // tm,),
        in_specs=[row(d), row(dp), row(d), vec, mod, mod, mod, mod, vec,
                  res(d, d), res(dp, d), res(ne, d), res(ne, d), res(d, dsh), res(d, dsh), res(dsh, d)],
        out_specs=[row(d), pl.BlockSpec((tm * SUBLANES, LANES), lambda i: (i, 0)),
                   pl.BlockSpec((ne, tm), lambda i: (0, i))],
        out_shape=[jax.ShapeDtypeStruct((n, d), F32),
                   jax.ShapeDtypeStruct((n * SUBLANES, LANES), U32),
                   jax.ShapeDtypeStruct((ne, n), F32)],
        compiler_params=_cparams(("parallel",)),
        name="outproj",
    )(yg, yp, x2d, g_ssd.reshape(1, d), m2, m3, m4, m5, g_ffn.reshape(1, d),
      w_ssd, w_pool, wr_hi, wr_lo, wsg, wsu, wsd)


def _topk_kernel(lg_ref, rb_ref, te_ref, tw_ref, cnt_ref, carry):
    ne, tm = lg_ref.shape
    gsz = ne // N_EXPERT_GROUPS

    @pl.when(pl.program_id(0) == 0)
    def _():
        carry[...] = jnp.zeros_like(carry)

    s = _sigmoid(lg_ref[...])
    biased = s + rb_ref[...]
    neg = jnp.float32(-jnp.inf)
    big = jnp.int32(1 << 20)
    gi = lax.broadcasted_iota(I32, (gsz, tm), 0)
    gscore = []
    for g in range(N_EXPERT_GROUPS):
        v = biased[g * gsz:(g + 1) * gsz, :]
        m1 = jnp.max(v, axis=0, keepdims=True)
        i1 = jnp.min(jnp.where(v == m1, gi, big), axis=0, keepdims=True)
        m2 = jnp.max(jnp.where(gi == i1, neg, v), axis=0, keepdims=True)
        gscore.append(m1 + m2)
    parts = []
    for g in range(N_EXPERT_GROUPS):
        rank = jnp.zeros((1, tm), I32)
        for h in range(N_EXPERT_GROUPS):
            if h == g:
                continue
            ahead = (gscore[h] > gscore[g]) | ((gscore[h] == gscore[g]) & (h < g))
            rank = rank + ahead.astype(I32)
        keep = rank < TOPK_GROUPS
        parts.append(jnp.where(keep, biased[g * gsz:(g + 1) * gsz, :], neg))
    masked = jnp.concatenate(parts, axis=0)
    ei = lax.broadcasted_iota(I32, (ne, tm), 0)
    idxs, wts = [], []
    msel = jnp.zeros((ne, tm), F32)
    for _ in range(TOP_K):
        m = jnp.max(masked, axis=0, keepdims=True)
        idx = jnp.min(jnp.where(masked == m, ei, big), axis=0, keepdims=True)
        sel = ei == idx
        wts.append(jnp.sum(jnp.where(sel, s, 0.0), axis=0, keepdims=True))
        idxs.append(idx)
        masked = jnp.where(sel, neg, masked)
        msel = jnp.where(sel, 1.0, msel)
    wsum = wts[0]
    for w in wts[1:]:
        wsum = wsum + w
    for k in range(TOP_K):
        te_ref[k:k + 1, :] = idxs[k]
        tw_ref[k:k + 1, :] = wts[k] / wsum * ROUTE_SCALE
    total = carry[...] + _dot(msel.astype(BF16), jnp.ones((tm, LANES), BF16))
    carry[...] = total
    cnt_ref[...] = total


def _topk(lg_t, router_bias, tm):
    ne, n = lg_t.shape
    row8 = lambda dt: jax.ShapeDtypeStruct((TOP_K, n), dt)
    return pl.pallas_call(
        _topk_kernel,
        grid=(n // tm,),
        in_specs=[pl.BlockSpec((ne, tm), lambda i: (0, i)),
                  pl.BlockSpec((ne, 1), lambda i: (0, 0))],
        out_specs=[pl.BlockSpec((TOP_K, tm), lambda i: (0, i))] * 2
        + [pl.BlockSpec((ne, LANES), lambda i: (0, 0))],
        out_shape=[row8(I32), row8(F32), jax.ShapeDtypeStruct((ne, LANES), F32)],
        scratch_shapes=[pltpu.VMEM((ne, LANES), F32)],
        compiler_params=_cparams(("arbitrary",)),
        name="topk",
    )(lg_t, router_bias.reshape(ne, 1).astype(F32))


def _log2(n):
    assert n & (n - 1) == 0
    return n.bit_length() - 1


def _gather_groups(nvalid):
    return (nvalid + (GATHER_GROUP - 1)) >> _log2(GATHER_GROUP)


def _gather_rows(nvalid):
    return _gather_groups(nvalid) << _log2(GATHER_GROUP)


def _moe_kernel(n_tok, be_ref, nv_ref, a0_ref, tok_ref, tokn_ref, h_hbm, wg_ref, wu_ref, wd_ref, y_hbm,
                rows, stage, gsem, osem):
    del be_ref
    i = pl.program_id(0)
    nsteps = pl.num_programs(0)
    slot = i % 2
    id_group = SUBLANES
    tile = SUBLANES

    def start_group(tref, a0, g, dst):
        lead = a0 & (id_group - 1)
        w = (a0 & (LANES - 1)) - lead + g * id_group
        for u in range(id_group):
            tok = tref[w + u] & (n_tok - 1)
            dst_row = ROW_PAD - lead + g * id_group + u
            pltpu.make_async_copy(
                h_hbm.at[pl.ds(pl.multiple_of(tok * SUBLANES, SUBLANES), SUBLANES), :],
                rows.at[dst, pl.ds(pl.multiple_of(dst_row * SUBLANES, SUBLANES), SUBLANES), :],
                gsem.at[dst]).start()

    def issue(tref, a0, g_lo, g_hi, dst):
        def body(g, carry):
            start_group(tref, a0, g, dst)
            return carry
        lax.fori_loop(g_lo, g_hi, body, 0)

    def gather_groups(a0, nvalid):
        lead = a0 & (id_group - 1)
        return jnp.where(nvalid > 0, (lead + _gather_rows(nvalid) + id_group - 1) >> _log2(id_group), 0)

    def pieces(nvalid):
        rem = nvalid & (2 * MOE_SUB - 1)
        big = (nvalid >> _log2(2 * MOE_SUB)) + jnp.where(rem > MOE_SUB, 1, 0)
        small = jnp.where((rem > 0) & (rem <= MOE_SUB), 1, 0)
        return big, small

    def covered(nvalid):
        big, small = pieces(nvalid)
        return (2 * big + small) * MOE_SUB_GROUPS

    def out_copy(r, dst_row, src_slot, nrows=1):
        return pltpu.make_async_copy(
            stage.at[src_slot, pl.ds(pl.multiple_of(r * tile, tile), nrows * tile), :],
            y_hbm.at[pl.ds(pl.multiple_of(dst_row * tile, tile), nrows * tile), :],
            osem.at[src_slot])

    nv = nv_ref[i]
    a0 = a0_ref[i]
    nxt = jnp.minimum(i + 1, nsteps - 1)
    nv_n = jnp.where(i + 1 < nsteps, nv_ref[nxt], 0)
    a0_n = a0_ref[nxt]
    groups_n = gather_groups(a0_n, nv_n)
    covered_n = covered(nv)

    @pl.when(i == 0)
    def _():
        rows[...] = jnp.zeros_like(rows)
        issue(tok_ref, a0, 0, gather_groups(a0, nv), 0)

    issue(tokn_ref, a0_n, covered_n, groups_n, 1 - slot)

    prev = jnp.maximum(i - 1, 0)
    covered_here = jnp.where(i > 0, covered(nv_ref[prev]), 0)

    def wait_body(g, carry):
        pltpu.make_async_copy(h_hbm.at[pl.ds(0, id_group * SUBLANES), :],
                              rows.at[slot, pl.ds(0, id_group * SUBLANES), :], gsem.at[slot]).wait()
        return carry
    lax.fori_loop(0, jnp.maximum(gather_groups(a0, nv), covered_here), wait_body, 0)

    half = wg_ref.shape[1] // 2

    def piece(row0, m, g0):
        ng = (m // MOE_SUB) * MOE_SUB_GROUPS

        def start_next(k0, k1):
            for k in range(k0, k1):
                start_group(tokn_ref, a0_n, g0 + k, 1 - slot)

        q = ng // 4
        x_lo, x_hi = _unpack_bf16_pair(_load_row_tiles(rows.at[slot], ROW_PAD + row0, m))
        x_lo = x_lo.astype(BF16)
        x_hi = x_hi.astype(BF16)
        start_next(0, q)
        g = _dot_w(x_lo, wg_ref[0, :half, :])
        start_next(q, 2 * q)
        g = g + _dot_w(x_hi, wg_ref[0, half:, :])
        start_next(2 * q, 3 * q)
        u = _dot_w(x_lo, wu_ref[0, :half, :])
        start_next(3 * q, ng)
        u = u + _dot_w(x_hi, wu_ref[0, half:, :])
        act = (_silu(g) * u).astype(BF16)
        yb = _dot_w(act, wd_ref[0]).astype(BF16).astype(F32)
        _store_row_tiles(stage.at[slot], row0, _pack_bf16_pair(yb[:, :half], yb[:, half:]))

    n_big, n_small = pieces(nv)

    def big_piece(p, carry):
        piece(p * (2 * MOE_SUB), 2 * MOE_SUB, p * (2 * MOE_SUB_GROUPS))
        return carry
    lax.fori_loop(0, n_big, big_piece, 0)

    @pl.when(n_small > 0)
    def _():
        piece(n_big * (2 * MOE_SUB), MOE_SUB, n_big * (2 * MOE_SUB_GROUPS))

    nv_p = jnp.where(i > 0, nv_ref[prev], 0)

    def wait_prev_group(g, carry):
        out_copy(0, 0, 1 - slot, id_group).wait()
        return carry
    lax.fori_loop(0, nv_p >> _log2(id_group), wait_prev_group, 0)

    def wait_prev_row(g, carry):
        out_copy(0, 0, 1 - slot).wait()
        return carry
    lax.fori_loop(0, nv_p & (id_group - 1), wait_prev_row, 0)

    w_out0 = a0 & (LANES - 1)

    def start_out_group(g, carry):
        for u in range(id_group):
            r = g * id_group + u
            out_copy(r, tok_ref[w_out0 + r], slot).start()
        return carry
    lax.fori_loop(0, nv >> _log2(id_group), start_out_group, 0)

    def start_out_row(r, carry):
        out_copy(r, tok_ref[w_out0 + r], slot).start()
        return carry
    lax.fori_loop(nv & ~(id_group - 1), nv, start_out_row, 0)


def _moe(block_e, nvalid, a0, sorted_ids, h2p, w_gate, w_up, w_down, n_tok):
    nblk = block_e.shape[0]
    mb = MOE_BLOCK
    ne, d, f = w_gate.shape
    assert d == 2 * SUBLANES * LANES and mb % (2 * MOE_SUB) == 0 and MOE_SUB % GATHER_GROUP == 0
    assert MOE_SUB_GROUPS % 4 == 0
    assert n_tok & (n_tok - 1) == 0, "token id = assignment id & (n_tok - 1)"

    def win(shift):
        def imap(i, be, nv, a0):
            j = jnp.minimum(i + shift, nblk - 1)
            return (pl.multiple_of((a0[j] >> _log2(LANES)) << _log2(LANES), LANES),)
        return pl.BlockSpec((pl.Element(TOK_WINDOW),), imap, memory_space=pltpu.SMEM)

    wspec = lambda r, c: pl.BlockSpec((1, r, c), lambda i, be, nv, a0: (be[i], 0, 0))
    grid_spec = pltpu.PrefetchScalarGridSpec(
        num_scalar_prefetch=3,
        grid=(nblk,),
        in_specs=[win(0), win(1), pl.BlockSpec(memory_space=pl.ANY), wspec(d, f), wspec(d, f), wspec(f, d)],
        out_specs=pl.BlockSpec(memory_space=pl.ANY),
        scratch_shapes=[pltpu.VMEM((2, (mb + 2 * ROW_PAD) * SUBLANES, LANES), U32),
                        pltpu.VMEM((2, mb * SUBLANES, LANES), U32),
                        pltpu.SemaphoreType.DMA((2,)), pltpu.SemaphoreType.DMA((2,))],
    )
    return pl.pallas_call(
        functools.partial(_moe_kernel, n_tok),
        grid_spec=grid_spec,
        out_shape=jax.ShapeDtypeStruct((TOP_K * n_tok * SUBLANES, LANES), U32),
        compiler_params=_cparams(("arbitrary",)),
        name="moe",
    )(block_e, nvalid, a0, sorted_ids, sorted_ids, h2p, w_gate, w_up, w_down)


def _combine_kernel(*refs):
    y_refs = refs[:TOP_K]
    w_ref, xs1_ref, m5_ref, gf_ref, o_ref = refs[TOP_K:]
    tm = o_ref.shape[0]
    w = w_ref[...]
    acc_lo = acc_hi = None
    for k in range(TOP_K):
        lo, hi = _unpack_bf16_pair(_load_row_tiles(y_refs[k], 0, tm))
        wk = w[:, k:k + 1]
        acc_lo = wk * lo if acc_lo is None else acc_lo + wk * lo
        acc_hi = wk * hi if acc_hi is None else acc_hi + wk * hi
    routed = jnp.concatenate([acc_lo, acc_hi], axis=1)
    x = xs1_ref[...] + m5_ref[0] * routed
    ms = jnp.mean(x * x, axis=-1, keepdims=True)
    o_ref[...] = x * lax.rsqrt(ms + EPS) * gf_ref[...]


def _combine(y_packed, w_tok, xs1, m5, g_final, rows_per_batch, tm):
    n, d = xs1.shape
    assert d == 2 * SUBLANES * LANES
    nt = n // tm
    tpb = rows_per_batch // tm
    y_spec = lambda k: pl.BlockSpec((tm * SUBLANES, LANES), lambda i: (k * nt + i, 0))
    return pl.pallas_call(
        _combine_kernel,
        grid=(nt,),
        in_specs=[y_spec(k) for k in range(TOP_K)]
        + [pl.BlockSpec((tm, TOP_K), lambda i: (i, 0)),
           pl.BlockSpec((tm, d), lambda i: (i, 0)),
           pl.BlockSpec((1, 1, d), lambda i: (i // tpb, 0, 0)),
           pl.BlockSpec((1, d), lambda i: (0, 0))],
        out_specs=pl.BlockSpec((tm, d), lambda i: (i, 0)),
        out_shape=jax.ShapeDtypeStruct((n, d), F32),
        compiler_params=_cparams(("parallel",)),
        name="combine",
    )(*([y_packed] * TOP_K), w_tok, xs1, m5, g_final.reshape(1, d))


def _dispatch_plan(top_e, counts, n_tok):
    mb = MOE_BLOCK
    ne = counts.shape[0]
    n_asg = TOP_K * n_tok
    nblk = -(-(n_asg + ne * (mb - 1)) // mb) + 1
    asg_ids = jnp.arange(n_asg, dtype=I32).reshape(top_e.shape)
    keys = jnp.sort((top_e * n_asg + asg_ids).reshape(-1))
    sorted_ids = jnp.concatenate([keys % n_asg, jnp.zeros((TOK_WINDOW,), I32)])
    start = jnp.cumsum(counts) - counts
    eblk = (counts + mb - 1) // mb
    pend = jnp.cumsum(eblk)
    pstart = pend - eblk
    n_real = pend[-1]
    bid = jnp.arange(nblk, dtype=I32)
    last_real = jnp.maximum(n_real - 1, 0)
    bsrc = jnp.minimum(bid, last_real)
    block_e = jnp.minimum(jnp.sum((pend[None, :] <= bsrc[:, None]).astype(I32), axis=1), ne - 1)
    onehot = block_e[:, None] == jnp.arange(ne, dtype=I32)[None, :]
    look = lambda table: jnp.sum(jnp.where(onehot, table[None, :], 0), axis=1)
    off = (bsrc - look(pstart)) * mb
    nvalid = jnp.where(bid < n_real, jnp.clip(look(counts) - off, 0, mb), 0).astype(I32)
    a0 = (look(start) + off).astype(I32)
    return block_e.astype(I32), nvalid, a0, sorted_ids


def kernel(x, c, ctx, c_ctx, w_ada, b_ada, g_mix, w_in, conv_w, conv_b, dt_bias, a_log, d_skip, g_ssd,
           pool_w, pool_scale, w_out, g_ffn, w_router, router_bias, w_exp_gate, w_exp_up, w_exp_down,
           w_sh_gate, w_sh_up, w_sh_down, g_final):
    bsz, seq, d = x.shape
    ctx_len = ctx.shape[1]
    assert w_ada.shape[0] == 1, "single-layer block"
    d_ssm = g_ssd.shape[1]
    heads = d_skip.shape[1]
    d_pool = pool_scale.shape[1]
    d_xbc = conv_w.shape[2]
    n = bsz * seq

    cvec = jnp.zeros((8, d), F32).at[:bsz].set(c).at[bsz].set(c_ctx)
    mod_all = _ada(cvec, w_ada[0], b_ada[0]).reshape(8, N_MOD, d)
    mod = mod_all[:bsz]
    mod_c = mod_all[bsz:bsz + 1]
    mk = lambda m, k: m[:, k:k + 1, :]

    wt = jnp.transpose(w_in[0])
    c_dt = d_ssm + d_xbc
    wtb = wt.astype(BF16)
    w_pl = wtb[c_dt + 2 * heads:]
    w_dt = wt[c_dt:c_dt + 2 * heads].reshape(2, SSM_GROUPS, heads // SSM_GROUPS, d)
    w_dt = jnp.transpose(w_dt, (1, 0, 2, 3)).reshape(2 * heads, d)
    w_dt = jnp.pad(w_dt, ((0, LANES - 2 * heads), (0, 0)))

    main_c, dt_c = _inproj(ctx.reshape(bsz * ctx_len, d), g_mix[0], mk(mod_c, 0), mk(mod_c, 1),
                           wtb, c_dt, w_pl, w_dt, bsz * ctx_len, tm=256, tn=1024)
    h_zero = jnp.zeros((bsz, SSM_GROUPS, 2, SSM_STATE, 4 * SSM_HEADDIM), F32)
    h_ctx = _ssd(main_c.reshape(bsz, ctx_len, -1), dt_c, conv_w[0], conv_b[0],
                 dt_bias[0], a_log[0], None, h_zero, False, d_ssm, d_ssm)

    x2d = x.reshape(n, d)
    main, dt_raw = _inproj(x2d, g_mix[0], mk(mod, 0), mk(mod, 1), wtb, c_dt, w_pl, w_dt, seq,
                           tm=min(1024, seq), tn=1024)
    main3 = main.reshape(bsz, seq, -1)
    dsk = jnp.repeat(d_skip[0].astype(F32), SSM_HEADDIM).reshape(1, d_ssm)
    yg = _ssd(main3, dt_raw, conv_w[0], conv_b[0], dt_bias[0], a_log[0], dsk, h_ctx, True, d_ssm, d_ssm)
    yp = _pool(main3, pool_w[0], pool_scale[0], d_ssm + d_xbc)

    wo = w_out[0].astype(BF16)
    wr = w_router[0].T
    wr_hi = wr.astype(BF16)
    wr_lo = (wr - wr_hi.astype(F32)).astype(BF16)
    xs1, h2p, lg_t = _outproj(
        yg.reshape(n, d_ssm), yp.reshape(n, d_pool), x2d, g_ssd[0],
        (mk(mod, 2), mk(mod, 3), mk(mod, 4), mk(mod, 5)), g_ffn[0],
        wo[:d_ssm], wo[d_ssm:], wr_hi, wr_lo,
        w_sh_gate[0].astype(BF16), w_sh_up[0].astype(BF16), w_sh_down[0].astype(BF16), seq, tm=256)

    top_e, top_w, cnt = _topk(lg_t, router_bias[0], tm=512)
    counts = cnt[:, 0].astype(I32)
    block_e, nvalid, a0, sorted_ids = _dispatch_plan(top_e, counts, n)
    y_packed = _moe(block_e, nvalid, a0, sorted_ids, h2p, w_exp_gate[0], w_exp_up[0], w_exp_down[0], n)
    out = _combine(y_packed, top_w.T, xs1, mk(mod, 5), g_final, seq, tm=256)
    return out.reshape(bsz, seq, d)
```

```python
import functools

import jax
import jax.numpy as jnp
from jax import lax
from jax.experimental import pallas as pl
from jax.experimental.pallas import tpu as pltpu

F32 = jnp.float32
BF16 = jnp.bfloat16
I32 = jnp.int32
U32 = jnp.uint32

EPS = 1e-6
GRID_W = 64
SSM_HEADDIM = 64
SSM_GROUPS = 8
SSM_STATE = 128
SSD_CHUNK = 128
POOL_WINDOWS = (2, 4, 8, 16)
TOP_K = 8
N_EXPERT_GROUPS = 8
TOPK_GROUPS = 4
ROUTE_SCALE = 2.5
N_MOD = 6
LANES = 128
SUBLANES = 8
BF16_ROWS = 16
CONV_TILE = 128
MOE_BLOCK = 1024
GATHER_GROUP = 64
MOE_SUB = 128
MOE_SUB_GROUPS = 16
ROW_PAD = SUBLANES
TOK_WINDOW = pl.next_power_of_2(LANES + MOE_BLOCK + 2 * ROW_PAD)
V7X_VMEM_LIMIT = 56 * 1024 * 1024


def _cparams(sem, vmem=V7X_VMEM_LIMIT):
    return pltpu.CompilerParams(dimension_semantics=sem, vmem_limit_bytes=vmem)


def _sigmoid(x):
    return 1.0 / (1.0 + jnp.exp(-x))


def _silu(x):
    return x * _sigmoid(x)


def _split2(x):
    hi = x.astype(BF16)
    lo = (x - hi.astype(F32)).astype(BF16)
    return hi, lo


def _dot(a, b):
    return jnp.dot(a, b, preferred_element_type=F32)


def _dot_nt(a, b):
    return lax.dot_general(a, b, (((1,), (1,)), ((), ())), preferred_element_type=F32)


def _dot_tn(a, b):
    return lax.dot_general(a, b, (((0,), (0,)), ((), ())), preferred_element_type=F32)


def _dot_w(a, w):
    return lax.dot_general(a, w, (((1,), (0,)), ((), ())), preferred_element_type=F32)


def _dot3(a, b):
    a_hi, a_lo = _split2(a)
    b_hi, b_lo = _split2(b)
    return _dot(a_hi, b_hi) + _dot(a_lo, b_hi) + _dot(a_hi, b_lo)


def _ada_kernel(c_ref, w_ref, b_ref, o_ref):
    c = c_ref[...]
    o_ref[...] = _dot3(_silu(c), w_ref[...]) + b_ref[...]


def _ada(cvec, w_ada, b_ada, tn=1024):
    d, n = w_ada.shape
    return pl.pallas_call(
        _ada_kernel,
        grid=(n // tn,),
        in_specs=[pl.BlockSpec((8, d), lambda j: (0, 0)),
                  pl.BlockSpec((d, tn), lambda j: (0, j)),
                  pl.BlockSpec((1, tn), lambda j: (0, j))],
        out_specs=pl.BlockSpec((8, tn), lambda j: (0, j)),
        out_shape=jax.ShapeDtypeStruct((8, n), F32),
        compiler_params=_cparams(("parallel",)),
        name="ada",
    )(cvec, w_ada, b_ada.reshape(1, n))


def _inproj_kernel(na, x_ref, g_ref, sh_ref, sc_ref, wa_ref, wb_ref, wdt_ref, o_ref, dt_ref, h_scr):
    j = pl.program_id(1)

    @pl.when(j == 0)
    def _():
        x = x_ref[...]
        ms = jnp.mean(x * x, axis=-1, keepdims=True)
        y = x * lax.rsqrt(ms + EPS) * g_ref[...]
        h = y * (1.0 + sc_ref[0]) + sh_ref[0]
        h_hi, h_lo = _split2(h)
        h_scr[...] = h_hi
        w_hi, w_lo = _split2(wdt_ref[...])
        dt_ref[...] = _dot_nt(w_hi, h_hi) + _dot_nt(w_lo, h_hi) + _dot_nt(w_hi, h_lo)

    @pl.when(j < na)
    def _():
        o_ref[...] = _dot_nt(h_scr[...], wa_ref[...]).astype(o_ref.dtype)

    @pl.when(j >= na)
    def _():
        o_ref[...] = _dot_nt(h_scr[...], wb_ref[...]).astype(o_ref.dtype)


def _inproj(x2d, g, shift, scale, w_a_t, ca, w_b_t, w_dt_t, rows_per_batch, tm, tn):
    n, d = x2d.shape
    na, nbk = ca // tn, w_b_t.shape[0] // tn
    assert ca % tn == 0 and w_b_t.shape[0] % tn == 0
    nc = (na + nbk) * tn
    tpb = rows_per_batch // tm
    return pl.pallas_call(
        functools.partial(_inproj_kernel, na),
        grid=(n // tm, nc // tn),
        in_specs=[pl.BlockSpec((tm, d), lambda i, j: (i, 0)),
                  pl.BlockSpec((1, d), lambda i, j: (0, 0)),
                  pl.BlockSpec((1, 1, d), lambda i, j: (i // tpb, 0, 0)),
                  pl.BlockSpec((1, 1, d), lambda i, j: (i // tpb, 0, 0)),
                  pl.BlockSpec((tn, d), lambda i, j: (jnp.minimum(j, na - 1), 0)),
                  pl.BlockSpec((tn, d), lambda i, j: (jnp.maximum(j - na, 0), 0)),
                  pl.BlockSpec((LANES, d), lambda i, j: (0, 0))],
        out_specs=[pl.BlockSpec((tm, tn), lambda i, j: (i, j)),
                   pl.BlockSpec((LANES, tm), lambda i, j: (0, i))],
        out_shape=[jax.ShapeDtypeStruct((n, nc), BF16),
                   jax.ShapeDtypeStruct((LANES, n), F32)],
        scratch_shapes=[pltpu.VMEM((tm, d), BF16)],
        compiler_params=_cparams(("parallel", "arbitrary")),
        name="inproj",
    )(x2d, g.reshape(1, d), shift, scale, w_a_t, w_b_t, w_dt_t)


def _expand_heads(v, base, width):
    t = v.shape[0]
    lane = lax.broadcasted_iota(I32, (t, 4 * width), 1)
    out = jnp.broadcast_to(v[:, base + 3:base + 4], (t, 4 * width))
    for j in (2, 1, 0):
        out = jnp.where(lane < (j + 1) * width, v[:, base + j:base + j + 1], out)
    return out


def _conv_silu(src_ref, w_ref, b_ref, dst_ref, length):
    c = src_ref.shape[-1]
    tile, grp = CONV_TILE, BF16_ROWS
    w = w_ref[...]
    b = b_ref[...]
    rid = lax.broadcasted_iota(I32, (tile, c), 0)

    def body(k, carry):
        r0 = pl.multiple_of(k * tile, tile)
        cur = src_ref[0, pl.ds(r0, tile), :].astype(F32)
        lo = pl.multiple_of(jnp.maximum(r0 - grp, 0), grp)
        hi = pl.multiple_of(jnp.minimum(r0 + tile, length - grp), grp)
        prev_row = src_ref[0, pl.ds(lo, grp), :].astype(F32)[grp - 1:grp, :]
        next_row = src_ref[0, pl.ds(hi, grp), :].astype(F32)[0:1, :]
        prev_row = jnp.where(r0 > 0, prev_row, 0.0)
        next_row = jnp.where(r0 + tile < length, next_row, 0.0)
        up = jnp.where(rid == 0, prev_row, pltpu.roll(cur, 1, 0))
        dn = jnp.where(rid == tile - 1, next_row, pltpu.roll(cur, tile - 1, 0))
        o = up * w[0:1, :] + cur * w[1:2, :] + dn * w[2:3, :] + b
        dst_ref[pl.ds(r0, tile), :] = _silu(o).astype(dst_ref.dtype)
        return carry
    lax.fori_loop(0, length // tile, body, 0, unroll=2)


def _softplus(x):
    return jnp.maximum(x, 0.0) + jnp.log(1.0 + jnp.exp(-jnp.abs(x)))


def _ssd_kernel(need_y, length, *refs):
    t = SSD_CHUNK
    nch = length // t
    hw = 4 * SSM_HEADDIM
    if need_y:
        (xr, br, cr, zr, cwx, cbx, cwb, cbb, cwc, cbc, dtr, bias_r, a_r, dsk, h0,
         y_ref, xs_s, b_s, c_s, e_s, dtr_s, csr_s, csc_s, aux_s, st_s, dec_s) = refs
    else:
        (xr, br, cwx, cbx, cwb, cbb, dtr, bias_r, a_r, h0,
         hfin, xs_s, b_s, dtr_s, csr_s, csc_s, aux_s, st_s, dec_s) = refs

    _conv_silu(xr, cwx, cbx, xs_s, length)
    _conv_silu(br, cwb, cbb, b_s, length)
    if need_y:
        _conv_silu(cr, cwc, cbc, c_s, length)

    dt_r = _softplus(dtr[0, 0] + bias_r[0])
    dtr_s[...] = dt_r
    da2 = (dt_r * a_r[0]).reshape(nch * 8, t)
    kk = lax.broadcasted_iota(I32, (t, 2 * t), 0)
    ll = lax.broadcasted_iota(I32, (t, 2 * t), 1)
    tri = jnp.where(ll < t, jnp.where(kk <= ll, 1.0, 0.0), jnp.where(kk >= ll - t, 1.0, 0.0)).astype(BF16)
    p0 = da2.astype(BF16)
    r1 = da2 - p0.astype(F32)
    p1 = r1.astype(BF16)
    p2 = (r1 - p1.astype(F32)).astype(BF16)
    cum = _dot(p0, tri) + _dot(p1, tri) + _dot(p2, tri)
    rowj = lax.broadcasted_iota(I32, (nch * 8, t), 0) & 7
    csr = jnp.where(rowj < 4, cum[:, :t], cum[:, t:])
    csr_s[...] = csr.reshape(nch, 8, t)

    edge = jnp.where(rowj[:, 0:1] < 4, csr[:, t - 1:t], csr[:, 0:1])
    aux = [csr, dt_r.reshape(nch * 8, t) * jnp.exp(edge - csr), jnp.exp(csr),
           jnp.broadcast_to(jnp.exp(edge), (nch * 8, t))]
    for k, v in enumerate(aux):
        aux_s[:, 8 * k:8 * (k + 1), :] = v.reshape(nch, 8, t)

    spread = ((lax.broadcasted_iota(I32, (8, 2 * hw), 1) >> _log2(SSM_HEADDIM))
              == lax.broadcasted_iota(I32, (8, 2 * hw), 0))
    spread = jnp.where(spread, 1.0, 0.0).astype(BF16)

    def spread_heads(v):
        hi, lo = _split2(v)
        return _dot(hi, spread) + _dot(lo, spread)

    def phase_a(c, carry):
        r0 = pl.multiple_of(c * t, t)
        cols = jnp.concatenate([aux_s[c], jnp.zeros((t - 8 * len(aux), t), F32)], axis=0).T
        csc_s[c] = cols[:, 0:8]
        wcol = cols[:, 8:16]
        if need_y:
            e_s[c] = cols[:, 16:24]
        dec = cols[0:1, 24:32]
        xs = xs_s[pl.ds(r0, t), :]
        bc = b_s[pl.ds(r0, t), :]
        w_e = spread_heads(wcol)
        xw_f = (xs * w_e[:, :hw]).astype(BF16)
        xw_b = (xs * w_e[:, hw:]).astype(BF16)
        st_s[c, 0] = _dot_tn(bc, xw_f)
        st_s[c, 1] = _dot_tn(bc, xw_b)
        dec_s[c, 0:1, :] = _expand_heads(dec, 0, SSM_HEADDIM)
        dec_s[c, 1:2, :] = _expand_heads(dec, 4, SSM_HEADDIM)
        return carry

    lax.fori_loop(0, nch, phase_a, 0, unroll=4 if nch % 4 == 0 else 2)

    def rec_f(c, s):
        loc = st_s[c, 0]
        st_s[c, 0] = s
        return dec_s[c, 0:1, :] * s + loc

    def rec_b(k, s):
        c = nch - 1 - k
        loc = st_s[c, 1]
        st_s[c, 1] = s
        return dec_s[c, 1:2, :] * s + loc

    s_f = lax.fori_loop(0, nch, rec_f, h0[0, 0, 0])
    s_b = lax.fori_loop(0, nch, rec_b, h0[0, 0, 1])
    if not need_y:
        hfin[0, 0, 0] = s_f
        hfin[0, 0, 1] = s_b
        return

    li = lax.broadcasted_iota(I32, (t, t), 0)
    si = lax.broadcasted_iota(I32, (t, t), 1)
    dskip = dsk[...]

    def phase_c(c, carry):
        r0 = pl.multiple_of(c * t, t)
        csr_c = csr_s[c]
        csc_c = csc_s[c]
        dtr_c = dtr_s[c]
        e_c = e_s[c]
        xs = xs_s[pl.ds(r0, t), :]
        xs_b = xs.astype(BF16)
        bc = b_s[pl.ds(r0, t), :]
        cc = c_s[pl.ds(r0, t), :]
        cb = _dot_nt(cc, bc)
        ys = []
        for j in range(4):
            d_f = csc_c[:, j:j + 1] - csr_c[j:j + 1, :]
            att_f = jnp.where(si <= li, (cb * dtr_c[j:j + 1, :]) * jnp.exp(jnp.minimum(d_f, 0.0)), 0.0)
            d_b = csc_c[:, 4 + j:5 + j] - csr_c[4 + j:5 + j, :]
            att_b = jnp.where(si >= li, (cb * dtr_c[4 + j:5 + j, :]) * jnp.exp(jnp.minimum(d_b, 0.0)), 0.0)
            lhs = jnp.concatenate([att_f, att_b], axis=1).astype(BF16)
            xj = xs_b[:, j * SSM_HEADDIM:(j + 1) * SSM_HEADDIM]
            ys.append(_dot(lhs, jnp.concatenate([xj, xj], axis=0)))
        y = jnp.concatenate(ys, axis=1)
        e_e = spread_heads(e_c)
        y = y + _dot(cc, st_s[c, 0].astype(BF16)) * e_e[:, :hw]
        y = y + _dot(cc, st_s[c, 1].astype(BF16)) * e_e[:, hw:]
        y = y + xs * dskip
        z = zr[0, pl.ds(r0, t), :].astype(F32)
        y_ref[0, pl.ds(r0, t), :] = (y * _silu(z)).astype(y_ref.dtype)
        return carry

    lax.fori_loop(0, nch, phase_c, 0, unroll=4 if nch % 4 == 0 else 2)


def _ssd(main3, dt_t, conv_w, conv_b, dt_bias, a_log, d_skip, h0, need_y, d_ssm, xbc_col0):
    nb, length, _ = main3.shape
    g_n, t = SSM_GROUPS, SSD_CHUNK
    nch = length // t
    heads = d_ssm // SSM_HEADDIM
    hpg = heads // g_n
    assert hpg == 4 and SSM_STATE == 128
    hw = hpg * SSM_HEADDIM
    dtr = jnp.transpose(dt_t[:2 * heads].reshape(g_n, 2 * hpg, nb, nch, t), (2, 0, 3, 1, 4))
    par = lambda p: jnp.transpose(p.reshape(2, g_n, hpg), (1, 0, 2)).reshape(g_n, 8)
    bias_r = par(dt_bias.astype(F32)).reshape(g_n, 8, 1)
    a_r = par(-jnp.exp(a_log.astype(F32))).reshape(g_n, 8, 1)
    nx = d_ssm
    nbc = g_n * SSM_STATE
    cwx, cwb, cwc = conv_w[:, :nx], conv_w[:, nx:nx + nbc], conv_w[:, nx + nbc:]
    cb2 = conv_b.reshape(1, -1)
    cbx, cbb, cbc = cb2[:, :nx], cb2[:, nx:nx + nbc], cb2[:, nx + nbc:]
    xcol = xbc_col0 // hw
    bcol = (xbc_col0 + nx) // SSM_STATE
    ccol = (xbc_col0 + nx + nbc) // SSM_STATE

    seq = lambda shape, imap: pl.BlockSpec(shape, imap)
    x_spec = seq((1, length, hw), lambda b, g: (b, 0, xcol + g))
    b_spec = seq((1, length, SSM_STATE), lambda b, g: (b, 0, bcol + g))
    c_spec = seq((1, length, SSM_STATE), lambda b, g: (b, 0, ccol + g))
    z_spec = seq((1, length, hw), lambda b, g: (b, 0, g))
    wx_spec = [seq((3, hw), lambda b, g: (0, g)), seq((1, hw), lambda b, g: (0, g))]
    wn_spec = [seq((3, SSM_STATE), lambda b, g: (0, g)), seq((1, SSM_STATE), lambda b, g: (0, g))]
    dt_specs = [seq((1, 1, nch, 8, t), lambda b, g: (b, g, 0, 0, 0)),
                seq((1, 8, 1), lambda b, g: (g, 0, 0)), seq((1, 8, 1), lambda b, g: (g, 0, 0))]
    h_spec = seq((1, 1, 2, SSM_STATE, hw), lambda b, g: (b, g, 0, 0, 0))
    common_scr = [pltpu.VMEM((nch, 8, t), F32),
                  pltpu.VMEM((nch, 8, t), F32),
                  pltpu.VMEM((nch, t, 8), F32),
                  pltpu.VMEM((nch, 32, t), F32),
                  pltpu.VMEM((nch, 2, SSM_STATE, hw), F32),
                  pltpu.VMEM((nch, 8, hw), F32)]
    if need_y:
        in_specs = ([x_spec, b_spec, c_spec, z_spec] + wx_spec + wn_spec + wn_spec + dt_specs
                    + [seq((1, hw), lambda b, g: (0, g)), h_spec])
        args = (main3, main3, main3, main3, cwx, cbx, cwb, cbb, cwc, cbc, dtr, bias_r, a_r, d_skip, h0)
        out_specs = seq((1, length, hw), lambda b, g: (b, 0, g))
        out_shape = jax.ShapeDtypeStruct((nb, length, d_ssm), BF16)
        scratch = [pltpu.VMEM((length, hw), F32), pltpu.VMEM((length, SSM_STATE), BF16),
                   pltpu.VMEM((length, SSM_STATE), BF16),
                   pltpu.VMEM((nch, t, 8), F32)] + common_scr
    else:
        in_specs = [x_spec, b_spec] + wx_spec + wn_spec + dt_specs + [h_spec]
        args = (main3, main3, cwx, cbx, cwb, cbb, dtr, bias_r, a_r, h0)
        out_specs = h_spec
        out_shape = jax.ShapeDtypeStruct((nb, g_n, 2, SSM_STATE, hw), F32)
        scratch = [pltpu.VMEM((length, hw), F32), pltpu.VMEM((length, SSM_STATE), BF16)] + common_scr
    return pl.pallas_call(
        functools.partial(_ssd_kernel, need_y, length),
        grid=(nb, g_n),
        in_specs=in_specs,
        out_specs=out_specs,
        out_shape=out_shape,
        scratch_shapes=scratch,
        compiler_params=_cparams(("parallel", "parallel")),
        name="ssd_y" if need_y else "ssd_state",
    )(*args)


def _pool_kernel(length, u_ref, pw_ref, ps_ref, o_ref, pad_s):
    gw = GRID_W
    rows = length // gw
    cg = pw_ref.shape[-1]
    halo = (max(POOL_WINDOWS) // 2) * gw
    tile = min(256, length)
    zeros = jnp.zeros((halo, cg), F32)
    pad_s[0:halo, :] = zeros
    pad_s[halo + length:halo + length + halo, :] = zeros
    for g, w in enumerate(POOL_WINDOWS):
        lo = -(w // 2)
        csl = slice(g * cg, (g + 1) * cg)
        for r0 in range(0, length, tile):
            pad_s[halo + r0:halo + r0 + tile, :] = u_ref[0, r0:r0 + tile, csl].astype(F32)
        pw = pw_ref[g]
        scale = ps_ref[:, csl]

        def body(i, carry, lo=lo, w=w, csl=csl, pw=pw, scale=scale):
            r0 = pl.multiple_of(i * tile, tile)
            acc = pad_s[pl.ds(halo + r0 + lo * gw, tile), :]
            for k in range(lo + 1, lo + w):
                acc = acc + pad_s[pl.ds(halo + r0 + k * gw, tile), :]
            l_idx = r0 + lax.broadcasted_iota(I32, (tile, cg), 0)
            ii = l_idx >> (gw.bit_length() - 1)
            jj = l_idx & (gw - 1)
            tot = acc
            for k in range(lo, lo + w):
                if k == 0:
                    continue
                sh = pltpu.roll(acc, (-k) % tile, 0)
                ok = (jj >= -k) if k < 0 else (jj < gw - k)
                tot = tot + jnp.where(ok, sh, 0.0)
            cnt_i = jnp.minimum(ii + lo + w, rows) - jnp.maximum(ii + lo, 0)
            cnt_j = jnp.minimum(jj + lo + w, gw) - jnp.maximum(jj + lo, 0)
            u = pad_s[pl.ds(halo + r0, tile), :]
            d = tot / (cnt_i * cnt_j).astype(F32) - u
            y = _dot(d.astype(BF16), pw) * scale
            o_ref[0, pl.ds(r0, tile), csl] = y.astype(o_ref.dtype)
            return carry

        lax.fori_loop(0, length // tile, body, 0)


def _pool(main3, pool_w, pool_scale, col0):
    nb, length, _ = main3.shape
    ng, cg, _ = pool_w.shape
    dp = ng * cg
    halo = (max(POOL_WINDOWS) // 2) * GRID_W
    return pl.pallas_call(
        functools.partial(_pool_kernel, length),
        grid=(nb,),
        in_specs=[pl.BlockSpec((1, length, dp), lambda b: (b, 0, col0 // dp)),
                  pl.BlockSpec((ng, cg, cg), lambda b: (0, 0, 0)),
                  pl.BlockSpec((1, dp), lambda b: (0, 0))],
        out_specs=pl.BlockSpec((1, length, dp), lambda b: (b, 0, 0)),
        out_shape=jax.ShapeDtypeStruct((nb, length, dp), BF16),
        scratch_shapes=[pltpu.VMEM((length + 2 * halo, cg), F32)],
        compiler_params=_cparams(("parallel",)),
        name="pool",
    )(main3, pool_w.astype(BF16), pool_scale.reshape(1, dp))


def _pack_bf16_pair(lo, hi):
    lo_b = lax.bitcast_convert_type(lo, U32) >> 16
    hi_b = lax.bitcast_convert_type(hi, U32) & jnp.uint32(0xFFFF0000)
    return lo_b | hi_b


def _unpack_bf16_pair(p):
    lo = lax.bitcast_convert_type(p << 16, F32)
    hi = lax.bitcast_convert_type(p & jnp.uint32(0xFFFF0000), F32)
    return lo, hi


def _store_row_tiles(ref, row0, value):
    m = value.shape[0]
    for c in range(SUBLANES):
        ref[pl.ds(row0 * SUBLANES + c, m, stride=SUBLANES), :] = value[:, c * LANES:(c + 1) * LANES]


def _load_row_tiles(ref, row0, m):
    return jnp.concatenate(
        [ref[pl.ds(row0 * SUBLANES + c, m, stride=SUBLANES), :] for c in range(SUBLANES)], axis=1)


def _outproj_kernel(yg_ref, yp_ref, x_ref, gs_ref, m2_ref, m3_ref, m4_ref, m5_ref, gf_ref,
                    ws_ref, wp_ref, wrh_ref, wrl_ref, wg_ref, wu_ref, wd_ref,
                    xs1_ref, h2p_ref, lg_ref):
    yg = yg_ref[...].astype(F32)
    ms = jnp.mean(yg * yg, axis=-1, keepdims=True)
    a = (yg * lax.rsqrt(ms + EPS) * gs_ref[...]).astype(BF16)
    o = _dot(a, ws_ref[...]) + _dot(yp_ref[...], wp_ref[...])
    x1 = x_ref[...] + m2_ref[0] * o
    ms1 = jnp.mean(x1 * x1, axis=-1, keepdims=True)
    h2 = x1 * lax.rsqrt(ms1 + EPS) * gf_ref[...]
    h2 = h2 * (1.0 + m4_ref[0]) + m3_ref[0]
    h_hi, h_lo = _split2(h2)
    wrh = wrh_ref[...]
    lg_ref[...] = _dot_nt(wrh, h_hi) + _dot_nt(wrl_ref[...], h_hi) + _dot_nt(wrh, h_lo)
    act = (_silu(_dot(h_hi, wg_ref[...])) * _dot(h_hi, wu_ref[...])).astype(BF16)
    xs1_ref[...] = x1 + m5_ref[0] * _dot(act, wd_ref[...])
    hf = h_hi.astype(F32)
    half = hf.shape[1] // 2
    _store_row_tiles(h2p_ref, 0, _pack_bf16_pair(hf[:, :half], hf[:, half:]))


def _outproj(yg, yp, x2d, g_ssd, mods, g_ffn, w_ssd, w_pool, wr_hi, wr_lo, wsg, wsu, wsd,
             rows_per_batch, tm):
    n, d = x2d.shape
    dp = yp.shape[1]
    ne = wr_hi.shape[0]
    dsh = wsg.shape[1]
    tpb = rows_per_batch // tm
    row = lambda c: pl.BlockSpec((tm, c), lambda i: (i, 0))
    vec = pl.BlockSpec((1, d), lambda i: (0, 0))
    mod = pl.BlockSpec((1, 1, d), lambda i: (i // tpb, 0, 0))
    res = lambda r, c: pl.BlockSpec((r, c), lambda i: (0, 0), pipeline_mode=pl.Buffered(1))
    m2, m3, m4, m5 = mods
    return pl.pallas_call(
        _outproj_kernel,
        grid=(n // tm,),
        in_specs=[row(d), row(dp), row(d), vec, mod, mod, mod, mod, vec,
                  res(d, d), res(dp, d), res(ne, d), res(ne, d), res(d, dsh), res(d, dsh), res(dsh, d)],
        out_specs=[row(d), pl.BlockSpec((tm * SUBLANES, LANES), lambda i: (i, 0)),
                   pl.BlockSpec((ne, tm), lambda i: (0, i))],
        out_shape=[jax.ShapeDtypeStruct((n, d), F32),
                   jax.ShapeDtypeStruct((n * SUBLANES, LANES), U32),
                   jax.ShapeDtypeStruct((ne, n), F32)],
        compiler_params=_cparams(("parallel",)),
        name="outproj",
    )(yg, yp, x2d, g_ssd.reshape(1, d), m2, m3, m4, m5, g_ffn.reshape(1, d),
      w_ssd, w_pool, wr_hi, wr_lo, wsg, wsu, wsd)


def _topk_kernel(lg_ref, rb_ref, te_ref, tw_ref, cnt_ref, carry):
    ne, tm = lg_ref.shape
    gsz = ne // N_EXPERT_GROUPS

    @pl.when(pl.program_id(0) == 0)
    def _():
        carry[...] = jnp.zeros_like(carry)

    s = _sigmoid(lg_ref[...])
    biased = s + rb_ref[...]
    neg = jnp.float32(-jnp.inf)
    big = jnp.int32(1 << 20)
    gi = lax.broadcasted_iota(I32, (gsz, tm), 0)
    gscore = []
    for g in range(N_EXPERT_GROUPS):
        v = biased[g * gsz:(g + 1) * gsz, :]
        m1 = jnp.max(v, axis=0, keepdims=True)
        i1 = jnp.min(jnp.where(v == m1, gi, big), axis=0, keepdims=True)
        m2 = jnp.max(jnp.where(gi == i1, neg, v), axis=0, keepdims=True)
        gscore.append(m1 + m2)
    parts = []
    for g in range(N_EXPERT_GROUPS):
        rank = jnp.zeros((1, tm), I32)
        for h in range(N_EXPERT_GROUPS):
            if h == g:
                continue
            ahead = (gscore[h] > gscore[g]) | ((gscore[h] == gscore[g]) & (h < g))
            rank = rank + ahead.astype(I32)
        keep = rank < TOPK_GROUPS
        parts.append(jnp.where(keep, biased[g * gsz:(g + 1) * gsz, :], neg))
    masked = jnp.concatenate(parts, axis=0)
    ei = lax.broadcasted_iota(I32, (ne, tm), 0)
    idxs, wts = [], []
    msel = jnp.zeros((ne, tm), F32)
    for _ in range(TOP_K):
        m = jnp.max(masked, axis=0, keepdims=True)
        idx = jnp.min(jnp.where(masked == m, ei, big), axis=0, keepdims=True)
        sel = ei == idx
        wts.append(jnp.sum(jnp.where(sel, s, 0.0), axis=0, keepdims=True))
        idxs.append(idx)
        masked = jnp.where(sel, neg, masked)
        msel = jnp.where(sel, 1.0, msel)
    wsum = wts[0]
    for w in wts[1:]:
        wsum = wsum + w
    for k in range(TOP_K):
        te_ref[k:k + 1, :] = idxs[k]
        tw_ref[k:k + 1, :] = wts[k] / wsum * ROUTE_SCALE
    total = carry[...] + _dot(msel.astype(BF16), jnp.ones((tm, LANES), BF16))
    carry[...] = total
    cnt_ref[...] = total


def _topk(lg_t, router_bias, tm):
    ne, n = lg_t.shape
    row8 = lambda dt: jax.ShapeDtypeStruct((TOP_K, n), dt)
    return pl.pallas_call(
        _topk_kernel,
        grid=(n // tm,),
        in_specs=[pl.BlockSpec((ne, tm), lambda i: (0, i)),
                  pl.BlockSpec((ne, 1), lambda i: (0, 0))],
        out_specs=[pl.BlockSpec((TOP_K, tm), lambda i: (0, i))] * 2
        + [pl.BlockSpec((ne, LANES), lambda i: (0, 0))],
        out_shape=[row8(I32), row8(F32), jax.ShapeDtypeStruct((ne, LANES), F32)],
        scratch_shapes=[pltpu.VMEM((ne, LANES), F32)],
        compiler_params=_cparams(("arbitrary",)),
        name="topk",
    )(lg_t, router_bias.reshape(ne, 1).astype(F32))


def _log2(n):
    assert n & (n - 1) == 0
    return n.bit_length() - 1


def _gather_groups(nvalid):
    return (nvalid + (GATHER_GROUP - 1)) >> _log2(GATHER_GROUP)


def _gather_rows(nvalid):
    return _gather_groups(nvalid) << _log2(GATHER_GROUP)


def _moe_kernel(n_tok, be_ref, nv_ref, a0_ref, tok_ref, tokn_ref, h_hbm, wg_ref, wu_ref, wd_ref, y_hbm,
                rows, stage, gsem, osem):
    del be_ref
    i = pl.program_id(0)
    nsteps = pl.num_programs(0)
    slot = i % 2
    id_group = SUBLANES
    tile = SUBLANES

    def start_group(tref, a0, g, dst):
        lead = a0 & (id_group - 1)
        w = (a0 & (LANES - 1)) - lead + g * id_group
        for u in range(id_group):
            tok = tref[w + u] & (n_tok - 1)
            dst_row = ROW_PAD - lead + g * id_group + u
            pltpu.make_async_copy(
                h_hbm.at[pl.ds(pl.multiple_of(tok * SUBLANES, SUBLANES), SUBLANES), :],
                rows.at[dst, pl.ds(pl.multiple_of(dst_row * SUBLANES, SUBLANES), SUBLANES), :],
                gsem.at[dst]).start()

    def issue(tref, a0, g_lo, g_hi, dst):
        def body(g, carry):
            start_group(tref, a0, g, dst)
            return carry
        lax.fori_loop(g_lo, g_hi, body, 0)

    def gather_groups(a0, nvalid):
        lead = a0 & (id_group - 1)
        return jnp.where(nvalid > 0, (lead + _gather_rows(nvalid) + id_group - 1) >> _log2(id_group), 0)

    def pieces(nvalid):
        rem = nvalid & (2 * MOE_SUB - 1)
        big = (nvalid >> _log2(2 * MOE_SUB)) + jnp.where(rem > MOE_SUB, 1, 0)
        small = jnp.where((rem > 0) & (rem <= MOE_SUB), 1, 0)
        return big, small

    def covered(nvalid):
        big, small = pieces(nvalid)
        return (2 * big + small) * MOE_SUB_GROUPS

    def out_copy(r, dst_row, src_slot, nrows=1):
        return pltpu.make_async_copy(
            stage.at[src_slot, pl.ds(pl.multiple_of(r * tile, tile), nrows * tile), :],
            y_hbm.at[pl.ds(pl.multiple_of(dst_row * tile, tile), nrows * tile), :],
            osem.at[src_slot])

    nv = nv_ref[i]
    a0 = a0_ref[i]
    nxt = jnp.minimum(i + 1, nsteps - 1)
    nv_n = jnp.where(i + 1 < nsteps, nv_ref[nxt], 0)
    a0_n = a0_ref[nxt]
    groups_n = gather_groups(a0_n, nv_n)
    covered_n = covered(nv)

    @pl.when(i == 0)
    def _():
        rows[...] = jnp.zeros_like(rows)
        issue(tok_ref, a0, 0, gather_groups(a0, nv), 0)

    issue(tokn_ref, a0_n, covered_n, groups_n, 1 - slot)

    prev = jnp.maximum(i - 1, 0)
    covered_here = jnp.where(i > 0, covered(nv_ref[prev]), 0)

    def wait_body(g, carry):
        pltpu.make_async_copy(h_hbm.at[pl.ds(0, id_group * SUBLANES), :],
                              rows.at[slot, pl.ds(0, id_group * SUBLANES), :], gsem.at[slot]).wait()
        return carry
    lax.fori_loop(0, jnp.maximum(gather_groups(a0, nv), covered_here), wait_body, 0)

    half = wg_ref.shape[1] // 2

    def piece(row0, m, g0):
        ng = (m // MOE_SUB) * MOE_SUB_GROUPS

        def start_next(k0, k1):
            for k in range(k0, k1):
                start_group(tokn_ref, a0_n, g0 + k, 1 - slot)

        q = ng // 4
        x_lo, x_hi = _unpack_bf16_pair(_load_row_tiles(rows.at[slot], ROW_PAD + row0, m))
        x_lo = x_lo.astype(BF16)
        x_hi = x_hi.astype(BF16)
        start_next(0, q)
        g = _dot_w(x_lo, wg_ref[0, :half, :])
        start_next(q, 2 * q)
        g = g + _dot_w(x_hi, wg_ref[0, half:, :])
        start_next(2 * q, 3 * q)
        u = _dot_w(x_lo, wu_ref[0, :half, :])
        start_next(3 * q, ng)
        u = u + _dot_w(x_hi, wu_ref[0, half:, :])
        act = (_silu(g) * u).astype(BF16)
        yb = _dot_w(act, wd_ref[0]).astype(BF16).astype(F32)
        _store_row_tiles(stage.at[slot], row0, _pack_bf16_pair(yb[:, :half], yb[:, half:]))

    n_big, n_small = pieces(nv)

    def big_piece(p, carry):
        piece(p * (2 * MOE_SUB), 2 * MOE_SUB, p * (2 * MOE_SUB_GROUPS))
        return carry
    lax.fori_loop(0, n_big, big_piece, 0)

    @pl.when(n_small > 0)
    def _():
        piece(n_big * (2 * MOE_SUB), MOE_SUB, n_big * (2 * MOE_SUB_GROUPS))

    nv_p = jnp.where(i > 0, nv_ref[prev], 0)

    def wait_prev_group(g, carry):
        out_copy(0, 0, 1 - slot, id_group).wait()
        return carry
    lax.fori_loop(0, nv_p >> _log2(id_group), wait_prev_group, 0)

    def wait_prev_row(g, carry):
        out_copy(0, 0, 1 - slot).wait()
        return carry
    lax.fori_loop(0, nv_p & (id_group - 1), wait_prev_row, 0)

    w_out0 = a0 & (LANES - 1)

    def start_out_group(g, carry):
        for u in range(id_group):
            r = g * id_group + u
            out_copy(r, tok_ref[w_out0 + r], slot).start()
        return carry
    lax.fori_loop(0, nv >> _log2(id_group), start_out_group, 0)

    def start_out_row(r, carry):
        out_copy(r, tok_ref[w_out0 + r], slot).start()
        return carry
    lax.fori_loop(nv & ~(id_group - 1), nv, start_out_row, 0)


def _moe(block_e, nvalid, a0, sorted_ids, h2p, w_gate, w_up, w_down, n_tok):
    nblk = block_e.shape[0]
    mb = MOE_BLOCK
    ne, d, f = w_gate.shape
    assert d == 2 * SUBLANES * LANES and mb % (2 * MOE_SUB) == 0 and MOE_SUB % GATHER_GROUP == 0
    assert MOE_SUB_GROUPS % 4 == 0
    assert n_tok & (n_tok - 1) == 0, "token id = assignment id & (n_tok - 1)"

    def win(shift):
        def imap(i, be, nv, a0):
            j = jnp.minimum(i + shift, nblk - 1)
            return (pl.multiple_of((a0[j] >> _log2(LANES)) << _log2(LANES), LANES),)
        return pl.BlockSpec((pl.Element(TOK_WINDOW),), imap, memory_space=pltpu.SMEM)

    wspec = lambda r, c: pl.BlockSpec((1, r, c), lambda i, be, nv, a0: (be[i], 0, 0))
    grid_spec = pltpu.PrefetchScalarGridSpec(
        num_scalar_prefetch=3,
        grid=(nblk,),
        in_specs=[win(0), win(1), pl.BlockSpec(memory_space=pl.ANY), wspec(d, f), wspec(d, f), wspec(f, d)],
        out_specs=pl.BlockSpec(memory_space=pl.ANY),
        scratch_shapes=[pltpu.VMEM((2, (mb + 2 * ROW_PAD) * SUBLANES, LANES), U32),
                        pltpu.VMEM((2, mb * SUBLANES, LANES), U32),
                        pltpu.SemaphoreType.DMA((2,)), pltpu.SemaphoreType.DMA((2,))],
    )
    return pl.pallas_call(
        functools.partial(_moe_kernel, n_tok),
        grid_spec=grid_spec,
        out_shape=jax.ShapeDtypeStruct((TOP_K * n_tok * SUBLANES, LANES), U32),
        compiler_params=_cparams(("arbitrary",)),
        name="moe",
    )(block_e, nvalid, a0, sorted_ids, sorted_ids, h2p, w_gate, w_up, w_down)


def _combine_kernel(*refs):
    y_refs = refs[:TOP_K]
    w_ref, xs1_ref, m5_ref, gf_ref, o_ref = refs[TOP_K:]
    tm = o_ref.shape[0]
    w = w_ref[...]
    acc_lo = acc_hi = None
    for k in range(TOP_K):
        lo, hi = _unpack_bf16_pair(_load_row_tiles(y_refs[k], 0, tm))
        wk = w[:, k:k + 1]
        acc_lo = wk * lo if acc_lo is None else acc_lo + wk * lo
        acc_hi = wk * hi if acc_hi is None else acc_hi + wk * hi
    routed = jnp.concatenate([acc_lo, acc_hi], axis=1)
    x = xs1_ref[...] + m5_ref[0] * routed
    ms = jnp.mean(x * x, axis=-1, keepdims=True)
    o_ref[...] = x * lax.rsqrt(ms + EPS) * gf_ref[...]


def _combine(y_packed, w_tok, xs1, m5, g_final, rows_per_batch, tm):
    n, d = xs1.shape
    assert d == 2 * SUBLANES * LANES
    nt = n // tm
    tpb = rows_per_batch // tm
    y_spec = lambda k: pl.BlockSpec((tm * SUBLANES, LANES), lambda i: (k * nt + i, 0))
    return pl.pallas_call(
        _combine_kernel,
        grid=(nt,),
        in_specs=[y_spec(k) for k in range(TOP_K)]
        + [pl.BlockSpec((tm, TOP_K), lambda i: (i, 0)),
           pl.BlockSpec((tm, d), lambda i: (i, 0)),
           pl.BlockSpec((1, 1, d), lambda i: (i // tpb, 0, 0)),
           pl.BlockSpec((1, d), lambda i: (0, 0))],
        out_specs=pl.BlockSpec((tm, d), lambda i: (i, 0)),
        out_shape=jax.ShapeDtypeStruct((n, d), F32),
        compiler_params=_cparams(("parallel",)),
        name="combine",
    )(*([y_packed] * TOP_K), w_tok, xs1, m5, g_final.reshape(1, d))


def _dispatch_plan(top_e, counts, n_tok):
    mb = MOE_BLOCK
    ne = counts.shape[0]
    n_asg = TOP_K * n_tok
    nblk = -(-(n_asg + ne * (mb - 1)) // mb) + 1
    asg_ids = jnp.arange(n_asg, dtype=I32).reshape(top_e.shape)
    keys = jnp.sort((top_e * n_asg + asg_ids).reshape(-1))
    sorted_ids = jnp.concatenate([keys % n_asg, jnp.zeros((TOK_WINDOW,), I32)])
    start = jnp.cumsum(counts) - counts
    eblk = (counts + mb - 1) // mb
    pend = jnp.cumsum(eblk)
    pstart = pend - eblk
    n_real = pend[-1]
    bid = jnp.arange(nblk, dtype=I32)
    last_real = jnp.maximum(n_real - 1, 0)
    bsrc = jnp.minimum(bid, last_real)
    block_e = jnp.minimum(jnp.sum((pend[None, :] <= bsrc[:, None]).astype(I32), axis=1), ne - 1)
    onehot = block_e[:, None] == jnp.arange(ne, dtype=I32)[None, :]
    look = lambda table: jnp.sum(jnp.where(onehot, table[None, :], 0), axis=1)
    off = (bsrc - look(pstart)) * mb
    nvalid = jnp.where(bid < n_real, jnp.clip(look(counts) - off, 0, mb), 0).astype(I32)
    a0 = (look(start) + off).astype(I32)
    return block_e.astype(I32), nvalid, a0, sorted_ids


def kernel(x, c, ctx, c_ctx, w_ada, b_ada, g_mix, w_in, conv_w, conv_b, dt_bias, a_log, d_skip, g_ssd,
           pool_w, pool_scale, w_out, g_ffn, w_router, router_bias, w_exp_gate, w_exp_up, w_exp_down,
           w_sh_gate, w_sh_up, w_sh_down, g_final):
    bsz, seq, d = x.shape
    ctx_len = ctx.shape[1]
    assert w_ada.shape[0] == 1, "single-layer block"
    d_ssm = g_ssd.shape[1]
    heads = d_skip.shape[1]
    d_pool = pool_scale.shape[1]
    d_xbc = conv_w.shape[2]
    n = bsz * seq

    cvec = jnp.zeros((8, d), F32).at[:bsz].set(c).at[bsz].set(c_ctx)
    mod_all = _ada(cvec, w_ada[0], b_ada[0]).reshape(8, N_MOD, d)
    mod = mod_all[:bsz]
    mod_c = mod_all[bsz:bsz + 1]
    mk = lambda m, k: m[:, k:k + 1, :]

    wt = jnp.transpose(w_in[0])
    c_dt = d_ssm + d_xbc
    wtb = wt.astype(BF16)
    w_pl = wtb[c_dt + 2 * heads:]
    w_dt = wt[c_dt:c_dt + 2 * heads].reshape(2, SSM_GROUPS, heads // SSM_GROUPS, d)
    w_dt = jnp.transpose(w_dt, (1, 0, 2, 3)).reshape(2 * heads, d)
    w_dt = jnp.pad(w_dt, ((0, LANES - 2 * heads), (0, 0)))

    main_c, dt_c = _inproj(ctx.reshape(bsz * ctx_len, d), g_mix[0], mk(mod_c, 0), mk(mod_c, 1),
                           wtb, c_dt, w_pl, w_dt, bsz * ctx_len, tm=256, tn=1024)
    h_zero = jnp.zeros((bsz, SSM_GROUPS, 2, SSM_STATE, 4 * SSM_HEADDIM), F32)
    h_ctx = _ssd(main_c.reshape(bsz, ctx_len, -1), dt_c, conv_w[0], conv_b[0],
                 dt_bias[0], a_log[0], None, h_zero, False, d_ssm, d_ssm)

    x2d = x.reshape(n, d)
    main, dt_raw = _inproj(x2d, g_mix[0], mk(mod, 0), mk(mod, 1), wtb, c_dt, w_pl, w_dt, seq,
                           tm=min(1024, seq), tn=1024)
    main3 = main.reshape(bsz, seq, -1)
    dsk = jnp.repeat(d_skip[0].astype(F32), SSM_HEADDIM).reshape(1, d_ssm)
    yg = _ssd(main3, dt_raw, conv_w[0], conv_b[0], dt_bias[0], a_log[0], dsk, h_ctx, True, d_ssm, d_ssm)
    yp = _pool(main3, pool_w[0], pool_scale[0], d_ssm + d_xbc)

    wo = w_out[0].astype(BF16)
    wr = w_router[0].T
    wr_hi = wr.astype(BF16)
    wr_lo = (wr - wr_hi.astype(F32)).astype(BF16)
    xs1, h2p, lg_t = _outproj(
        yg.reshape(n, d_ssm), yp.reshape(n, d_pool), x2d, g_ssd[0],
        (mk(mod, 2), mk(mod, 3), mk(mod, 4), mk(mod, 5)), g_ffn[0],
        wo[:d_ssm], wo[d_ssm:], wr_hi, wr_lo,
        w_sh_gate[0].astype(BF16), w_sh_up[0].astype(BF16), w_sh_down[0].astype(BF16), seq, tm=256)

    top_e, top_w, cnt = _topk(lg_t, router_bias[0], tm=512)
    counts = cnt[:, 0].astype(I32)
    block_e, nvalid, a0, sorted_ids = _dispatch_plan(top_e, counts, n)
    y_packed = _moe(block_e, nvalid, a0, sorted_ids, h2p, w_exp_gate[0], w_exp_up[0], w_exp_down[0], n)
    out = _combine(y_packed, top_w.T, xs1, mk(mod, 5), g_final, seq, tm=256)
    return out.reshape(bsz, seq, d)
```

```python
import functools

import jax
import jax.numpy as jnp
from jax import lax
from jax.experimental import pallas as pl
from jax.experimental.pallas import tpu as pltpu

F32 = jnp.float32
BF16 = jnp.bfloat16
I32 = jnp.int32
U32 = jnp.uint32

EPS = 1e-6
GRID_W = 64
SSM_HEADDIM = 64
SSM_GROUPS = 8
SSM_STATE = 128
SSD_CHUNK = 128
POOL_WINDOWS = (2, 4, 8, 16)
TOP_K = 8
N_EXPERT_GROUPS = 8
TOPK_GROUPS = 4
ROUTE_SCALE = 2.5
N_MOD = 6
LANES = 128
SUBLANES = 8
BF16_ROWS = 16
CONV_TILE = 128
MOE_BLOCK = 1024
GATHER_GROUP = 64
MOE_SUB = 128
MOE_SUB_GROUPS = 16
ROW_PAD = SUBLANES
TOK_WINDOW = pl.next_power_of_2(LANES + MOE_BLOCK + 2 * ROW_PAD)
V7X_VMEM_LIMIT = 56 * 1024 * 1024


def _cparams(sem, vmem=V7X_VMEM_LIMIT):
    return pltpu.CompilerParams(dimension_semantics=sem, vmem_limit_bytes=vmem)


def _sigmoid(x):
    return 1.0 / (1.0 + jnp.exp(-x))


def _silu(x):
    return x * _sigmoid(x)


def _split2(x):
    hi = x.astype(BF16)
    lo = (x - hi.astype(F32)).astype(BF16)
    return hi, lo


def _dot(a, b):
    return jnp.dot(a, b, preferred_element_type=F32)


def _dot_nt(a, b):
    return lax.dot_general(a, b, (((1,), (1,)), ((), ())), preferred_element_type=F32)


def _dot_tn(a, b):
    return lax.dot_general(a, b, (((0,), (0,)), ((), ())), preferred_element_type=F32)


def _dot_w(a, w):
    return lax.dot_general(a, w, (((1,), (0,)), ((), ())), preferred_element_type=F32)


def _dot3(a, b):
    a_hi, a_lo = _split2(a)
    b_hi, b_lo = _split2(b)
    return _dot(a_hi, b_hi) + _dot(a_lo, b_hi) + _dot(a_hi, b_lo)


def _ada_kernel(c_ref, w_ref, b_ref, o_ref):
    c = c_ref[...]
    o_ref[...] = _dot3(_silu(c), w_ref[...]) + b_ref[...]


def _ada(cvec, w_ada, b_ada, tn=1024):
    d, n = w_ada.shape
    return pl.pallas_call(
        _ada_kernel,
        grid=(n // tn,),
        in_specs=[pl.BlockSpec((8, d), lambda j: (0, 0)),
                  pl.BlockSpec((d, tn), lambda j: (0, j)),
                  pl.BlockSpec((1, tn), lambda j: (0, j))],
        out_specs=pl.BlockSpec((8, tn), lambda j: (0, j)),
        out_shape=jax.ShapeDtypeStruct((8, n), F32),
        compiler_params=_cparams(("parallel",)),
        name="ada",
    )(cvec, w_ada, b_ada.reshape(1, n))


def _inproj_kernel(na, x_ref, g_ref, sh_ref, sc_ref, wa_ref, wb_ref, wdt_ref, o_ref, dt_ref, h_scr):
    j = pl.program_id(1)

    @pl.when(j == 0)
    def _():
        x = x_ref[...]
        ms = jnp.mean(x * x, axis=-1, keepdims=True)
        y = x * lax.rsqrt(ms + EPS) * g_ref[...]
        h = y * (1.0 + sc_ref[0]) + sh_ref[0]
        h_hi, h_lo = _split2(h)
        h_scr[...] = h_hi
        w_hi, w_lo = _split2(wdt_ref[...])
        dt_ref[...] = _dot_nt(w_hi, h_hi) + _dot_nt(w_lo, h_hi) + _dot_nt(w_hi, h_lo)

    @pl.when(j < na)
    def _():
        o_ref[...] = _dot_nt(h_scr[...], wa_ref[...]).astype(o_ref.dtype)

    @pl.when(j >= na)
    def _():
        o_ref[...] = _dot_nt(h_scr[...], wb_ref[...]).astype(o_ref.dtype)


def _inproj(x2d, g, shift, scale, w_a_t, ca, w_b_t, w_dt_t, rows_per_batch, tm, tn, a_rows=None):
    n, d = x2d.shape
    na, nbk = ca // tn, w_b_t.shape[0] // tn
    assert ca % tn == 0 and w_b_t.shape[0] % tn == 0
    j0 = 0
    if a_rows is not None:
        assert a_rows[0] % tn == 0 and a_rows[1] % tn == 0 and a_rows[1] <= ca
        j0, na, nbk = a_rows[0] // tn, (a_rows[1] - a_rows[0]) // tn, 0
    nc = (na + nbk) * tn
    tpb = rows_per_batch // tm
    return pl.pallas_call(
        functools.partial(_inproj_kernel, na),
        grid=(n // tm, nc // tn),
        in_specs=[pl.BlockSpec((tm, d), lambda i, j: (i, 0)),
                  pl.BlockSpec((1, d), lambda i, j: (0, 0)),
                  pl.BlockSpec((1, 1, d), lambda i, j: (i // tpb, 0, 0)),
                  pl.BlockSpec((1, 1, d), lambda i, j: (i // tpb, 0, 0)),
                  pl.BlockSpec((tn, d), lambda i, j: (j0 + jnp.minimum(j, na - 1), 0)),
                  pl.BlockSpec((tn, d), lambda i, j: (jnp.maximum(j - na, 0), 0)),
                  pl.BlockSpec((LANES, d), lambda i, j: (0, 0))],
        out_specs=[pl.BlockSpec((tm, tn), lambda i, j: (i, j)),
                   pl.BlockSpec((LANES, tm), lambda i, j: (0, i))],
        out_shape=[jax.ShapeDtypeStruct((n, nc), BF16),
                   jax.ShapeDtypeStruct((LANES, n), F32)],
        scratch_shapes=[pltpu.VMEM((tm, d), BF16)],
        compiler_params=_cparams(("parallel", "arbitrary")),
        name="inproj",
    )(x2d, g.reshape(1, d), shift, scale, w_a_t, w_b_t, w_dt_t)


def _expand_heads(v, base, width):
    t = v.shape[0]
    lane = lax.broadcasted_iota(I32, (t, 4 * width), 1)
    out = jnp.broadcast_to(v[:, base + 3:base + 4], (t, 4 * width))
    for j in (2, 1, 0):
        out = jnp.where(lane < (j + 1) * width, v[:, base + j:base + j + 1], out)
    return out


def _conv_silu(src_ref, w_ref, b_ref, dst_ref, length):
    c = src_ref.shape[-1]
    tile, grp = CONV_TILE, BF16_ROWS
    w = w_ref[...]
    b = b_ref[...]
    rid = lax.broadcasted_iota(I32, (tile, c), 0)

    def body(k, carry):
        r0 = pl.multiple_of(k * tile, tile)
        cur = src_ref[0, pl.ds(r0, tile), :].astype(F32)
        lo = pl.multiple_of(jnp.maximum(r0 - grp, 0), grp)
        hi = pl.multiple_of(jnp.minimum(r0 + tile, length - grp), grp)
        prev_row = src_ref[0, pl.ds(lo, grp), :].astype(F32)[grp - 1:grp, :]
        next_row = src_ref[0, pl.ds(hi, grp), :].astype(F32)[0:1, :]
        prev_row = jnp.where(r0 > 0, prev_row, 0.0)
        next_row = jnp.where(r0 + tile < length, next_row, 0.0)
        up = jnp.where(rid == 0, prev_row, pltpu.roll(cur, 1, 0))
        dn = jnp.where(rid == tile - 1, next_row, pltpu.roll(cur, tile - 1, 0))
        o = up * w[0:1, :] + cur * w[1:2, :] + dn * w[2:3, :] + b
        dst_ref[pl.ds(r0, tile), :] = _silu(o).astype(dst_ref.dtype)
        return carry
    lax.fori_loop(0, length // tile, body, 0, unroll=2)


def _softplus(x):
    return jnp.maximum(x, 0.0) + jnp.log(1.0 + jnp.exp(-jnp.abs(x)))


def _ssd_kernel(need_y, length, *refs):
    t = SSD_CHUNK
    nch = length // t
    hw = 4 * SSM_HEADDIM
    if need_y:
        (xr, br, cr, zr, cwx, cbx, cwb, cbb, cwc, cbc, dtr, bias_r, a_r, dsk, h0,
         y_ref, xs_s, b_s, c_s, e_s, dtr_s, csr_s, csc_s, aux_s, st_s, dec_s) = refs
    else:
        (xr, br, cwx, cbx, cwb, cbb, dtr, bias_r, a_r, h0,
         hfin, xs_s, b_s, dtr_s, csr_s, csc_s, aux_s, st_s, dec_s) = refs

    _conv_silu(xr, cwx, cbx, xs_s, length)
    _conv_silu(br, cwb, cbb, b_s, length)
    if need_y:
        _conv_silu(cr, cwc, cbc, c_s, length)

    dt_r = _softplus(dtr[0, 0] + bias_r[0])
    dtr_s[...] = dt_r
    da2 = (dt_r * a_r[0]).reshape(nch * 8, t)
    kk = lax.broadcasted_iota(I32, (t, 2 * t), 0)
    ll = lax.broadcasted_iota(I32, (t, 2 * t), 1)
    tri = jnp.where(ll < t, jnp.where(kk <= ll, 1.0, 0.0), jnp.where(kk >= ll - t, 1.0, 0.0)).astype(BF16)
    p0 = da2.astype(BF16)
    r1 = da2 - p0.astype(F32)
    p1 = r1.astype(BF16)
    p2 = (r1 - p1.astype(F32)).astype(BF16)
    cum = _dot(p0, tri) + _dot(p1, tri) + _dot(p2, tri)
    rowj = lax.broadcasted_iota(I32, (nch * 8, t), 0) & 7
    csr = jnp.where(rowj < 4, cum[:, :t], cum[:, t:])
    csr_s[...] = csr.reshape(nch, 8, t)

    edge = jnp.where(rowj[:, 0:1] < 4, csr[:, t - 1:t], csr[:, 0:1])
    aux = [csr, dt_r.reshape(nch * 8, t) * jnp.exp(edge - csr), jnp.exp(csr),
           jnp.broadcast_to(jnp.exp(edge), (nch * 8, t))]
    for k, v in enumerate(aux):
        aux_s[:, 8 * k:8 * (k + 1), :] = v.reshape(nch, 8, t)

    spread = ((lax.broadcasted_iota(I32, (8, 2 * hw), 1) >> _log2(SSM_HEADDIM))
              == lax.broadcasted_iota(I32, (8, 2 * hw), 0))
    spread = jnp.where(spread, 1.0, 0.0).astype(BF16)

    def spread_heads(v):
        hi, lo = _split2(v)
        return _dot(hi, spread) + _dot(lo, spread)

    def phase_a(c, carry):
        r0 = pl.multiple_of(c * t, t)
        cols = jnp.concatenate([aux_s[c], jnp.zeros((t - 8 * len(aux), t), F32)], axis=0).T
        csc_s[c] = cols[:, 0:8]
        wcol = cols[:, 8:16]
        if need_y:
            e_s[c] = cols[:, 16:24]
        dec = cols[0:1, 24:32]
        xs = xs_s[pl.ds(r0, t), :]
        bc = b_s[pl.ds(r0, t), :]
        w_e = spread_heads(wcol)
        xw_f = (xs * w_e[:, :hw]).astype(BF16)
        xw_b = (xs * w_e[:, hw:]).astype(BF16)
        st_s[c, 0] = _dot_tn(bc, xw_f)
        st_s[c, 1] = _dot_tn(bc, xw_b)
        dec_s[c, 0:1, :] = _expand_heads(dec, 0, SSM_HEADDIM)
        dec_s[c, 1:2, :] = _expand_heads(dec, 4, SSM_HEADDIM)
        return carry

    lax.fori_loop(0, nch, phase_a, 0, unroll=4 if nch % 4 == 0 else 2)

    def rec_f(c, s):
        loc = st_s[c, 0]
        st_s[c, 0] = s
        return dec_s[c, 0:1, :] * s + loc

    def rec_b(k, s):
        c = nch - 1 - k
        loc = st_s[c, 1]
        st_s[c, 1] = s
        return dec_s[c, 1:2, :] * s + loc

    s_f = lax.fori_loop(0, nch, rec_f, h0[0, 0, 0])
    s_b = lax.fori_loop(0, nch, rec_b, h0[0, 0, 1])
    if not need_y:
        hfin[0, 0, 0] = s_f
        hfin[0, 0, 1] = s_b
        return

    li = lax.broadcasted_iota(I32, (t, t), 0)
    si = lax.broadcasted_iota(I32, (t, t), 1)
    dskip = dsk[...]

    def phase_c(c, carry):
        r0 = pl.multiple_of(c * t, t)
        csr_c = csr_s[c]
        csc_c = csc_s[c]
        dtr_c = dtr_s[c]
        e_c = e_s[c]
        xs = xs_s[pl.ds(r0, t), :]
        xs_b = xs.astype(BF16)
        bc = b_s[pl.ds(r0, t), :]
        cc = c_s[pl.ds(r0, t), :]
        cb = _dot_nt(cc, bc)
        ys = []
        for j in range(4):
            d_f = csc_c[:, j:j + 1] - csr_c[j:j + 1, :]
            att_f = jnp.where(si <= li, (cb * dtr_c[j:j + 1, :]) * jnp.exp(jnp.minimum(d_f, 0.0)), 0.0)
            d_b = csc_c[:, 4 + j:5 + j] - csr_c[4 + j:5 + j, :]
            att_b = jnp.where(si >= li, (cb * dtr_c[4 + j:5 + j, :]) * jnp.exp(jnp.minimum(d_b, 0.0)), 0.0)
            lhs = jnp.concatenate([att_f, att_b], axis=1).astype(BF16)
            xj = xs_b[:, j * SSM_HEADDIM:(j + 1) * SSM_HEADDIM]
            ys.append(_dot(lhs, jnp.concatenate([xj, xj], axis=0)))
        y = jnp.concatenate(ys, axis=1)
        e_e = spread_heads(e_c)
        y = y + _dot(cc, st_s[c, 0].astype(BF16)) * e_e[:, :hw]
        y = y + _dot(cc, st_s[c, 1].astype(BF16)) * e_e[:, hw:]
        y = y + xs * dskip
        z = zr[0, pl.ds(r0, t), :].astype(F32)
        y_ref[0, pl.ds(r0, t), :] = (y * _silu(z)).astype(y_ref.dtype)
        return carry

    lax.fori_loop(0, nch, phase_c, 0, unroll=4 if nch % 4 == 0 else 2)


def _ssd(main3, dt_t, conv_w, conv_b, dt_bias, a_log, d_skip, h0, need_y, d_ssm, xbc_col0):
    nb, length, _ = main3.shape
    g_n, t = SSM_GROUPS, SSD_CHUNK
    nch = length // t
    heads = d_ssm // SSM_HEADDIM
    hpg = heads // g_n
    assert hpg == 4 and SSM_STATE == 128
    hw = hpg * SSM_HEADDIM
    dtr = jnp.transpose(dt_t[:2 * heads].reshape(g_n, 2 * hpg, nb, nch, t), (2, 0, 3, 1, 4))
    par = lambda p: jnp.transpose(p.reshape(2, g_n, hpg), (1, 0, 2)).reshape(g_n, 8)
    bias_r = par(dt_bias.astype(F32)).reshape(g_n, 8, 1)
    a_r = par(-jnp.exp(a_log.astype(F32))).reshape(g_n, 8, 1)
    nx = d_ssm
    nbc = g_n * SSM_STATE
    cwx, cwb, cwc = conv_w[:, :nx], conv_w[:, nx:nx + nbc], conv_w[:, nx + nbc:]
    cb2 = conv_b.reshape(1, -1)
    cbx, cbb, cbc = cb2[:, :nx], cb2[:, nx:nx + nbc], cb2[:, nx + nbc:]
    xcol = xbc_col0 // hw
    bcol = (xbc_col0 + nx) // SSM_STATE
    ccol = (xbc_col0 + nx + nbc) // SSM_STATE

    seq = lambda shape, imap: pl.BlockSpec(shape, imap)
    x_spec = seq((1, length, hw), lambda b, g: (b, 0, xcol + g))
    b_spec = seq((1, length, SSM_STATE), lambda b, g: (b, 0, bcol + g))
    c_spec = seq((1, length, SSM_STATE), lambda b, g: (b, 0, ccol + g))
    z_spec = seq((1, length, hw), lambda b, g: (b, 0, g))
    wx_spec = [seq((3, hw), lambda b, g: (0, g)), seq((1, hw), lambda b, g: (0, g))]
    wn_spec = [seq((3, SSM_STATE), lambda b, g: (0, g)), seq((1, SSM_STATE), lambda b, g: (0, g))]
    dt_specs = [seq((1, 1, nch, 8, t), lambda b, g: (b, g, 0, 0, 0)),
                seq((1, 8, 1), lambda b, g: (g, 0, 0)), seq((1, 8, 1), lambda b, g: (g, 0, 0))]
    h_spec = seq((1, 1, 2, SSM_STATE, hw), lambda b, g: (b, g, 0, 0, 0))
    common_scr = [pltpu.VMEM((nch, 8, t), F32),
                  pltpu.VMEM((nch, 8, t), F32),
                  pltpu.VMEM((nch, t, 8), F32),
                  pltpu.VMEM((nch, 32, t), F32),
                  pltpu.VMEM((nch, 2, SSM_STATE, hw), F32),
                  pltpu.VMEM((nch, 8, hw), F32)]
    if need_y:
        in_specs = ([x_spec, b_spec, c_spec, z_spec] + wx_spec + wn_spec + wn_spec + dt_specs
                    + [seq((1, hw), lambda b, g: (0, g)), h_spec])
        args = (main3, main3, main3, main3, cwx, cbx, cwb, cbb, cwc, cbc, dtr, bias_r, a_r, d_skip, h0)
        out_specs = seq((1, length, hw), lambda b, g: (b, 0, g))
        out_shape = jax.ShapeDtypeStruct((nb, length, d_ssm), BF16)
        scratch = [pltpu.VMEM((length, hw), F32), pltpu.VMEM((length, SSM_STATE), BF16),
                   pltpu.VMEM((length, SSM_STATE), BF16),
                   pltpu.VMEM((nch, t, 8), F32)] + common_scr
    else:
        in_specs = [x_spec, b_spec] + wx_spec + wn_spec + dt_specs + [h_spec]
        args = (main3, main3, cwx, cbx, cwb, cbb, dtr, bias_r, a_r, h0)
        out_specs = h_spec
        out_shape = jax.ShapeDtypeStruct((nb, g_n, 2, SSM_STATE, hw), F32)
        scratch = [pltpu.VMEM((length, hw), F32), pltpu.VMEM((length, SSM_STATE), BF16)] + common_scr
    return pl.pallas_call(
        functools.partial(_ssd_kernel, need_y, length),
        grid=(nb, g_n),
        in_specs=in_specs,
        out_specs=out_specs,
        out_shape=out_shape,
        scratch_shapes=scratch,
        compiler_params=_cparams(("parallel", "parallel")),
        name="ssd_y" if need_y else "ssd_state",
    )(*args)


def _pool_kernel(length, u_ref, pw_ref, ps_ref, o_ref, pad_s):
    gw = GRID_W
    rows = length // gw
    cg = pw_ref.shape[-1]
    halo = (max(POOL_WINDOWS) // 2) * gw
    tile = min(256, length)
    zeros = jnp.zeros((halo, cg), F32)
    pad_s[0:halo, :] = zeros
    pad_s[halo + length:halo + length + halo, :] = zeros
    for g, w in enumerate(POOL_WINDOWS):
        lo = -(w // 2)
        csl = slice(g * cg, (g + 1) * cg)
        for r0 in range(0, length, tile):
            pad_s[halo + r0:halo + r0 + tile, :] = u_ref[0, r0:r0 + tile, csl].astype(F32)
        pw = pw_ref[g]
        scale = ps_ref[:, csl]

        def body(i, carry, lo=lo, w=w, csl=csl, pw=pw, scale=scale):
            r0 = pl.multiple_of(i * tile, tile)
            acc = pad_s[pl.ds(halo + r0 + lo * gw, tile), :]
            for k in range(lo + 1, lo + w):
                acc = acc + pad_s[pl.ds(halo + r0 + k * gw, tile), :]
            l_idx = r0 + lax.broadcasted_iota(I32, (tile, cg), 0)
            ii = l_idx >> (gw.bit_length() - 1)
            jj = l_idx & (gw - 1)
            tot = acc
            for k in range(lo, lo + w):
                if k == 0:
                    continue
                sh = pltpu.roll(acc, (-k) % tile, 0)
                ok = (jj >= -k) if k < 0 else (jj < gw - k)
                tot = tot + jnp.where(ok, sh, 0.0)
            cnt_i = jnp.minimum(ii + lo + w, rows) - jnp.maximum(ii + lo, 0)
            cnt_j = jnp.minimum(jj + lo + w, gw) - jnp.maximum(jj + lo, 0)
            u = pad_s[pl.ds(halo + r0, tile), :]
            d = tot / (cnt_i * cnt_j).astype(F32) - u
            y = _dot(d.astype(BF16), pw) * scale
            o_ref[0, pl.ds(r0, tile), csl] = y.astype(o_ref.dtype)
            return carry

        lax.fori_loop(0, length // tile, body, 0)


def _pool(main3, pool_w, pool_scale, col0):
    nb, length, _ = main3.shape
    ng, cg, _ = pool_w.shape
    dp = ng * cg
    halo = (max(POOL_WINDOWS) // 2) * GRID_W
    return pl.pallas_call(
        functools.partial(_pool_kernel, length),
        grid=(nb,),
        in_specs=[pl.BlockSpec((1, length, dp), lambda b: (b, 0, col0 // dp)),
                  pl.BlockSpec((ng, cg, cg), lambda b: (0, 0, 0)),
                  pl.BlockSpec((1, dp), lambda b: (0, 0))],
        out_specs=pl.BlockSpec((1, length, dp), lambda b: (b, 0, 0)),
        out_shape=jax.ShapeDtypeStruct((nb, length, dp), BF16),
        scratch_shapes=[pltpu.VMEM((length + 2 * halo, cg), F32)],
        compiler_params=_cparams(("parallel",)),
        name="pool",
    )(main3, pool_w.astype(BF16), pool_scale.reshape(1, dp))


def _pack_bf16_pair(lo, hi):
    lo_b = lax.bitcast_convert_type(lo, U32) >> 16
    hi_b = lax.bitcast_convert_type(hi, U32) & jnp.uint32(0xFFFF0000)
    return lo_b | hi_b


def _unpack_bf16_pair(p):
    lo = lax.bitcast_convert_type(p << 16, F32)
    hi = lax.bitcast_convert_type(p & jnp.uint32(0xFFFF0000), F32)
    return lo, hi


def _store_row_tiles(ref, row0, value):
    m = value.shape[0]
    for c in range(SUBLANES):
        ref[pl.ds(row0 * SUBLANES + c, m, stride=SUBLANES), :] = value[:, c * LANES:(c + 1) * LANES]


def _load_row_tiles(ref, row0, m):
    return jnp.concatenate(
        [ref[pl.ds(row0 * SUBLANES + c, m, stride=SUBLANES), :] for c in range(SUBLANES)], axis=1)


def _outproj_kernel(yg_ref, yp_ref, x_ref, gs_ref, m2_ref, m3_ref, m4_ref, m5_ref, gf_ref,
                    ws_ref, wp_ref, wrh_ref, wrl_ref, wg_ref, wu_ref, wd_ref,
                    xs1_ref, h2p_ref, lg_ref):
    yg = yg_ref[...].astype(F32)
    ms = jnp.mean(yg * yg, axis=-1, keepdims=True)
    a = (yg * lax.rsqrt(ms + EPS) * gs_ref[...]).astype(BF16)
    o = _dot(a, ws_ref[...]) + _dot(yp_ref[...], wp_ref[...])
    x1 = x_ref[...] + m2_ref[0] * o
    ms1 = jnp.mean(x1 * x1, axis=-1, keepdims=True)
    h2 = x1 * lax.rsqrt(ms1 + EPS) * gf_ref[...]
    h2 = h2 * (1.0 + m4_ref[0]) + m3_ref[0]
    h_hi, h_lo = _split2(h2)
    wrh = wrh_ref[...]
    lg_ref[...] = _dot_nt(wrh, h_hi) + _dot_nt(wrl_ref[...], h_hi) + _dot_nt(wrh, h_lo)
    act = (_silu(_dot(h_hi, wg_ref[...])) * _dot(h_hi, wu_ref[...])).astype(BF16)
    xs1_ref[...] = x1 + m5_ref[0] * _dot(act, wd_ref[...])
    hf = h_hi.astype(F32)
    half = hf.shape[1] // 2
    _store_row_tiles(h2p_ref, 0, _pack_bf16_pair(hf[:, :half], hf[:, half:]))


def _outproj(yg, yp, x2d, g_ssd, mods, g_ffn, w_ssd, w_pool, wr_hi, wr_lo, wsg, wsu, wsd,
             rows_per_batch, tm):
    n, d = x2d.shape
    dp = yp.shape[1]
    ne = wr_hi.shape[0]
    dsh = wsg.shape[1]
    tpb = rows_per_batch // tm
    row = lambda c: pl.BlockSpec((tm, c), lambda i: (i, 0))
    vec = pl.BlockSpec((1, d), lambda i: (0, 0))
    mod = pl.BlockSpec((1, 1, d), lambda i: (i // tpb, 0, 0))
    res = lambda r, c: pl.BlockSpec((r, c), lambda i: (0, 0), pipeline_mode=pl.Buffered(1))
    m2, m3, m4, m5 = mods
    return pl.pallas_call(
        _outproj_kernel,
        grid=(n // tm,),
        in_specs=[row(d), row(dp), row(d), vec, mod, mod, mod, mod, vec,
                  res(d, d), res(dp, d), res(ne, d), res(ne, d), res(d, dsh), res(d, dsh), res(dsh, d)],
        out_specs=[row(d), pl.BlockSpec((tm * SUBLANES, LANES), lambda i: (i, 0)),
                   pl.BlockSpec((ne, tm), lambda i: (0, i))],
        out_shape=[jax.ShapeDtypeStruct((n, d), F32),
                   jax.ShapeDtypeStruct((n * SUBLANES, LANES), U32),
                   jax.ShapeDtypeStruct((ne, n), F32)],
        compiler_params=_cparams(("parallel",)),
        name="outproj",
    )(yg, yp, x2d, g_ssd.reshape(1, d), m2, m3, m4, m5, g_ffn.reshape(1, d),
      w_ssd, w_pool, wr_hi, wr_lo, wsg, wsu, wsd)


def _topk_kernel(lg_ref, rb_ref, te_ref, tw_ref, cnt_ref, carry):
    ne, tm = lg_ref.shape
    gsz = ne // N_EXPERT_GROUPS

    @pl.when(pl.program_id(0) == 0)
    def _():
        carry[...] = jnp.zeros_like(carry)

    s = _sigmoid(lg_ref[...])
    biased = s + rb_ref[...]
    neg = jnp.float32(-jnp.inf)
    big = jnp.int32(1 << 20)
    gi = lax.broadcasted_iota(I32, (gsz, tm), 0)
    gscore = []
    for g in range(N_EXPERT_GROUPS):
        v = biased[g * gsz:(g + 1) * gsz, :]
        m1 = jnp.max(v, axis=0, keepdims=True)
        i1 = jnp.min(jnp.where(v == m1, gi, big), axis=0, keepdims=True)
        m2 = jnp.max(jnp.where(gi == i1, neg, v), axis=0, keepdims=True)
        gscore.append(m1 + m2)
    parts = []
    for g in range(N_EXPERT_GROUPS):
        rank = jnp.zeros((1, tm), I32)
        for h in range(N_EXPERT_GROUPS):
            if h == g:
                continue
            ahead = (gscore[h] > gscore[g]) | ((gscore[h] == gscore[g]) & (h < g))
            rank = rank + ahead.astype(I32)
        keep = rank < TOPK_GROUPS
        parts.append(jnp.where(keep, biased[g * gsz:(g + 1) * gsz, :], neg))
    masked = jnp.concatenate(parts, axis=0)
    ei = lax.broadcasted_iota(I32, (ne, tm), 0)
    idxs, wts = [], []
    msel = jnp.zeros((ne, tm), F32)
    for _ in range(TOP_K):
        m = jnp.max(masked, axis=0, keepdims=True)
        idx = jnp.min(jnp.where(masked == m, ei, big), axis=0, keepdims=True)
        sel = ei == idx
        wts.append(jnp.sum(jnp.where(sel, s, 0.0), axis=0, keepdims=True))
        idxs.append(idx)
        masked = jnp.where(sel, neg, masked)
        msel = jnp.where(sel, 1.0, msel)
    wsum = wts[0]
    for w in wts[1:]:
        wsum = wsum + w
    for k in range(TOP_K):
        te_ref[k:k + 1, :] = idxs[k]
        tw_ref[k:k + 1, :] = wts[k] / wsum * ROUTE_SCALE
    total = carry[...] + _dot(msel.astype(BF16), jnp.ones((tm, LANES), BF16))
    carry[...] = total
    cnt_ref[...] = total


def _topk(lg_t, router_bias, tm):
    ne, n = lg_t.shape
    row8 = lambda dt: jax.ShapeDtypeStruct((TOP_K, n), dt)
    return pl.pallas_call(
        _topk_kernel,
        grid=(n // tm,),
        in_specs=[pl.BlockSpec((ne, tm), lambda i: (0, i)),
                  pl.BlockSpec((ne, 1), lambda i: (0, 0))],
        out_specs=[pl.BlockSpec((TOP_K, tm), lambda i: (0, i))] * 2
        + [pl.BlockSpec((ne, LANES), lambda i: (0, 0))],
        out_shape=[row8(I32), row8(F32), jax.ShapeDtypeStruct((ne, LANES), F32)],
        scratch_shapes=[pltpu.VMEM((ne, LANES), F32)],
        compiler_params=_cparams(("arbitrary",)),
        name="topk",
    )(lg_t, router_bias.reshape(ne, 1).astype(F32))


def _log2(n):
    assert n & (n - 1) == 0
    return n.bit_length() - 1


def _gather_groups(nvalid):
    return (nvalid + (GATHER_GROUP - 1)) >> _log2(GATHER_GROUP)


def _gather_rows(nvalid):
    return _gather_groups(nvalid) << _log2(GATHER_GROUP)


def _moe_kernel(n_tok, be_ref, nv_ref, a0_ref, tok_ref, tokn_ref, h_hbm, wg_ref, wu_ref, wd_ref, y_hbm,
                rows, stage, gsem, osem):
    del be_ref
    i = pl.program_id(0)
    nsteps = pl.num_programs(0)
    slot = i % 2
    id_group = SUBLANES
    tile = SUBLANES

    def start_group(tref, a0, g, dst):
        lead = a0 & (id_group - 1)
        w = (a0 & (LANES - 1)) - lead + g * id_group
        for u in range(id_group):
            tok = tref[w + u] & (n_tok - 1)
            dst_row = ROW_PAD - lead + g * id_group + u
            pltpu.make_async_copy(
                h_hbm.at[pl.ds(pl.multiple_of(tok * SUBLANES, SUBLANES), SUBLANES), :],
                rows.at[dst, pl.ds(pl.multiple_of(dst_row * SUBLANES, SUBLANES), SUBLANES), :],
                gsem.at[dst]).start()

    def issue(tref, a0, g_lo, g_hi, dst):
        def body(g, carry):
            start_group(tref, a0, g, dst)
            return carry
        lax.fori_loop(g_lo, g_hi, body, 0)

    def gather_groups(a0, nvalid):
        lead = a0 & (id_group - 1)
        return jnp.where(nvalid > 0, (lead + _gather_rows(nvalid) + id_group - 1) >> _log2(id_group), 0)

    def pieces(nvalid):
        rem = nvalid & (2 * MOE_SUB - 1)
        big = (nvalid >> _log2(2 * MOE_SUB)) + jnp.where(rem > MOE_SUB, 1, 0)
        small = jnp.where((rem > 0) & (rem <= MOE_SUB), 1, 0)
        return big, small

    def covered(nvalid):
        big, small = pieces(nvalid)
        return (2 * big + small) * MOE_SUB_GROUPS

    def out_copy(r, dst_row, src_slot, nrows=1):
        return pltpu.make_async_copy(
            stage.at[src_slot, pl.ds(pl.multiple_of(r * tile, tile), nrows * tile), :],
            y_hbm.at[pl.ds(pl.multiple_of(dst_row * tile, tile), nrows * tile), :],
            osem.at[src_slot])

    nv = nv_ref[i]
    a0 = a0_ref[i]
    nxt = jnp.minimum(i + 1, nsteps - 1)
    nv_n = jnp.where(i + 1 < nsteps, nv_ref[nxt], 0)
    a0_n = a0_ref[nxt]
    groups_n = gather_groups(a0_n, nv_n)
    covered_n = covered(nv)

    @pl.when(i == 0)
    def _():
        rows[...] = jnp.zeros_like(rows)
        issue(tok_ref, a0, 0, gather_groups(a0, nv), 0)

    issue(tokn_ref, a0_n, covered_n, groups_n, 1 - slot)

    prev = jnp.maximum(i - 1, 0)
    covered_here = jnp.where(i > 0, covered(nv_ref[prev]), 0)

    def wait_body(g, carry):
        pltpu.make_async_copy(h_hbm.at[pl.ds(0, id_group * SUBLANES), :],
                              rows.at[slot, pl.ds(0, id_group * SUBLANES), :], gsem.at[slot]).wait()
        return carry
    lax.fori_loop(0, jnp.maximum(gather_groups(a0, nv), covered_here), wait_body, 0)

    half = wg_ref.shape[1] // 2

    def piece(row0, m, g0):
        ng = (m // MOE_SUB) * MOE_SUB_GROUPS

        def start_next(k0, k1):
            for k in range(k0, k1):
                start_group(tokn_ref, a0_n, g0 + k, 1 - slot)

        q = ng // 4
        x_lo, x_hi = _unpack_bf16_pair(_load_row_tiles(rows.at[slot], ROW_PAD + row0, m))
        x_lo = x_lo.astype(BF16)
        x_hi = x_hi.astype(BF16)
        start_next(0, q)
        g = _dot_w(x_lo, wg_ref[0, :half, :])
        start_next(q, 2 * q)
        g = g + _dot_w(x_hi, wg_ref[0, half:, :])
        start_next(2 * q, 3 * q)
        u = _dot_w(x_lo, wu_ref[0, :half, :])
        start_next(3 * q, ng)
        u = u + _dot_w(x_hi, wu_ref[0, half:, :])
        act = (_silu(g) * u).astype(BF16)
        yb = _dot_w(act, wd_ref[0]).astype(BF16).astype(F32)
        _store_row_tiles(stage.at[slot], row0, _pack_bf16_pair(yb[:, :half], yb[:, half:]))

    n_big, n_small = pieces(nv)

    def big_piece(p, carry):
        piece(p * (2 * MOE_SUB), 2 * MOE_SUB, p * (2 * MOE_SUB_GROUPS))
        return carry
    lax.fori_loop(0, n_big, big_piece, 0)

    @pl.when(n_small > 0)
    def _():
        piece(n_big * (2 * MOE_SUB), MOE_SUB, n_big * (2 * MOE_SUB_GROUPS))

    nv_p = jnp.where(i > 0, nv_ref[prev], 0)

    def wait_prev_group(g, carry):
        out_copy(0, 0, 1 - slot, id_group).wait()
        return carry
    lax.fori_loop(0, nv_p >> _log2(id_group), wait_prev_group, 0)

    def wait_prev_row(g, carry):
        out_copy(0, 0, 1 - slot).wait()
        return carry
    lax.fori_loop(0, nv_p & (id_group - 1), wait_prev_row, 0)

    w_out0 = a0 & (LANES - 1)

    def start_out_group(g, carry):
        for u in range(id_group):
            r = g * id_group + u
            out_copy(r, tok_ref[w_out0 + r], slot).start()
        return carry
    lax.fori_loop(0, nv >> _log2(id_group), start_out_group, 0)

    def start_out_row(r, carry):
        out_copy(r, tok_ref[w_out0 + r], slot).start()
        return carry
    lax.fori_loop(nv & ~(id_group - 1), nv, start_out_row, 0)


def _moe(block_e, nvalid, a0, sorted_ids, h2p, w_gate, w_up, w_down, n_tok):
    nblk = block_e.shape[0]
    mb = MOE_BLOCK
    ne, d, f = w_gate.shape
    assert d == 2 * SUBLANES * LANES and mb % (2 * MOE_SUB) == 0 and MOE_SUB % GATHER_GROUP == 0
    assert MOE_SUB_GROUPS % 4 == 0
    assert n_tok & (n_tok - 1) == 0, "token id = assignment id & (n_tok - 1)"

    def win(shift):
        def imap(i, be, nv, a0):
            j = jnp.minimum(i + shift, nblk - 1)
            return (pl.multiple_of((a0[j] >> _log2(LANES)) << _log2(LANES), LANES),)
        return pl.BlockSpec((pl.Element(TOK_WINDOW),), imap, memory_space=pltpu.SMEM)

    wspec = lambda r, c: pl.BlockSpec((1, r, c), lambda i, be, nv, a0: (be[i], 0, 0))
    grid_spec = pltpu.PrefetchScalarGridSpec(
        num_scalar_prefetch=3,
        grid=(nblk,),
        in_specs=[win(0), win(1), pl.BlockSpec(memory_space=pl.ANY), wspec(d, f), wspec(d, f), wspec(f, d)],
        out_specs=pl.BlockSpec(memory_space=pl.ANY),
        scratch_shapes=[pltpu.VMEM((2, (mb + 2 * ROW_PAD) * SUBLANES, LANES), U32),
                        pltpu.VMEM((2, mb * SUBLANES, LANES), U32),
                        pltpu.SemaphoreType.DMA((2,)), pltpu.SemaphoreType.DMA((2,))],
    )
    return pl.pallas_call(
        functools.partial(_moe_kernel, n_tok),
        grid_spec=grid_spec,
        out_shape=jax.ShapeDtypeStruct((TOP_K * n_tok * SUBLANES, LANES), U32),
        compiler_params=_cparams(("arbitrary",)),
        name="moe",
    )(block_e, nvalid, a0, sorted_ids, sorted_ids, h2p, w_gate, w_up, w_down)


def _combine_kernel(*refs):
    y_refs = refs[:TOP_K]
    w_ref, xs1_ref, m5_ref, gf_ref, o_ref = refs[TOP_K:]
    tm = o_ref.shape[0]
    w = w_ref[...]
    acc_lo = acc_hi = None
    for k in range(TOP_K):
        lo, hi = _unpack_bf16_pair(_load_row_tiles(y_refs[k], 0, tm))
        wk = w[:, k:k + 1]
        acc_lo = wk * lo if acc_lo is None else acc_lo + wk * lo
        acc_hi = wk * hi if acc_hi is None else acc_hi + wk * hi
    routed = jnp.concatenate([acc_lo, acc_hi], axis=1)
    x = xs1_ref[...] + m5_ref[0] * routed
    ms = jnp.mean(x * x, axis=-1, keepdims=True)
    o_ref[...] = x * lax.rsqrt(ms + EPS) * gf_ref[...]


def _combine(y_packed, w_tok, xs1, m5, g_final, rows_per_batch, tm):
    n, d = xs1.shape
    assert d == 2 * SUBLANES * LANES
    nt = n // tm
    tpb = rows_per_batch // tm
    y_spec = lambda k: pl.BlockSpec((tm * SUBLANES, LANES), lambda i: (k * nt + i, 0))
    return pl.pallas_call(
        _combine_kernel,
        grid=(nt,),
        in_specs=[y_spec(k) for k in range(TOP_K)]
        + [pl.BlockSpec((tm, TOP_K), lambda i: (i, 0)),
           pl.BlockSpec((tm, d), lambda i: (i, 0)),
           pl.BlockSpec((1, 1, d), lambda i: (i // tpb, 0, 0)),
           pl.BlockSpec((1, d), lambda i: (0, 0))],
        out_specs=pl.BlockSpec((tm, d), lambda i: (i, 0)),
        out_shape=jax.ShapeDtypeStruct((n, d), F32),
        compiler_params=_cparams(("parallel",)),
        name="combine",
    )(*([y_packed] * TOP_K), w_tok, xs1, m5, g_final.reshape(1, d))


def _dispatch_plan(top_e, counts, n_tok):
    mb = MOE_BLOCK
    ne = counts.shape[0]
    n_asg = TOP_K * n_tok
    nblk = -(-(n_asg + ne * (mb - 1)) // mb) + 1
    asg_ids = jnp.arange(n_asg, dtype=I32).reshape(top_e.shape)
    keys = jnp.sort((top_e * n_asg + asg_ids).reshape(-1))
    sorted_ids = jnp.concatenate([keys % n_asg, jnp.zeros((TOK_WINDOW,), I32)])
    start = jnp.cumsum(counts) - counts
    eblk = (counts + mb - 1) // mb
    pend = jnp.cumsum(eblk)
    pstart = pend - eblk
    n_real = pend[-1]
    bid = jnp.arange(nblk, dtype=I32)
    last_real = jnp.maximum(n_real - 1, 0)
    bsrc = jnp.minimum(bid, last_real)
    block_e = jnp.minimum(jnp.sum((pend[None, :] <= bsrc[:, None]).astype(I32), axis=1), ne - 1)
    onehot = block_e[:, None] == jnp.arange(ne, dtype=I32)[None, :]
    look = lambda table: jnp.sum(jnp.where(onehot, table[None, :], 0), axis=1)
    off = (bsrc - look(pstart)) * mb
    nvalid = jnp.where(bid < n_real, jnp.clip(look(counts) - off, 0, mb), 0).astype(I32)
    a0 = (look(start) + off).astype(I32)
    return block_e.astype(I32), nvalid, a0, sorted_ids


def kernel(x, c, ctx, c_ctx, w_ada, b_ada, g_mix, w_in, conv_w, conv_b, dt_bias, a_log, d_skip, g_ssd,
           pool_w, pool_scale, w_out, g_ffn, w_router, router_bias, w_exp_gate, w_exp_up, w_exp_down,
           w_sh_gate, w_sh_up, w_sh_down, g_final):
    bsz, seq, d = x.shape
    ctx_len = ctx.shape[1]
    assert w_ada.shape[0] == 1, "single-layer block"
    d_ssm = g_ssd.shape[1]
    heads = d_skip.shape[1]
    d_pool = pool_scale.shape[1]
    d_xbc = conv_w.shape[2]
    n = bsz * seq

    cvec = jnp.zeros((8, d), F32).at[:bsz].set(c).at[bsz].set(c_ctx)
    mod_all = _ada(cvec, w_ada[0], b_ada[0]).reshape(8, N_MOD, d)
    mod = mod_all[:bsz]
    mod_c = mod_all[bsz:bsz + 1]
    mk = lambda m, k: m[:, k:k + 1, :]

    wt = jnp.transpose(w_in[0])
    c_dt = d_ssm + d_xbc
    wtb = wt.astype(BF16)
    w_pl = wtb[c_dt + 2 * heads:]
    w_dt = wt[c_dt:c_dt + 2 * heads].reshape(2, SSM_GROUPS, heads // SSM_GROUPS, d)
    w_dt = jnp.transpose(w_dt, (1, 0, 2, 3)).reshape(2 * heads, d)
    w_dt = jnp.pad(w_dt, ((0, LANES - 2 * heads), (0, 0)))

    xbc_c, dt_c = _inproj(ctx.reshape(bsz * ctx_len, d), g_mix[0], mk(mod_c, 0), mk(mod_c, 1),
                          wtb, c_dt, w_pl, w_dt, bsz * ctx_len, tm=256, tn=1024, a_rows=(d_ssm, c_dt))
    h_zero = jnp.zeros((bsz, SSM_GROUPS, 2, SSM_STATE, 4 * SSM_HEADDIM), F32)
    h_ctx = _ssd(xbc_c.reshape(bsz, ctx_len, -1), dt_c, conv_w[0], conv_b[0],
                 dt_bias[0], a_log[0], None, h_zero, False, d_ssm, 0)

    x2d = x.reshape(n, d)
    main, dt_raw = _inproj(x2d, g_mix[0], mk(mod, 0), mk(mod, 1), wtb, c_dt, w_pl, w_dt, seq,
                           tm=min(1024, seq), tn=1024)
    main3 = main.reshape(bsz, seq, -1)
    dsk = jnp.repeat(d_skip[0].astype(F32), SSM_HEADDIM).reshape(1, d_ssm)
    yg = _ssd(main3, dt_raw, conv_w[0], conv_b[0], dt_bias[0], a_log[0], dsk, h_ctx, True, d_ssm, d_ssm)
    yp = _pool(main3, pool_w[0], pool_scale[0], d_ssm + d_xbc)

    wo = w_out[0].astype(BF16)
    wr = w_router[0].T
    wr_hi = wr.astype(BF16)
    wr_lo = (wr - wr_hi.astype(F32)).astype(BF16)
    xs1, h2p, lg_t = _outproj(
        yg.reshape(n, d_ssm), yp.reshape(n, d_pool), x2d, g_ssd[0],
        (mk(mod, 2), mk(mod, 3), mk(mod, 4), mk(mod, 5)), g_ffn[0],
        wo[:d_ssm], wo[d_ssm:], wr_hi, wr_lo,
        w_sh_gate[0].astype(BF16), w_sh_up[0].astype(BF16), w_sh_down[0].astype(BF16), seq, tm=256)

    top_e, top_w, cnt = _topk(lg_t, router_bias[0], tm=512)
    counts = cnt[:, 0].astype(I32)
    block_e, nvalid, a0, sorted_ids = _dispatch_plan(top_e, counts, n)
    y_packed = _moe(block_e, nvalid, a0, sorted_ids, h2p, w_exp_gate[0], w_exp_up[0], w_exp_down[0], n)
    out = _combine(y_packed, top_w.T, xs1, mk(mod, 5), g_final, seq, tm=256)
    return out.reshape(bsz, seq, d)
```
